```python
import math
import jax, jax.numpy as jnp
from jax import lax
import numpy as np

D_MODEL = 1024
BATCH = 8
SEQ = 8192
DEPTH = 4

CTX_LEN = 256
GRID_W = 64
N_MIXERS = 3
EPS = 1e-6
ROPE_BASE = 10000.0
BLOCK = 128

A_HEADS = 8
A_DK = 128
A_DV = 128
A_CONV = 5
A_CHUNK = 64
A_QKV = 2 * A_HEADS * A_DK + A_HEADS * A_DV
A_IN = A_QKV + A_HEADS * A_DV + 4 * A_HEADS

B_QHEADS = 16
B_KVHEADS = 4
B_GROUP = B_QHEADS // B_KVHEADS
B_HD = 64
B_WINDOW = 128
B_IN = (B_QHEADS + 2 * B_KVHEADS) * B_HD

C_HEADS = 8
C_HD = 64
C_QK = C_HEADS * 2 * C_HD
C_IN = 3 * C_QK

N_EXPERTS = 16
EC_CAPACITY = 2
D_EXPERT = 1024

N_A = (DEPTH + N_MIXERS - 1) // N_MIXERS
N_B = (DEPTH + N_MIXERS - 2) // N_MIXERS
N_C = DEPTH // N_MIXERS

kernel_name = "hybrid_diffusion_delta_swa_diff_ec"

F32 = jnp.float32


def rmsnorm(x, g):
    xf = x.astype(F32)
    y = xf * lax.rsqrt(jnp.mean(xf * xf, -1, keepdims=True) + EPS)
    return (y * g.astype(F32)).astype(x.dtype)


def l2norm(x):
    return x * lax.rsqrt(jnp.sum(x * x, -1, keepdims=True) + EPS)


def modulate(h, shift, scale):
    return h * (1 + scale) + shift


def rope_tables(n_tok, head_dim):
    rows = n_tok // GRID_W
    t_row = jnp.broadcast_to(jnp.arange(rows)[:, None], (rows, GRID_W)).reshape(-1).astype(F32)
    t_col = jnp.broadcast_to(jnp.arange(GRID_W)[None, :], (rows, GRID_W)).reshape(-1).astype(F32)
    n_freq = head_dim // 4
    inv = ROPE_BASE ** (-jnp.arange(n_freq, dtype=F32) / n_freq)
    ang = jnp.concatenate([t_row[:, None] * inv, t_col[:, None] * inv], -1)
    return jnp.cos(ang), jnp.sin(ang)


def apply_rope(x, cos, sin):
    half = x.shape[-1] // 2
    shape = (cos.shape[0],) + (1,) * (x.ndim - 3) + (half,)
    cs, sn = cos.reshape(shape), sin.reshape(shape)
    x1, x2 = x[..., :half].astype(F32), x[..., half:].astype(F32)
    return jnp.concatenate([x1 * cs - x2 * sn, x2 * cs + x1 * sn], -1).astype(x.dtype)


def short_conv(x, w):
    pad = w.shape[0] // 2
    return lax.conv_general_dilated(x, w[:, None, :], window_strides=(1,), padding=[(pad, pad)],
                                    dimension_numbers=('NWC', 'WIO', 'NWC'), feature_group_count=x.shape[-1])


def gated_delta(q, k, v, g, beta, s0):
    bn, h, n, _ = q.shape
    dv = v.shape[-1]
    nc = n // A_CHUNK
    ch = lambda t: t.reshape(bn, h, nc, A_CHUNK, *t.shape[3:])
    q, k, v, g, beta = ch(q), ch(k), ch(v), ch(g), ch(beta)
    gcum = jnp.cumsum(g, -1)
    ar = jnp.arange(A_CHUNK)
    incl = ar[:, None] >= ar[None, :]
    strict = ar[:, None] > ar[None, :]
    decay = jnp.exp(jnp.where(incl, gcum[..., :, None] - gcum[..., None, :], -jnp.inf))
    a_mat = jnp.where(strict, beta[..., :, None] * jnp.einsum('bhcid,bhcjd->bhcij', k, k) * decay, 0.0)
    rhs = jnp.concatenate([v * beta[..., None], k * (beta * jnp.exp(gcum))[..., None]], -1)
    sol = lax.linalg.triangular_solve(a_mat, rhs, left_side=True, lower=True, unit_diagonal=True)
    u, w = sol[..., :dv], sol[..., dv:]
    qk = jnp.einsum('bhcid,bhcjd->bhcij', q, k) * decay

    def step(s, inp):
        q_c, k_c, u_c, w_c, g_c, qk_c = inp
        v_new = u_c - jnp.einsum('bhld,bhde->bhle', w_c, s)
        o_c = (jnp.einsum('bhld,bhde->bhle', q_c * jnp.exp(g_c)[..., None], s)
               + jnp.einsum('bhij,bhje->bhie', qk_c, v_new))
        g_last = g_c[..., -1:]
        s = s * jnp.exp(g_last)[..., None] + jnp.einsum('bhld,bhle->bhde', k_c * jnp.exp(g_last - g_c)[..., None], v_new)
        return s, o_c

    xs = tuple(jnp.moveaxis(t, 2, 0) for t in (q, k, u, w, gcum, qk))
    s_fin, o = lax.scan(step, s0, xs)
    return jnp.moveaxis(o, 0, 2).reshape(bn, h, n, dv), s_fin


def delta_inputs(h, w_in, conv_w, a_log, dt_bias):
    bn, n, _ = h.shape
    p = h @ w_in
    qkv = jax.nn.silu(short_conv(p[..., :A_QKV], conv_w)).astype(F32)
    z = p[..., A_QKV:A_QKV + A_HEADS * A_DV]
    ab = p[..., A_QKV + A_HEADS * A_DV:].astype(F32).reshape(bn, n, 2, 2, A_HEADS)
    q = l2norm(qkv[..., :A_HEADS * A_DK].reshape(bn, n, A_HEADS, A_DK)) * (A_DK ** -0.5)
    k = l2norm(qkv[..., A_HEADS * A_DK:2 * A_HEADS * A_DK].reshape(bn, n, A_HEADS, A_DK))
    v = qkv[..., 2 * A_HEADS * A_DK:].reshape(bn, n, A_HEADS, A_DV)
    g = -jnp.exp(a_log.astype(F32)) * jax.nn.softplus(ab[:, :, 0] + dt_bias.astype(F32))
    beta = jax.nn.sigmoid(ab[:, :, 1])
    bh = lambda t: jnp.moveaxis(t, 1, 2)
    return bh(q), bh(k), bh(v), z, jnp.moveaxis(g, 1, 3), jnp.moveaxis(beta, 1, 3)


def delta_out(o, z, g_out, w_out, dtype):
    o = jnp.moveaxis(o, 1, 2)
    bn, n = o.shape[:2]
    o = rmsnorm(o, g_out) * jax.nn.silu(z.reshape(bn, n, A_HEADS, A_DV).astype(F32))
    return o.reshape(bn, n, A_HEADS * A_DV).astype(dtype) @ w_out


def mixer_delta(hx, hc, w_in, conv_w, a_log, dt_bias, g_out, w_out, need_ctx):
    qx, kx, vx, zx, gx, bx = delta_inputs(hx, w_in, conv_w, a_log, dt_bias)
    qc, kc, vc, zc, gc, bc = delta_inputs(hc, w_in, conv_w, a_log, dt_bias)
    s0 = jnp.zeros((hx.shape[0], A_HEADS, A_DK, A_DV), F32)
    ident = lambda t: t
    rev = lambda t: jnp.flip(t, 2)
    oc_f, sc_f = gated_delta(qc, kc, vc, gc[:, 0], bc[:, 0], s0)
    ox_f, _ = gated_delta(qx, kx, vx, gx[:, 0], bx[:, 0], sc_f)
    oc_b, sc_b = gated_delta(rev(qc), rev(kc), rev(vc), rev(gc[:, 1]), rev(bc[:, 1]), s0)
    ox_b, _ = gated_delta(rev(qx), rev(kx), rev(vx), rev(gx[:, 1]), rev(bx[:, 1]), sc_b)
    yx = delta_out(ident(ox_f) + rev(ox_b), zx, g_out, w_out, hx.dtype)
    yc = delta_out(oc_f + rev(oc_b), zc, g_out, w_out, hc.dtype) if need_ctx else None
    return yc, yx


def swa_project(h, w_in, qn, kn):
    bn, n, _ = h.shape
    p = h @ w_in
    q = rmsnorm(p[..., :B_QHEADS * B_HD].reshape(bn, n, B_KVHEADS, B_GROUP, B_HD), qn)
    k = rmsnorm(p[..., B_QHEADS * B_HD:(B_QHEADS + B_KVHEADS) * B_HD].reshape(bn, n, B_KVHEADS, B_HD), kn)
    v = p[..., (B_QHEADS + B_KVHEADS) * B_HD:].reshape(bn, n, B_KVHEADS, B_HD)
    return q, k, v


def sink_attend(q, k, v, sink, mask):
    s = jnp.einsum('bqkgd,bskd->bkgqs', q, k).astype(F32) * (B_HD ** -0.5)
    if mask is not None:
        s = jnp.where(mask, s, -jnp.inf)
    sk = jnp.broadcast_to(sink.astype(F32)[None, :, :, None, None], s.shape[:-1] + (1,))
    p = jax.nn.softmax(jnp.concatenate([s, sk], -1), -1)[..., :-1]
    return jnp.einsum('bkgqs,bskd->bqkgd', p.astype(v.dtype), v)


def mixer_swa(hx, hc, w_in, qn, kn, sink, w_out, cos, sin, need_ctx):
    bn, n, _ = hx.shape
    ctx_len = hc.shape[1]
    qx, kx, vx = swa_project(hx, w_in, qn, kn)
    qx, kx = apply_rope(qx, cos, sin), apply_rope(kx, cos, sin)
    qc, kc, vc = swa_project(hc, w_in, qn, kn)
    sink = sink.reshape(B_KVHEADS, B_GROUP)
    pad = ((0, 0), (BLOCK, BLOCK), (0, 0), (0, 0))
    kpad, vpad = jnp.pad(kx, pad), jnp.pad(vx, pad)
    ar_q = jnp.arange(BLOCK)
    ar_k = jnp.arange(3 * BLOCK) - BLOCK

    def block(i):
        start = i * BLOCK
        qb = lax.dynamic_slice_in_dim(qx, start, BLOCK, axis=1)
        kb = jnp.concatenate([lax.dynamic_slice_in_dim(kpad, start, 3 * BLOCK, axis=1), kc], 1)
        vb = jnp.concatenate([lax.dynamic_slice_in_dim(vpad, start, 3 * BLOCK, axis=1), vc], 1)
        t = start + ar_q
        sp = start + ar_k
        win = (jnp.abs(t[:, None] - sp[None, :]) <= B_WINDOW) & (sp >= 0)[None, :] & (sp < n)[None, :]
        mask = jnp.concatenate([win, jnp.ones((BLOCK, ctx_len), bool)], 1)
        return sink_attend(qb, kb, vb, sink, mask)

    ox = lax.map(block, jnp.arange(n // BLOCK))
    yx = jnp.moveaxis(ox, 0, 1).reshape(bn, n, B_QHEADS * B_HD) @ w_out
    yc = sink_attend(qc, kc, vc, sink, None).reshape(bn, ctx_len, B_QHEADS * B_HD) @ w_out if need_ctx else None
    return yc, yx


def diff_project(h, w_in, qn, kn):
    bn, n, _ = h.shape
    p = h @ w_in
    q = rmsnorm(p[..., :C_QK].reshape(bn, n, C_HEADS, 2, C_HD), qn)
    k = rmsnorm(p[..., C_QK:2 * C_QK].reshape(bn, n, C_HEADS, 2, C_HD), kn)
    v = p[..., 2 * C_QK:].reshape(bn, n, C_HEADS, 2 * C_HD)
    return q, k, v


def diff_attend(q, k, v, lam):
    s = jnp.einsum('bqhmd,bshmd->bhmqs', q, k).astype(F32) * (C_HD ** -0.5)
    p = jax.nn.softmax(s, -1)
    w = p[:, :, 0] - lam * p[:, :, 1]
    return jnp.einsum('bhqs,bshe->bqhe', w.astype(v.dtype), v)


def mixer_diff(hx, hc, w_in, qn, kn, lam_vecs, g_sub, w_out, cos, sin, lam_init, need_ctx):
    bn, n, _ = hx.shape
    lv = lam_vecs.astype(F32)
    lam = jnp.exp(jnp.sum(lv[0] * lv[1])) - jnp.exp(jnp.sum(lv[2] * lv[3])) + lam_init
    qx, kx, vx = diff_project(hx, w_in, qn, kn)
    qx, kx = apply_rope(qx, cos, sin), apply_rope(kx, cos, sin)
    qc, kc, vc = diff_project(hc, w_in, qn, kn)
    kcat = jnp.concatenate([kx, kc], 1)
    vcat = jnp.concatenate([vx, vc], 1)

    def block(i):
        qb = lax.dynamic_slice_in_dim(qx, i * BLOCK, BLOCK, axis=1)
        return diff_attend(qb, kcat, vcat, lam)

    def finish(o):
        return (rmsnorm(o, g_sub) * (1.0 - lam_init)).reshape(bn, o.shape[1], C_HEADS * 2 * C_HD) @ w_out

    ox = lax.map(block, jnp.arange(n // BLOCK))
    yx = finish(jnp.moveaxis(ox, 0, 1).reshape(bn, n, C_HEADS, 2 * C_HD))
    yc = finish(diff_attend(qc, kc, vc, lam)) if need_ctx else None
    return yc, yx


def ec_moe(h, w_router, w_gate_up, w_down):
    bn, n, d = h.shape
    cap = EC_CAPACITY * n // N_EXPERTS
    aff = jax.nn.softmax((h @ w_router).astype(F32), -1)
    gate, idx = lax.top_k(jnp.swapaxes(aff, 1, 2), cap)
    xg = jax.vmap(lambda hb, ib: hb[ib])(h, idx)
    gu = jnp.einsum('becd,edf->becf', xg, w_gate_up)
    act = jax.nn.silu(gu[..., :D_EXPERT]) * gu[..., D_EXPERT:]
    y = jnp.einsum('becf,efd->becd', act, w_down) * gate[..., None].astype(h.dtype)
    return jax.vmap(lambda ib, yb: jnp.zeros((n, d), yb.dtype).at[ib.reshape(-1)].add(yb.reshape(-1, d)))(idx, y)


def setup_inputs(seed: int = 0) -> dict:
    key = jax.random.key(seed)
    ks = jax.random.split(key, 32)
    D = D_MODEL

    def nrm(i, shape, scale):
        return jax.random.normal(ks[i], shape, F32) * scale

    dt = jnp.exp(jax.random.uniform(ks[10], (N_A, 2, A_HEADS), F32, minval=math.log(1e-3), maxval=math.log(1e-1)))
    return {
        "x": nrm(0, (BATCH, SEQ, D), 1.0),
        "c": nrm(1, (BATCH, D), 1.0),
        "ctx": nrm(2, (BATCH, CTX_LEN, D), 1.0),
        "c_ctx": nrm(3, (D,), 1.0),
        "w_ada": nrm(4, (DEPTH, D, 6 * D), 0.5 * D ** -0.5),
        "b_ada": nrm(5, (DEPTH, 6 * D), 0.02),
        "g_mix": 1.0 + nrm(6, (DEPTH, D), 0.02),
        "g_ffn": 1.0 + nrm(7, (DEPTH, D), 0.02),
        "a_w_in": nrm(8, (N_A, D, A_IN), D ** -0.5),
        "a_conv": nrm(9, (N_A, A_CONV, A_QKV), A_CONV ** -0.5),
        "a_log": jnp.log(jax.random.uniform(ks[11], (N_A, 2, A_HEADS), F32, minval=1.0, maxval=16.0)),
        "a_dt_bias": dt + jnp.log(-jnp.expm1(-dt)),
        "a_g_out": 1.0 + nrm(12, (N_A, A_DV), 0.02),
        "a_w_out": nrm(13, (N_A, A_HEADS * A_DV, D), (A_HEADS * A_DV) ** -0.5),
        "b_w_in": nrm(14, (N_B, D, B_IN), D ** -0.5),
        "b_q_norm": 1.0 + nrm(15, (N_B, B_HD), 0.02),
        "b_k_norm": 1.0 + nrm(16, (N_B, B_HD), 0.02),
        "b_sink": nrm(17, (N_B, B_QHEADS), 1.0),
        "b_w_out": nrm(18, (N_B, B_QHEADS * B_HD, D), (B_QHEADS * B_HD) ** -0.5),
        "c_w_in": nrm(19, (N_C, D, C_IN), D ** -0.5),
        "c_q_norm": 1.0 + nrm(20, (N_C, C_HD), 0.02),
        "c_k_norm": 1.0 + nrm(21, (N_C, C_HD), 0.02),
        "c_lambda": nrm(22, (N_C, 4, C_HD), 0.1),
        "c_g_sub": 1.0 + nrm(23, (N_C, 2 * C_HD), 0.02),
        "c_w_out": nrm(24, (N_C, C_HEADS * 2 * C_HD, D), (C_HEADS * 2 * C_HD) ** -0.5),
        "w_router": nrm(25, (DEPTH, D, N_EXPERTS), D ** -0.5),
        "w_gate_up": nrm(26, (DEPTH, N_EXPERTS, D, 2 * D_EXPERT), D ** -0.5),
        "w_down": nrm(27, (DEPTH, N_EXPERTS, D_EXPERT, D), D_EXPERT ** -0.5),
    }


def reference(x, c, ctx, c_ctx, w_ada, b_ada, g_mix, g_ffn, a_w_in, a_conv, a_log, a_dt_bias, a_g_out, a_w_out,
              b_w_in, b_q_norm, b_k_norm, b_sink, b_w_out, c_w_in, c_q_norm, c_k_norm, c_lambda, c_g_sub, c_w_out,
              w_router, w_gate_up, w_down):
    n = x.shape[1]
    cos_b, sin_b = rope_tables(n, B_HD)
    cos_c, sin_c = rope_tables(n, C_HD)
    silu_c = jax.nn.silu(c)
    silu_cc = jax.nn.silu(c_ctx)
    for l in range(DEPTH):
        last = l == DEPTH - 1
        mx = jnp.split((silu_c @ w_ada[l] + b_ada[l])[:, None, :], 6, axis=-1)
        mc = jnp.split(silu_cc @ w_ada[l] + b_ada[l], 6, axis=-1)
        hx = modulate(rmsnorm(x, g_mix[l]), mx[0], mx[1])
        hc = modulate(rmsnorm(ctx, g_mix[l]), mc[0], mc[1])
        kind, j = l % N_MIXERS, l // N_MIXERS
        if kind == 0:
            yc, yx = mixer_delta(hx, hc, a_w_in[j], a_conv[j], a_log[j], a_dt_bias[j], a_g_out[j], a_w_out[j], not last)
        elif kind == 1:
            yc, yx = mixer_swa(hx, hc, b_w_in[j], b_q_norm[j], b_k_norm[j], b_sink[j], b_w_out[j], cos_b, sin_b, not last)
        else:
            lam_init = 0.8 - 0.6 * math.exp(-0.3 * l)
            yc, yx = mixer_diff(hx, hc, c_w_in[j], c_q_norm[j], c_k_norm[j], c_lambda[j], c_g_sub[j], c_w_out[j],
                                cos_c, sin_c, lam_init, not last)
        x = x + mx[2] * yx
        x = x + mx[5] * ec_moe(modulate(rmsnorm(x, g_ffn[l]), mx[3], mx[4]), w_router[l], w_gate_up[l], w_down[l])
        if not last:
            ctx = ctx + mc[2] * yc
            ctx = ctx + mc[5] * ec_moe(modulate(rmsnorm(ctx, g_ffn[l]), mc[3], mc[4]), w_router[l], w_gate_up[l], w_down[l])
    return x
```

```python
import functools
import math

import jax
import jax.numpy as jnp
from jax import lax
from jax.experimental import pallas as pl
from jax.experimental.pallas import tpu as pltpu

F32 = jnp.float32
BF16 = jnp.bfloat16
I32 = jnp.int32

EPS = 1e-6
ROPE_BASE = 10000.0
GRID_W = 64
N_MIXERS = 3

A_HEADS, A_DK, A_DV, A_CHUNK = 8, 128, 128, 64
B_QHEADS, B_KVHEADS, B_HD = 16, 4, 64
B_GROUP = B_QHEADS // B_KVHEADS
C_HEADS, C_HD = 8, 64
N_EXPERTS, EC_CAPACITY = 16, 2

LANES = 128
TM = 256
TQ = 128
NEG = -1e30
VMEM_LIMIT = 56 * 1024 * 1024


def _cp(sem, vmem=None):
    return pltpu.CompilerParams(dimension_semantics=sem, vmem_limit_bytes=vmem)


def _dot(a, b):
    return jnp.dot(a, b, preferred_element_type=F32)


def _dot_nt(a, b):
    return lax.dot_general(a, b, (((1,), (1,)), ((), ())), preferred_element_type=F32)


def _split2(a):
    hi = a.astype(BF16)
    return hi, (a - hi.astype(F32)).astype(BF16)


def _sigmoid(x):
    return 1.0 / (1.0 + jnp.exp(-x))


def _silu(x):
    return x * _sigmoid(x)


def _softplus(x):
    return jnp.maximum(x, 0.0) + jnp.log(1.0 + jnp.exp(-jnp.abs(x)))


def _norm_mod(x, g, mod, i_shift, i_scale):
    ms = jnp.mean(x * x, axis=-1, keepdims=True)
    y = x * lax.rsqrt(ms + EPS) * g
    return y * (1.0 + mod[i_scale:i_scale + 1]) + mod[i_shift:i_shift + 1]


def _iota(shape, dim):
    return lax.broadcasted_iota(I32, shape, dim)


def _ada_kernel(s_ref, w_ref, b_ref, o_ref):
    s = _silu(s_ref[...])
    s_hi, s_lo = _split2(s)
    w_hi, w_lo = _split2(w_ref[0])
    o_ref[0] = _dot(s_hi, w_hi) + _dot(s_hi, w_lo) + _dot(s_lo, w_hi) + b_ref[0]


def _adaln(c, c_ctx, w_ada, b_ada):
    depth, d, d6 = w_ada.shape
    bn = c.shape[0]
    rows_n = -(-(bn + 1) // 8) * 8
    rows = jnp.zeros((rows_n, d), F32).at[:bn].set(c).at[bn].set(c_ctx)
    nb = d6 // 4
    out = pl.pallas_call(
        _ada_kernel,
        grid=(depth, d6 // nb),
        in_specs=[pl.BlockSpec((rows_n, d), lambda l, j: (0, 0)),
                  pl.BlockSpec((1, d, nb), lambda l, j: (l, 0, j)),
                  pl.BlockSpec((1, 1, nb), lambda l, j: (l, 0, j))],
        out_specs=pl.BlockSpec((1, rows_n, nb), lambda l, j: (l, 0, j)),
        out_shape=jax.ShapeDtypeStruct((depth, rows_n, d6), F32),
        compiler_params=_cp(("arbitrary", "arbitrary"), VMEM_LIMIT),
        name="adaln",
    )(rows, w_ada, b_ada.reshape(depth, 1, d6))
    mx = out[:, :bn].reshape(depth, bn, 6, d)
    mc = jnp.broadcast_to(out[:, bn].reshape(depth, 1, 6, d), (depth, bn, 6, d))
    return jnp.stack([mc, mx], axis=2)


def _seg(i, nct):
    return jnp.where(i < nct, 0, 1)


def _proj_qk_kernel(x_ref, mod_ref, g_ref, w_ref, gain_ref, cos_ref, sin_ref, qk_ref, v_ref, *, nqk):
    h = _norm_mod(x_ref[0], g_ref[...], mod_ref[0, 0], 0, 1)
    p = _dot(h.astype(BF16), w_ref[...])
    grp = (_iota((LANES, LANES), 0) // 64 == _iota((LANES, LANES), 1) // 64).astype(BF16)
    first = (_iota((1, LANES), 1) % 64) < 32
    cs = cos_ref[...]
    sn = sin_ref[...]
    for t in range(nqk // LANES):
        sl = slice(t * LANES, (t + 1) * LANES)
        xt = p[:, sl]
        sq_hi, sq_lo = _split2(xt * xt)
        ms = (_dot(sq_hi, grp) + _dot(sq_lo, grp)) * (1.0 / 64)
        y = xt * lax.rsqrt(ms + EPS) * gain_ref[:, sl]
        rot = jnp.where(first, pltpu.roll(y, 96, 1), pltpu.roll(y, 32, 1))
        qk_ref[0, :, sl] = (y * cs + rot * sn).astype(BF16)
    v_ref[0] = p[:, nqk:].astype(BF16)


def _proj_qk(xs, mod, g, w, gain, cosf, sinf, nqk, nct_m):
    bn, t, d = xs.shape
    nout = w.shape[1]
    return pl.pallas_call(
        functools.partial(_proj_qk_kernel, nqk=nqk),
        grid=(bn, t // TM),
        in_specs=[pl.BlockSpec((1, TM, d), lambda b, i: (b, i, 0)),
                  pl.BlockSpec((1, 1, 6, d), lambda b, i: (b, _seg(i, nct_m), 0, 0)),
                  pl.BlockSpec((1, d), lambda b, i: (0, 0)),
                  pl.BlockSpec((d, nout), lambda b, i: (0, 0)),
                  pl.BlockSpec((1, nqk), lambda b, i: (0, 0)),
                  pl.BlockSpec((TM, LANES), lambda b, i: (i, 0)),
                  pl.BlockSpec((TM, LANES), lambda b, i: (i, 0))],
        out_specs=[pl.BlockSpec((1, TM, nqk), lambda b, i: (b, i, 0)),
                   pl.BlockSpec((1, TM, nout - nqk), lambda b, i: (b, i, 0))],
        out_shape=[jax.ShapeDtypeStruct((bn, t, nqk), BF16),
                   jax.ShapeDtypeStruct((bn, t, nout - nqk), BF16)],
        compiler_params=_cp(("arbitrary", "arbitrary"), VMEM_LIMIT),
        name="proj_qk",
    )(xs, mod, g, w, gain, cosf, sinf)


def _rope_tables(n, l, head_dim):
    t = jnp.arange(n)
    n_freq = head_dim // 4
    inv = ROPE_BASE ** (-jnp.arange(n_freq, dtype=F32) / n_freq)
    ang = jnp.concatenate([(t // GRID_W).astype(F32)[:, None] * inv, (t % GRID_W).astype(F32)[:, None] * inv], -1)
    cs, sn = jnp.cos(ang), jnp.sin(ang)
    reps = LANES // head_dim
    cosf = jnp.tile(jnp.concatenate([cs, cs], -1), (1, reps))
    sinf = jnp.tile(jnp.concatenate([-sn, sn], -1), (1, reps))
    cosf = jnp.concatenate([jnp.ones((l, LANES), F32), cosf], 0)
    sinf = jnp.concatenate([jnp.zeros((l, LANES), F32), sinf], 0)
    return cosf, sinf


def _swa_kernel(q_ref, kp_ref, kc_ref, kn_ref, kx_ref, vp_ref, vc_ref, vn_ref, vx_ref, sink_ref, o_ref, *, nct, ntx, l):
    xi = pl.program_id(2) - nct
    q = q_ref[0]
    lo = _iota((TQ, LANES), 1) < 64
    zero = jnp.zeros((TQ, LANES), BF16)
    qa, qb = q[:, :LANES], q[:, LANES:]
    q4 = jnp.concatenate([jnp.where(lo, qa, zero), jnp.where(lo, zero, qa),
                          jnp.where(lo, qb, zero), jnp.where(lo, zero, qb)], axis=0)
    kcat = jnp.concatenate([kp_ref[0], kc_ref[0], kn_ref[0], kx_ref[0]], axis=0)
    s = _dot_nt(q4, kcat)
    nk = 3 * TQ + l
    r = _iota((TQ, nk), 0)
    c = _iota((TQ, nk), 1)
    is_x = xi >= 0
    m_prev = (c < TQ) & (c >= r) & (xi >= 1)
    m_cur = (c >= TQ) & (c < 2 * TQ) & is_x
    m_next = (c >= 2 * TQ) & (c < 3 * TQ) & (c - 2 * TQ <= r) & (xi + 1 < ntx) & is_x
    mask = m_prev | m_cur | m_next | (c >= 3 * TQ)
    s = jnp.where(jnp.concatenate([mask] * B_GROUP, axis=0), s, NEG)
    sk = sink_ref[0]
    m = jnp.maximum(jnp.max(s, axis=-1, keepdims=True), sk)
    p = jnp.exp(s - m)
    den = jnp.sum(p, axis=-1, keepdims=True) + jnp.exp(sk - m)
    vcat = jnp.concatenate([vp_ref[0], vc_ref[0], vn_ref[0], vx_ref[0]], axis=0)
    o4 = _dot(p.astype(BF16), vcat) / den
    oa = jnp.where(lo, o4[0:TQ], o4[TQ:2 * TQ])
    ob = jnp.where(lo, o4[2 * TQ:3 * TQ], o4[3 * TQ:4 * TQ])
    o_ref[0] = jnp.concatenate([oa, ob], axis=1).astype(BF16)


def _swa_attention(qk, v2, sink, l):
    bn, t, _ = qk.shape
    nt = t // TQ
    nct = l // TQ
    gw = B_GROUP * B_HD
    kcol = B_QHEADS * B_HD // LANES
    prev = lambda b, kv, i: jnp.maximum(i - 1, 0)
    nxt = lambda b, kv, i: jnp.minimum(i + 1, nt - 1)
    kspec = lambda f: pl.BlockSpec((1, TQ, LANES), lambda b, kv, i: (b, f(b, kv, i), kcol + kv))
    vspec = lambda f: pl.BlockSpec((1, TQ, LANES), lambda b, kv, i: (b, f(b, kv, i), kv))
    cur = lambda b, kv, i: i
    sinkcol = jnp.repeat(sink.reshape(B_KVHEADS, B_GROUP), TQ, axis=1).reshape(B_KVHEADS, B_GROUP * TQ, 1).astype(F32)
    return pl.pallas_call(
        functools.partial(_swa_kernel, nct=nct, ntx=nt - nct, l=l),
        grid=(bn, B_KVHEADS, nt),
        in_specs=[pl.BlockSpec((1, TQ, gw), lambda b, kv, i: (b, i, kv)),
                  kspec(prev), kspec(cur), kspec(nxt),
                  pl.BlockSpec((1, l, LANES), lambda b, kv, i: (b, 0, kcol + kv)),
                  vspec(prev), vspec(cur), vspec(nxt),
                  pl.BlockSpec((1, l, LANES), lambda b, kv, i: (b, 0, kv)),
                  pl.BlockSpec((1, B_GROUP * TQ, 1), lambda b, kv, i: (kv, 0, 0))],
        out_specs=pl.BlockSpec((1, TQ, gw), lambda b, kv, i: (b, i, kv)),
        out_shape=jax.ShapeDtypeStruct((bn, t, B_QHEADS * B_HD), BF16),
        compiler_params=_cp(("arbitrary", "arbitrary", "arbitrary"), VMEM_LIMIT),
        name="swa_attention",
    )(qk, qk, qk, qk, qk, v2, v2, v2, v2, sinkcol)


def _diff_kernel(lam_ref, gsub_ref, q_ref, k_ref, v_ref, o_ref, *, nct, l, t, ck, lam_init):
    i = pl.program_id(2)
    q = q_ref[0]
    lo = _iota((TQ, LANES), 1) < 64
    zero = jnp.zeros((TQ, LANES), BF16)
    q2 = jnp.concatenate([jnp.where(lo, q, zero), jnp.where(lo, zero, q)], axis=0)

    def chunk(kc, vc, carry):
        m, den, acc = carry
        s = _dot_nt(q2, kc)
        m2 = jnp.maximum(m, jnp.max(s, axis=-1, keepdims=True))
        a = jnp.exp(m - m2)
        p = jnp.exp(s - m2)
        den = a * den + jnp.sum(p, axis=-1, keepdims=True)
        acc = a * acc + _dot(p.astype(BF16), vc)
        return m2, den, acc

    init = (jnp.full((2 * TQ, 1), NEG, F32), jnp.zeros((2 * TQ, 1), F32), jnp.zeros((2 * TQ, LANES), F32))

    def finish(carry):
        _, den, acc = carry
        lv = lam_ref[...]
        lam = (jnp.exp(jnp.sum(lv[0:1] * lv[1:2], axis=-1, keepdims=True))
               - jnp.exp(jnp.sum(lv[2:3] * lv[3:4], axis=-1, keepdims=True)) + lam_init)
        o = acc[0:TQ] / den[0:TQ] - lam * (acc[TQ:] / den[TQ:])
        ms = jnp.mean(o * o, axis=-1, keepdims=True)
        o_ref[0] = (o * lax.rsqrt(ms + EPS) * gsub_ref[...] * (1.0 - lam_init)).astype(BF16)

    @pl.when(i < nct)
    def _():
        finish(chunk(k_ref[0, 0:l, :], v_ref[0, 0:l, :], init))

    @pl.when(i >= nct)
    def _():
        def body(j, carry):
            st = pl.multiple_of(j * ck, ck)
            return chunk(k_ref[0, pl.ds(st, ck), :], v_ref[0, pl.ds(st, ck), :], carry)
        finish(lax.fori_loop(0, t // ck, body, init))


def _diff_attention(qk, v, lam_vecs, g_sub, l, lam_init):
    bn, t, _ = qk.shape
    nt = t // TQ
    ck = 768 if t % 768 == 0 else TQ
    kcol = C_HEADS * 2 * C_HD // LANES
    return pl.pallas_call(
        functools.partial(_diff_kernel, nct=l // TQ, l=l, t=t, ck=ck, lam_init=lam_init),
        grid=(bn, C_HEADS, nt),
        in_specs=[pl.BlockSpec((4, C_HD), lambda b, h, i: (0, 0)),
                  pl.BlockSpec((1, LANES), lambda b, h, i: (0, 0)),
                  pl.BlockSpec((1, TQ, LANES), lambda b, h, i: (b, i, h)),
                  pl.BlockSpec((1, t, LANES), lambda b, h, i: (b, 0, kcol + h)),
                  pl.BlockSpec((1, t, LANES), lambda b, h, i: (b, 0, h))],
        out_specs=pl.BlockSpec((1, TQ, LANES), lambda b, h, i: (b, i, h)),
        out_shape=jax.ShapeDtypeStruct((bn, t, C_HEADS * 2 * C_HD), BF16),
        compiler_params=_cp(("arbitrary", "arbitrary", "arbitrary"), VMEM_LIMIT),
        name="diff_attention",
    )(lam_vecs.astype(F32), g_sub.reshape(1, LANES).astype(F32), qk, qk, v)


def _proj_delta_kernel(x_ref, mod_ref, g_ref, w_ref, wab_ref, wabt_ref, alog_ref, dtb_ref, alogt_ref, dtbt_ref,
                       p_ref, z_ref, gc_ref, gct_ref, beta_ref, *, nqkv):
    h = _norm_mod(x_ref[0], g_ref[...], mod_ref[0, 0], 0, 1)
    h_hi, h_lo = _split2(h)
    p = _dot(h_hi, w_ref[...])
    p_ref[0] = p[:, :nqkv]
    z_ref[0] = p[:, nqkv:].astype(BF16)
    ab = _dot(h_hi, wab_ref[0]) + _dot(h_hi, wab_ref[1]) + _dot(h_lo, wab_ref[0])
    abt = _dot_nt(wabt_ref[0], h_hi) + _dot_nt(wabt_ref[1], h_hi) + _dot_nt(wabt_ref[0], h_lo)
    nd = 2 * A_HEADS
    g = -jnp.exp(alog_ref[...]) * _softplus(ab[:, :nd] + dtb_ref[...])
    beta_ref[0] = _sigmoid(ab[:, nd:])
    gt = -jnp.exp(alogt_ref[...]) * _softplus(abt[:nd] + dtbt_ref[...])
    ri = _iota((TM, TM), 0)
    ci = _iota((TM, TM), 1)
    same = (ri // A_CHUNK) == (ci // A_CHUNK)
    lbd = (same & (ci <= ri)).astype(BF16)
    ubd = (same & (ci >= ri)).astype(BF16)
    g_hi, g_lo = _split2(g)
    pre = _dot(lbd, g_hi) + _dot(lbd, g_lo)
    suf = _dot(ubd, g_hi) + _dot(ubd, g_lo)
    gc_ref[0] = jnp.where(_iota((TM, nd), 1) < A_HEADS, pre, suf)
    gt_hi, gt_lo = _split2(gt)
    pre_t = _dot(gt_hi, ubd) + _dot(gt_lo, ubd)
    suf_t = _dot(gt_hi, lbd) + _dot(gt_lo, lbd)
    gct_ref[0] = jnp.where(_iota((nd, TM), 0) < A_HEADS, pre_t, suf_t)


def _proj_delta(xs, mod, g, w_main, wab, wabt, a_log, dt_bias, nct_m):
    bn, t, d = xs.shape
    nout = w_main.shape[1]
    nqkv = 2 * A_HEADS * A_DK + A_HEADS * A_DV
    nd = 2 * A_HEADS
    row = lambda a: a.reshape(1, nd).astype(F32)
    col = lambda a: a.reshape(nd, 1).astype(F32)
    full = lambda shape: pl.BlockSpec(shape, lambda b, i: (0,) * len(shape))
    return pl.pallas_call(
        functools.partial(_proj_delta_kernel, nqkv=nqkv),
        grid=(bn, t // TM),
        in_specs=[pl.BlockSpec((1, TM, d), lambda b, i: (b, i, 0)),
                  pl.BlockSpec((1, 1, 6, d), lambda b, i: (b, _seg(i, nct_m), 0, 0)),
                  full((1, d)), full((d, nout)), full((2, d, 2 * nd)), full((2, 2 * nd, d)),
                  full((1, nd)), full((1, nd)), full((nd, 1)), full((nd, 1))],
        out_specs=[pl.BlockSpec((1, TM, nqkv), lambda b, i: (b, i, 0)),
                   pl.BlockSpec((1, TM, nout - nqkv), lambda b, i: (b, i, 0)),
                   pl.BlockSpec((1, TM, nd), lambda b, i: (b, i, 0)),
                   pl.BlockSpec((1, nd, TM), lambda b, i: (b, 0, i)),
                   pl.BlockSpec((1, TM, nd), lambda b, i: (b, i, 0))],
        out_shape=[jax.ShapeDtypeStruct((bn, t, nqkv), F32),
                   jax.ShapeDtypeStruct((bn, t, nout - nqkv), BF16),
                   jax.ShapeDtypeStruct((bn, t, nd), F32),
                   jax.ShapeDtypeStruct((bn, nd, t), F32),
                   jax.ShapeDtypeStruct((bn, t, nd), F32)],
        compiler_params=_cp(("arbitrary", "arbitrary"), VMEM_LIMIT),
        name="proj_delta",
    )(xs, mod, g, w_main, wab, wabt, row(a_log), row(dt_bias), col(a_log), col(dt_bias))


def _conv_kernel(p_ref, w_ref, o_ref, scr, *, l, t, ch, pad):
    c = pl.program_id(1)
    scr[0:pad, :] = jnp.zeros((pad, LANES), F32)
    scr[t + pad:t + 2 * pad, :] = jnp.zeros((pad, LANES), F32)
    scr[pad:t + pad, :] = p_ref[0]
    w = w_ref[...]
    half = w.shape[0] // 2
    for r0 in range(0, t, ch):
        near = (r0 <= l + half) and (r0 + ch >= l - half)
        tt = r0 + _iota((ch, 1), 0)
        acc = None
        for d in range(-half, half + 1):
            xd = scr[pad + r0 + d:pad + r0 + d + ch, :]
            if near and d != 0:
                xd = jnp.where(((tt + d) < l) == (tt < l), xd, 0.0)
            term = xd * w[d + half:d + half + 1]
            acc = term if acc is None else acc + term
        y = _silu(acc)
        nrm = y * lax.rsqrt(jnp.sum(y * y, axis=-1, keepdims=True) + EPS)
        out = jnp.where(c < A_HEADS, nrm * (A_DK ** -0.5), jnp.where(c < 2 * A_HEADS, nrm, y))
        o_ref[0, r0:r0 + ch, :] = out


def _delta_conv(p, conv_w, l):
    bn, t, nq = p.shape
    ch = 384 if t % 384 == 0 else TQ
    pad = 8
    kw = conv_w.shape[0]
    return pl.pallas_call(
        functools.partial(_conv_kernel, l=l, t=t, ch=ch, pad=pad),
        grid=(bn, nq // LANES),
        in_specs=[pl.BlockSpec((1, t, LANES), lambda b, c: (b, 0, c)),
                  pl.BlockSpec((kw, LANES), lambda b, c: (0, c))],
        out_specs=pl.BlockSpec((1, t, LANES), lambda b, c: (b, 0, c)),
        out_shape=jax.ShapeDtypeStruct((bn, t, nq), F32),
        scratch_shapes=[pltpu.VMEM((t + 2 * pad, LANES), F32)],
        compiler_params=_cp(("arbitrary", "arbitrary"), VMEM_LIMIT),
        name="delta_conv",
    )(p, conv_w.astype(F32))


def _merge_masks(ii, jj, lower):
    masks = []
    s = 1
    while s < A_CHUNK:
        grp = (ii // (2 * s)) == (jj // (2 * s))
        odd_i, odd_j = (ii // s) % 2 == 1, (jj // s) % 2 == 1
        masks.append(grp & odd_i & ~odd_j if lower else grp & ~odd_i & odd_j)
        s *= 2
    return masks


def _unit_triangular_inverse(a, eye, masks):
    t = eye
    for m in masks:
        tb = t.astype(BF16)
        t = t - _dot(_dot(tb, jnp.where(m, a, 0.0).astype(BF16)).astype(BF16), tb)
    return t


def _delta_prep_kernel(q_ref, k_ref, v_ref, gc_ref, gct_ref, beta_ref,
                       uf, ub, wf, wb, qdf, qdb, qkf, qkb, kdtf, kdtb, eg_ref, *, hb):
    hblk = pl.program_id(1)
    ii = _iota((TQ, TQ), 0)
    jj = _iota((TQ, TQ), 1)
    same = (ii // A_CHUNK) == (jj // A_CHUNK)
    eye = (ii == jj).astype(F32)
    merge = (_merge_masks(ii, jj, True), _merge_masks(ii, jj, False))
    lane_d =_iota((TQ, 2 * A_HEADS), 1)
    gc_all = gc_ref[0]
    beta_all = beta_ref[0]
    outs = ((uf, wf, qdf, qkf, kdtf), (ub, wb, qdb, qkb, kdtb))
    for hh in range(hb):
        head = hblk * hb + hh
        sl = slice(hh * LANES, (hh + 1) * LANES)
        q = q_ref[0, :, sl]
        k = k_ref[0, :, sl]
        v = v_ref[0, :, sl]
        kb = k.astype(BF16)
        kk = _dot_nt(kb, kb)
        qk = _dot_nt(q.astype(BF16), kb)
        for d in range(2):
            u_o, w_o, qd_o, qk_o, kdt_o = outs[d]
            idx = d * A_HEADS + head
            gcc = jnp.sum(jnp.where(lane_d == idx, gc_all, 0.0), axis=1, keepdims=True)
            bet = jnp.sum(jnp.where(lane_d == idx, beta_all, 0.0), axis=1, keepdims=True)
            gcr = gct_ref[0, pl.ds(idx, 1), :]
            if d == 0:
                incl, strict = same & (ii >= jj), same & (ii > jj)
                last = (ii // A_CHUNK) * A_CHUNK + (A_CHUNK - 1)
            else:
                incl, strict = same & (ii <= jj), same & (ii < jj)
                last = (ii // A_CHUNK) * A_CHUNK
            dm = jnp.exp(jnp.where(incl, gcc - gcr, NEG))
            glast = jnp.sum(jnp.where(jj == last, gcr, 0.0), axis=1, keepdims=True)
            a = jnp.where(strict, bet * kk * dm, 0.0)
            tinv = _unit_triangular_inverse(a, eye, merge[d])
            egc = jnp.exp(gcc)
            rhs = jnp.concatenate([v * bet, k * (bet * egc)], axis=1).astype(BF16)
            sol = _dot(tinv.astype(BF16), rhs)
            u_o[0, :, sl] = sol[:, :LANES]
            w_o[0, :, sl] = sol[:, LANES:].astype(BF16)
            qd_o[0, :, sl] = (q * egc).astype(BF16)
            qk_o[0, :, sl] = (qk * dm).astype(BF16)
            kd = k * jnp.exp(glast - gcc)
            kdt_o[0, sl, :] = kd.T.astype(BF16)
            eglast = jnp.exp(glast)
            for cch in range(TQ // A_CHUNK):
                row = (d * hb + hh) * (TQ // A_CHUNK) + cch
                eg_ref[0, 0, 0, row:row + 1, :] = jnp.broadcast_to(eglast[cch * A_CHUNK:cch * A_CHUNK + 1], (1, LANES))


def _delta_prep(qkv, gc, gct, beta, hb):
    bn, t, _ = qkv.shape
    nt = t // TQ
    nhb = A_HEADS // hb
    wdt = A_HEADS * A_DV
    tok = lambda off: pl.BlockSpec((1, TQ, hb * LANES), lambda b, h, i: (b, i, off + h))
    nd = 2 * A_HEADS
    tok_shape = lambda dt: jax.ShapeDtypeStruct((bn, t, wdt), dt)
    return pl.pallas_call(
        functools.partial(_delta_prep_kernel, hb=hb),
        grid=(bn, nhb, nt),
        in_specs=[tok(0), tok(nhb), tok(2 * nhb),
                  pl.BlockSpec((1, TQ, nd), lambda b, h, i: (b, i, 0)),
                  pl.BlockSpec((1, nd, TQ), lambda b, h, i: (b, 0, i)),
                  pl.BlockSpec((1, TQ, nd), lambda b, h, i: (b, i, 0))],
        out_specs=[tok(0)] * 8 + [pl.BlockSpec((1, hb * LANES, TQ), lambda b, h, i: (b, h, i))] * 2
                  + [pl.BlockSpec((1, 1, 1, 4 * hb, LANES), lambda b, h, i: (b, h, i, 0, 0))],
        out_shape=[tok_shape(F32), tok_shape(F32)] + [tok_shape(BF16)] * 6
                  + [jax.ShapeDtypeStruct((bn, wdt, t), BF16)] * 2
                  + [jax.ShapeDtypeStruct((bn, nhb, nt, 4 * hb, LANES), F32)],
        compiler_params=_cp(("arbitrary", "arbitrary", "arbitrary"), VMEM_LIMIT),
        name="delta_prep",
    )(qkv, qkv, qkv, gc, gct, beta)


def _delta_scan_kernel(uf, wf, qdf, qkf, kdtf, egf, ub, wb, qdb, qkb, kdtb, egb, of_ref, ob_ref, s_ref, *, hb):
    @pl.when(pl.program_id(1) == 0)
    def _():
        s_ref[...] = jnp.zeros(s_ref.shape, F32)

    zeros = jnp.zeros((A_CHUNK, LANES), BF16)
    dirs = ((uf, wf, qdf, qkf, kdtf, egf, of_ref, (0, 1)), (ub, wb, qdb, qkb, kdtb, egb, ob_ref, (1, 0)))
    for d, (u, w, qd, qk, kdt, eg, o_ref, order) in enumerate(dirs):
        for head in range(A_HEADS):
            sl = slice(head * LANES, (head + 1) * LANES)
            hblk, hh = divmod(head, hb)
            s = s_ref[d, head]
            for cch in order:
                rs = slice(cch * A_CHUNK, (cch + 1) * A_CHUNK)
                wq = jnp.concatenate([w[0, rs, sl], qd[0, rs, sl]], axis=0)
                ws = _dot(wq, s.astype(BF16))
                vn = (u[0, rs, sl] - ws[:A_CHUNK]).astype(BF16)
                vfull = jnp.concatenate([vn, zeros] if cch == 0 else [zeros, vn], axis=0)
                o_ref[0, rs, sl] = ws[A_CHUNK:] + _dot(qk[0, rs, sl], vfull)
                row = (d * hb + hh) * 2 + cch
                s = s * eg[0, hblk, 0, row:row + 1, :] + _dot(kdt[0, sl, :], vfull)
            s_ref[d, head] = s


def _delta_scan(prep, l, hb):
    uf, ub, wf, wb, qdf, qdb, qkf, qkb, kdtf, kdtb, eg = prep
    bn, t, wdt = uf.shape
    nt = t // TQ
    nct = l // TQ
    nhb = A_HEADS // hb
    fwd = lambda s: s
    bwd = lambda s: jnp.where(s < nct, nct - 1 - s, nt - 1 - (s - nct))
    tok = lambda f: pl.BlockSpec((1, TQ, wdt), lambda b, s: (b, f(s), 0))
    tr = lambda f: pl.BlockSpec((1, wdt, TQ), lambda b, s: (b, 0, f(s)))
    egs = lambda f: pl.BlockSpec((1, nhb, 1, 4 * hb, LANES), lambda b, s: (b, 0, f(s), 0, 0))
    return pl.pallas_call(
        functools.partial(_delta_scan_kernel, hb=hb),
        grid=(bn, nt),
        in_specs=[tok(fwd)] * 4 + [tr(fwd), egs(fwd)] + [tok(bwd)] * 4 + [tr(bwd), egs(bwd)],
        out_specs=[tok(fwd), tok(bwd)],
        out_shape=[jax.ShapeDtypeStruct((bn, t, wdt), F32)] * 2,
        scratch_shapes=[pltpu.VMEM((2, A_HEADS, A_DK, A_DV), F32)],
        compiler_params=_cp(("arbitrary", "arbitrary"), VMEM_LIMIT),
        name="delta_scan",
    )(uf, wf, qdf, qkf, kdtf, eg, ub, wb, qdb, qkb, kdtb, eg)


def _residual_router(y, w_ref, x_ref, mod_ref, gffn_ref, wr_ref, xo_ref, h_ref, aff_ref):
    mod = mod_ref[0, 0]
    xn = x_ref[0] + mod[2:3] * _dot(y, w_ref[...])
    xo_ref[0] = xn
    h = _norm_mod(xn, gffn_ref[...], mod, 3, 4)
    h_hi, h_lo = _split2(h)
    h_ref[0] = h_hi
    lg = _dot_nt(wr_ref[0], h_hi) + _dot_nt(wr_ref[1], h_hi) + _dot_nt(wr_ref[0], h_lo)
    e = jnp.exp(lg - jnp.max(lg, axis=0, keepdims=True))
    aff_ref[0] = e / jnp.sum(e, axis=0, keepdims=True)


def _out_kernel(y_ref, *rest):
    _residual_router(y_ref[0], *rest)


def _out_delta_kernel(of_ref, ob_ref, z_ref, gout_ref, *rest):
    o = of_ref[0] + ob_ref[0]
    parts = []
    for hd in range(A_HEADS):
        sl = slice(hd * A_DV, (hd + 1) * A_DV)
        oh = o[:, sl]
        ms = jnp.mean(oh * oh, axis=-1, keepdims=True)
        parts.append((oh * lax.rsqrt(ms + EPS) * gout_ref[...] * _silu(z_ref[0, :, sl].astype(F32))).astype(BF16))
    _residual_router(jnp.concatenate(parts, axis=1), *rest)


def _out_proj(pre, w_out, xs, mod, g_ffn, wr, nct_m, delta):
    bn, t, d = xs.shape
    k = w_out.shape[0]
    ne = wr.shape[1]
    tok = lambda width: pl.BlockSpec((1, TM, width), lambda b, i: (b, i, 0))
    full = lambda shape: pl.BlockSpec(shape, lambda b, i: (0,) * len(shape))
    if delta:
        of, ob, z, gout = pre
        head_specs = [tok(k), tok(k), tok(k), full((1, A_DV))]
        head_args = (of, ob, z, gout.reshape(1, A_DV).astype(F32))
        body = _out_delta_kernel
    else:
        head_specs = [tok(k)]
        head_args = (pre,)
        body = _out_kernel
    return pl.pallas_call(
        body,
        grid=(bn, t // TM),
        in_specs=head_specs + [full((k, d)), tok(d),
                               pl.BlockSpec((1, 1, 6, d), lambda b, i: (b, _seg(i, nct_m), 0, 0)),
                               full((1, d)), full((2, ne, d))],
        out_specs=[tok(d), tok(d), pl.BlockSpec((1, ne, TM), lambda b, i: (b, 0, i))],
        out_shape=[jax.ShapeDtypeStruct((bn, t, d), F32), jax.ShapeDtypeStruct((bn, t, d), BF16),
                   jax.ShapeDtypeStruct((bn, ne, t), F32)],
        compiler_params=_cp(("arbitrary", "arbitrary"), VMEM_LIMIT),
        name="out_delta" if delta else "out_proj",
    )(*head_args, w_out, xs, mod, g_ffn, wr)


def _kth_largest_bits(bits, k):
    def body(it, thr):
        cand = thr | jnp.left_shift(jnp.int32(1), 30 - it)
        cnt = jnp.sum((bits >= cand).astype(I32), axis=1, keepdims=True)
        return jnp.where(cnt >= k, cand, thr)
    return lax.fori_loop(0, 31, body, jnp.zeros((bits.shape[0], 1), I32))


def _select_kernel(aff_ref, posd_ref, offs_ref, posc_ref, gatec_ref, *, l, t, cap_c, cap_x):
    a = aff_ref[0]
    ne = a.shape[0]
    bits = pltpu.bitcast(a, I32)
    upper = (_iota((LANES, LANES), 0) <= _iota((LANES, LANES), 1)).astype(BF16)
    ident = (_iota((LANES, LANES), 0) == _iota((LANES, LANES), 1)).astype(BF16)
    nt = t // LANES
    sel = [None] * nt
    for s0, s1, cap in ((0, l, cap_c), (l, t, cap_x)):
        bseg = bits[:, s0:s1]
        thr = _kth_largest_bits(bseg, cap)
        gtf = jnp.where(bseg > thr, 1.0, 0.0)
        eqf = jnp.where(bseg == thr, 1.0, 0.0)
        need = cap - jnp.sum(gtf, axis=1, keepdims=True)
        run = jnp.zeros((ne, 1), F32)
        for j in range((s1 - s0) // LANES):
            ej = eqf[:, j * LANES:(j + 1) * LANES]
            inc = _dot(ej.astype(BF16), upper)
            keep = jnp.where(inc - ej + run < need, ej, 0.0)
            sel[s0 // LANES + j] = jnp.maximum(gtf[:, j * LANES:(j + 1) * LANES], keep)
            run = run + inc[:, LANES - 1:LANES]
    run = jnp.zeros((ne, 1), F32)
    offs = jnp.zeros((ne, LANES), I32)
    lane = _iota((ne, LANES), 1)
    for j in range(nt):
        sj = sel[j]
        inc = _dot(sj.astype(BF16), upper)
        pos = jnp.where(sj > 0.0, inc - sj + run, -1.0)
        posd_ref[0, j] = pos.astype(I32)
        offs = jnp.where(lane == j, run.astype(I32), offs)
        p_hi, p_lo = _split2(pos)
        posc_ref[0, j * LANES:(j + 1) * LANES, :] = (_dot_nt(ident, p_hi) + _dot_nt(ident, p_lo)).astype(I32)
        gj = jnp.where(sj > 0.0, a[:, j * LANES:(j + 1) * LANES], 0.0)
        g1 = gj.astype(BF16)
        r1 = gj - g1.astype(F32)
        g2 = r1.astype(BF16)
        g3 = (r1 - g2.astype(F32)).astype(BF16)
        gatec_ref[0, j * LANES:(j + 1) * LANES, :] = _dot_nt(ident, g1) + _dot_nt(ident, g2) + _dot_nt(ident, g3)
        run = run + inc[:, LANES - 1:LANES]
    offs_ref[0] = offs


def _select(aff, l, cap_c, cap_x):
    bn, ne, t = aff.shape
    nt = t // LANES
    return pl.pallas_call(
        functools.partial(_select_kernel, l=l, t=t, cap_c=cap_c, cap_x=cap_x),
        grid=(bn,),
        in_specs=[pl.BlockSpec((1, ne, t), lambda b: (b, 0, 0))],
        out_specs=[pl.BlockSpec((1, nt, ne, LANES), lambda b: (b, 0, 0, 0)),
                   pl.BlockSpec((1, ne, LANES), lambda b: (b, 0, 0)),
                   pl.BlockSpec((1, t, ne), lambda b: (b, 0, 0)),
                   pl.BlockSpec((1, t, ne), lambda b: (b, 0, 0))],
        out_shape=[jax.ShapeDtypeStruct((bn, nt, ne, LANES), I32), jax.ShapeDtypeStruct((bn, ne, LANES), I32),
                   jax.ShapeDtypeStruct((bn, t, ne), I32), jax.ShapeDtypeStruct((bn, t, ne), F32)],
        compiler_params=_cp(("arbitrary",), VMEM_LIMIT),
        name="route_select",
    )(aff)


GATHER_WIN = TQ + 8
COMBINE_WIN = TQ + 16


def _expert_kernel(offs_ref, h_ref, pos_ref, wgu_ref, wd_ref, y_ref, xg_ref, acc_ref, *, nt, r, fc):
    b = pl.program_id(0)
    e = pl.program_id(1)
    ne = pl.num_programs(1)
    xg_ref[...] = jnp.zeros(xg_ref.shape, F32)
    riota = _iota((GATHER_WIN, LANES), 0)

    def tile(j, carry):
        off = offs_ref[(b * ne + e) * LANES + j]
        aoff = pl.multiple_of((off // 8) * 8, 8)
        lr = pos_ref[0, j, pl.ds(e, 1), :] - aoff
        onehot = jnp.where(riota == lr, 1.0, 0.0).astype(BF16)
        hj = h_ref[0, pl.ds(pl.multiple_of(j * TQ, TQ), TQ), :]
        xg_ref[pl.ds(aoff, GATHER_WIN), :] += _dot(onehot, hj)
        return carry

    lax.fori_loop(0, nt, tile, 0)
    xg = xg_ref[0:r, :].astype(BF16)
    f = wd_ref.shape[1]
    for c in range(f // fc):
        g = _dot(xg, wgu_ref[0, :, c * fc:(c + 1) * fc])
        u = _dot(xg, wgu_ref[0, :, f + c * fc:f + (c + 1) * fc])
        part = _dot((_silu(g) * u).astype(BF16), wd_ref[0, c * fc:(c + 1) * fc, :])
        if c == 0:
            acc_ref[...] = part
        else:
            acc_ref[...] += part
    y_ref[0, 0] = acc_ref[...].astype(BF16)


def _experts(offs, hf, posd, wgu, wd, r):
    bn, t, d = hf.shape
    ne, _, f2 = wgu.shape
    nt = t // TQ
    xg_rows = -(-(r + GATHER_WIN) // 8) * 8
    return pl.pallas_call(
        functools.partial(_expert_kernel, nt=nt, r=r, fc=min(512, f2 // 2)),
        grid_spec=pltpu.PrefetchScalarGridSpec(
            num_scalar_prefetch=1,
            grid=(bn, ne),
            in_specs=[pl.BlockSpec((1, t, d), lambda b, e, o: (b, 0, 0), pipeline_mode=pl.Buffered(1)),
                      pl.BlockSpec((1, nt, ne, LANES), lambda b, e, o: (b, 0, 0, 0)),
                      pl.BlockSpec((1, d, f2), lambda b, e, o: (e, 0, 0)),
                      pl.BlockSpec((1, f2 // 2, d), lambda b, e, o: (e, 0, 0))],
            out_specs=pl.BlockSpec((1, 1, r, d), lambda b, e, o: (b, e, 0, 0)),
            scratch_shapes=[pltpu.VMEM((xg_rows, d), F32), pltpu.VMEM((r, d), F32)]),
        out_shape=jax.ShapeDtypeStruct((bn, ne, r, d), BF16),
        compiler_params=_cp(("arbitrary", "arbitrary"), VMEM_LIMIT),
        name="experts",
    )(offs, hf, posd, wgu, wd)


def _combine_kernel(offs_ref, y_ref, posc_ref, gate_ref, x_ref, mod_ref, o_ref, *, r, win):
    b = pl.program_id(0)
    j = pl.program_id(1)
    ne = y_ref.shape[1]
    liota = _iota((TQ, win), 1)
    pc = posc_ref[0]
    gt = gate_ref[0]
    acc = jnp.zeros(x_ref.shape[1:], F32)
    for e in range(ne):
        off = offs_ref[(b * ne + e) * LANES + j]
        aoff = pl.multiple_of(jnp.minimum((off // 16) * 16, r - win), 16)
        onehot = jnp.where(liota == pc[:, e:e + 1] - aoff, 1.0, 0.0).astype(BF16)
        acc = acc + gt[:, e:e + 1] * _dot(onehot, y_ref[0, e, pl.ds(aoff, win), :])
    o_ref[0] = x_ref[0] + mod_ref[0, 0][5:6] * acc


def _combine(offs, y, posc, gatec, xs, mod, nct):
    bn, t, d = xs.shape
    ne, r = y.shape[1], y.shape[2]
    win = min(COMBINE_WIN, r)
    tok = lambda width: pl.BlockSpec((1, TQ, width), lambda b, j, o: (b, j, 0))
    return pl.pallas_call(
        functools.partial(_combine_kernel, r=r, win=win),
        grid_spec=pltpu.PrefetchScalarGridSpec(
            num_scalar_prefetch=1,
            grid=(bn, t // TQ),
            in_specs=[pl.BlockSpec((1, ne, r, d), lambda b, j, o: (b, 0, 0, 0), pipeline_mode=pl.Buffered(1)),
                      tok(ne), tok(ne), tok(d),
                      pl.BlockSpec((1, 1, 6, d), lambda b, j, o: (b, _seg(j, nct), 0, 0))],
            out_specs=tok(d)),
        out_shape=jax.ShapeDtypeStruct((bn, t, d), F32),
        compiler_params=_cp(("arbitrary", "arbitrary"), VMEM_LIMIT),
        name="moe_combine",
    )(offs, y, posc, gatec, xs, mod)


def _moe(xs, hf, aff, mod, wgu, wd, l):
    bn, t, _ = xs.shape
    cap_c = EC_CAPACITY * l // N_EXPERTS
    cap_x = EC_CAPACITY * (t - l) // N_EXPERTS
    posd, offs, posc, gatec = _select(aff, l, cap_c, cap_x)
    offs = offs.reshape(-1)
    y = _experts(offs, hf, posd, wgu, wd, cap_c + cap_x)
    return _combine(offs, y, posc, gatec, xs, mod, l // TQ)


def _hi_lo(w):
    hi = w.astype(BF16)
    return jnp.stack([hi, (w - hi.astype(F32)).astype(BF16)])


def _mixer_delta(xs, mod, g_mix, w_in, conv_w, a_log, dt_bias, l, hb=4):
    nqkvz = 2 * A_HEADS * A_DK + 2 * A_HEADS * A_DV
    wab = w_in[:, nqkvz:]
    p, z, gc, gct, beta = _proj_delta(xs, mod, g_mix, w_in[:, :nqkvz].astype(BF16), _hi_lo(wab), _hi_lo(wab.T),
                                      a_log, dt_bias, l // TM)
    qkv = _delta_conv(p, conv_w, l)
    of, ob = _delta_scan(_delta_prep(qkv, gc, gct, beta, hb), l, hb)
    return of, ob, z


def _mixer_swa(xs, mod, g_mix, w_in, qn, kn, sink, cosf, sinf, l):
    nq = B_QHEADS * B_HD
    nk = B_KVHEADS * B_HD
    dup = lambda w: jnp.concatenate([w.reshape(-1, B_KVHEADS, 1, B_HD)] * 2, axis=2).reshape(-1, 2 * nk)
    w = jnp.concatenate([w_in[:, :nq], dup(w_in[:, nq:nq + nk]), dup(w_in[:, nq + nk:])], axis=1).astype(BF16)
    gain = jnp.concatenate([jnp.tile(qn, B_QHEADS) * (B_HD ** -0.5), jnp.tile(kn, 2 * B_KVHEADS)])[None].astype(F32)
    qk, v2 = _proj_qk(xs, mod, g_mix, w, gain, cosf, sinf, nq + 2 * nk, l // TM)
    return _swa_attention(qk, v2, sink, l)


def _mixer_diff(xs, mod, g_mix, w_in, qn, kn, lam_vecs, g_sub, cosf, sinf, l, lam_init):
    nqk = C_HEADS * 2 * C_HD
    gain = jnp.concatenate([jnp.tile(qn, 2 * C_HEADS) * (C_HD ** -0.5), jnp.tile(kn, 2 * C_HEADS)])[None].astype(F32)
    qk, v = _proj_qk(xs, mod, g_mix, w_in.astype(BF16), gain, cosf, sinf, 2 * nqk, l // TM)
    return _diff_attention(qk, v, lam_vecs, g_sub, l, lam_init)


def kernel(x, c, ctx, c_ctx, w_ada, b_ada, g_mix, g_ffn, a_w_in, a_conv, a_log, a_dt_bias, a_g_out, a_w_out,
           b_w_in, b_q_norm, b_k_norm, b_sink, b_w_out, c_w_in, c_q_norm, c_k_norm, c_lambda, c_g_sub, c_w_out,
           w_router, w_gate_up, w_down):
    depth = w_ada.shape[0]
    n = x.shape[1]
    l = ctx.shape[1]
    assert l % TM == 0 and n % TM == 0
    xs = jnp.concatenate([ctx, x], axis=1)
    mods = _adaln(c, c_ctx, w_ada, b_ada)
    cos_b, sin_b = _rope_tables(n, l, B_HD)
    cos_c, sin_c = _rope_tables(n, l, C_HD)
    nct_m = l // TM
    for layer in range(depth):
        kind, j = layer % N_MIXERS, layer // N_MIXERS
        mod = mods[layer]
        gm = g_mix[layer][None].astype(F32)
        gf = g_ffn[layer][None].astype(F32)
        wr = _hi_lo(w_router[layer].T)
        if kind == 0:
            pre = _mixer_delta(xs, mod, gm, a_w_in[j], a_conv[j], a_log[j], a_dt_bias[j], l)
            xs, hf, aff = _out_proj(pre + (a_g_out[j],), a_w_out[j].astype(BF16), xs, mod, gf, wr, nct_m, True)
        elif kind == 1:
            pre = _mixer_swa(xs, mod, gm, b_w_in[j], b_q_norm[j], b_k_norm[j], b_sink[j], cos_b, sin_b, l)
            xs, hf, aff = _out_proj(pre, b_w_out[j].astype(BF16), xs, mod, gf, wr, nct_m, False)
        else:
            lam_init = 0.8 - 0.6 * math.exp(-0.3 * layer)
            pre = _mixer_diff(xs, mod, gm, c_w_in[j], c_q_norm[j], c_k_norm[j], c_lambda[j], c_g_sub[j],
                              cos_c, sin_c, l, lam_init)
            xs, hf, aff = _out_proj(pre, c_w_out[j].astype(BF16), xs, mod, gf, wr, nct_m, False)
        xs = _moe(xs, hf, aff, mod, w_gate_up[layer].astype(BF16), w_down[layer].astype(BF16), l)
    return xs[:, l:]
```

```python
import functools
import math

import jax
import jax.numpy as jnp
from jax import lax
from jax.experimental import pallas as pl
from jax.experimental.pallas import tpu as pltpu

F32 = jnp.float32
BF16 = jnp.bfloat16
I32 = jnp.int32

EPS = 1e-6
ROPE_BASE = 10000.0
GRID_W = 64
N_MIXERS = 3

A_HEADS, A_DK, A_DV, A_CHUNK = 8, 128, 128, 64
B_QHEADS, B_KVHEADS, B_HD = 16, 4, 64
B_GROUP = B_QHEADS // B_KVHEADS
C_HEADS, C_HD = 8, 64
N_EXPERTS, EC_CAPACITY = 16, 2

LANES = 128
TM = 256
TQ = 128
NEG = -1e30
VMEM_LIMIT = 56 * 1024 * 1024


def _cp(sem, vmem=None):
    return pltpu.CompilerParams(dimension_semantics=sem, vmem_limit_bytes=vmem)


def _dot(a, b):
    return jnp.dot(a, b, preferred_element_type=F32)


def _dot_nt(a, b):
    return lax.dot_general(a, b, (((1,), (1,)), ((), ())), preferred_element_type=F32)


def _split2(a):
    hi = a.astype(BF16)
    return hi, (a - hi.astype(F32)).astype(BF16)


def _sigmoid(x):
    return 1.0 / (1.0 + jnp.exp(-x))


def _silu(x):
    return x * _sigmoid(x)


def _softplus(x):
    return jnp.maximum(x, 0.0) + jnp.log(1.0 + jnp.exp(-jnp.abs(x)))


def _norm_mod(x, g, mod, i_shift, i_scale):
    ms = jnp.mean(x * x, axis=-1, keepdims=True)
    y = x * lax.rsqrt(ms + EPS) * g
    return y * (1.0 + mod[i_scale:i_scale + 1]) + mod[i_shift:i_shift + 1]


def _iota(shape, dim):
    return lax.broadcasted_iota(I32, shape, dim)


def _ada_kernel(s_ref, w_ref, b_ref, o_ref):
    s = _silu(s_ref[...])
    s_hi, s_lo = _split2(s)
    w_hi, w_lo = _split2(w_ref[0])
    o_ref[0] = _dot(s_hi, w_hi) + _dot(s_hi, w_lo) + _dot(s_lo, w_hi) + b_ref[0]


def _adaln(c, c_ctx, w_ada, b_ada):
    depth, d, d6 = w_ada.shape
    bn = c.shape[0]
    rows_n = -(-(bn + 1) // 8) * 8
    rows = jnp.zeros((rows_n, d), F32).at[:bn].set(c).at[bn].set(c_ctx)
    nb = d6 // 4
    out = pl.pallas_call(
        _ada_kernel,
        grid=(depth, d6 // nb),
        in_specs=[pl.BlockSpec((rows_n, d), lambda l, j: (0, 0)),
                  pl.BlockSpec((1, d, nb), lambda l, j: (l, 0, j)),
                  pl.BlockSpec((1, 1, nb), lambda l, j: (l, 0, j))],
        out_specs=pl.BlockSpec((1, rows_n, nb), lambda l, j: (l, 0, j)),
        out_shape=jax.ShapeDtypeStruct((depth, rows_n, d6), F32),
        compiler_params=_cp(("arbitrary", "arbitrary"), VMEM_LIMIT),
        name="adaln",
    )(rows, w_ada, b_ada.reshape(depth, 1, d6))
    mx = out[:, :bn].reshape(depth, bn, 6, d)
    mc = jnp.broadcast_to(out[:, bn].reshape(depth, 1, 6, d), (depth, bn, 6, d))
    return jnp.stack([mc, mx], axis=2)


def _seg(i, nct):
    return jnp.where(i < nct, 0, 1)


def _proj_qk_kernel(x_ref, mod_ref, g_ref, w_ref, gain_ref, cos_ref, sin_ref, qk_ref, v_ref, *, nqk):
    h = _norm_mod(x_ref[0], g_ref[...], mod_ref[0, 0], 0, 1)
    p = _dot(h.astype(BF16), w_ref[...])
    grp = (_iota((LANES, LANES), 0) // 64 == _iota((LANES, LANES), 1) // 64).astype(BF16)
    first = (_iota((1, LANES), 1) % 64) < 32
    cs = cos_ref[...]
    sn = sin_ref[...]
    for t in range(nqk // LANES):
        sl = slice(t * LANES, (t + 1) * LANES)
        xt = p[:, sl]
        sq_hi, sq_lo = _split2(xt * xt)
        ms = (_dot(sq_hi, grp) + _dot(sq_lo, grp)) * (1.0 / 64)
        y = xt * lax.rsqrt(ms + EPS) * gain_ref[:, sl]
        rot = jnp.where(first, pltpu.roll(y, 96, 1), pltpu.roll(y, 32, 1))
        qk_ref[0, :, sl] = (y * cs + rot * sn).astype(BF16)
    v_ref[0] = p[:, nqk:].astype(BF16)


def _proj_qk(xs, mod, g, w, gain, cosf, sinf, nqk, nct_m):
    bn, t, d = xs.shape
    nout = w.shape[1]
    return pl.pallas_call(
        functools.partial(_proj_qk_kernel, nqk=nqk),
        grid=(bn, t // TM),
        in_specs=[pl.BlockSpec((1, TM, d), lambda b, i: (b, i, 0)),
                  pl.BlockSpec((1, 1, 6, d), lambda b, i: (b, _seg(i, nct_m), 0, 0)),
                  pl.BlockSpec((1, d), lambda b, i: (0, 0)),
                  pl.BlockSpec((d, nout), lambda b, i: (0, 0)),
                  pl.BlockSpec((1, nqk), lambda b, i: (0, 0)),
                  pl.BlockSpec((TM, LANES), lambda b, i: (i, 0)),
                  pl.BlockSpec((TM, LANES), lambda b, i: (i, 0))],
        out_specs=[pl.BlockSpec((1, TM, nqk), lambda b, i: (b, i, 0)),
                   pl.BlockSpec((1, TM, nout - nqk), lambda b, i: (b, i, 0))],
        out_shape=[jax.ShapeDtypeStruct((bn, t, nqk), BF16),
                   jax.ShapeDtypeStruct((bn, t, nout - nqk), BF16)],
        compiler_params=_cp(("arbitrary", "arbitrary"), VMEM_LIMIT),
        name="proj_qk",
    )(xs, mod, g, w, gain, cosf, sinf)


def _rope_tables(n, l, head_dim):
    t = jnp.arange(n)
    n_freq = head_dim // 4
    inv = ROPE_BASE ** (-jnp.arange(n_freq, dtype=F32) / n_freq)
    ang = jnp.concatenate([(t // GRID_W).astype(F32)[:, None] * inv, (t % GRID_W).astype(F32)[:, None] * inv], -1)
    cs, sn = jnp.cos(ang), jnp.sin(ang)
    reps = LANES // head_dim
    cosf = jnp.tile(jnp.concatenate([cs, cs], -1), (1, reps))
    sinf = jnp.tile(jnp.concatenate([-sn, sn], -1), (1, reps))
    cosf = jnp.concatenate([jnp.ones((l, LANES), F32), cosf], 0)
    sinf = jnp.concatenate([jnp.zeros((l, LANES), F32), sinf], 0)
    return cosf, sinf


def _swa_kernel(q_ref, kp_ref, kc_ref, kn_ref, kx_ref, vp_ref, vc_ref, vn_ref, vx_ref, sink_ref, o_ref, *, nct, ntx, l):
    xi = pl.program_id(2) - nct
    q = q_ref[0]
    lo = _iota((TQ, LANES), 1) < 64
    zero = jnp.zeros((TQ, LANES), BF16)
    qa, qb = q[:, :LANES], q[:, LANES:]
    q4 = jnp.concatenate([jnp.where(lo, qa, zero), jnp.where(lo, zero, qa),
                          jnp.where(lo, qb, zero), jnp.where(lo, zero, qb)], axis=0)
    kcat = jnp.concatenate([kp_ref[0], kc_ref[0], kn_ref[0], kx_ref[0]], axis=0)
    s = _dot_nt(q4, kcat)
    nk = 3 * TQ + l
    r = _iota((TQ, nk), 0)
    c = _iota((TQ, nk), 1)
    is_x = xi >= 0
    m_prev = (c < TQ) & (c >= r) & (xi >= 1)
    m_cur = (c >= TQ) & (c < 2 * TQ) & is_x
    m_next = (c >= 2 * TQ) & (c < 3 * TQ) & (c - 2 * TQ <= r) & (xi + 1 < ntx) & is_x
    mask = m_prev | m_cur | m_next | (c >= 3 * TQ)
    s = jnp.where(jnp.concatenate([mask] * B_GROUP, axis=0), s, NEG)
    sk = sink_ref[0]
    m = jnp.maximum(jnp.max(s, axis=-1, keepdims=True), sk)
    p = jnp.exp(s - m)
    den = jnp.sum(p, axis=-1, keepdims=True) + jnp.exp(sk - m)
    vcat = jnp.concatenate([vp_ref[0], vc_ref[0], vn_ref[0], vx_ref[0]], axis=0)
    o4 = _dot(p.astype(BF16), vcat) / den
    oa = jnp.where(lo, o4[0:TQ], o4[TQ:2 * TQ])
    ob = jnp.where(lo, o4[2 * TQ:3 * TQ], o4[3 * TQ:4 * TQ])
    o_ref[0] = jnp.concatenate([oa, ob], axis=1).astype(BF16)


def _swa_attention(qk, v2, sink, l):
    bn, t, _ = qk.shape
    nt = t // TQ
    nct = l // TQ
    gw = B_GROUP * B_HD
    kcol = B_QHEADS * B_HD // LANES
    prev = lambda b, kv, i: jnp.maximum(i - 1, 0)
    nxt = lambda b, kv, i: jnp.minimum(i + 1, nt - 1)
    kspec = lambda f: pl.BlockSpec((1, TQ, LANES), lambda b, kv, i: (b, f(b, kv, i), kcol + kv))
    vspec = lambda f: pl.BlockSpec((1, TQ, LANES), lambda b, kv, i: (b, f(b, kv, i), kv))
    cur = lambda b, kv, i: i
    sinkcol = jnp.repeat(sink.reshape(B_KVHEADS, B_GROUP), TQ, axis=1).reshape(B_KVHEADS, B_GROUP * TQ, 1).astype(F32)
    return pl.pallas_call(
        functools.partial(_swa_kernel, nct=nct, ntx=nt - nct, l=l),
        grid=(bn, B_KVHEADS, nt),
        in_specs=[pl.BlockSpec((1, TQ, gw), lambda b, kv, i: (b, i, kv)),
                  kspec(prev), kspec(cur), kspec(nxt),
                  pl.BlockSpec((1, l, LANES), lambda b, kv, i: (b, 0, kcol + kv)),
                  vspec(prev), vspec(cur), vspec(nxt),
                  pl.BlockSpec((1, l, LANES), lambda b, kv, i: (b, 0, kv)),
                  pl.BlockSpec((1, B_GROUP * TQ, 1), lambda b, kv, i: (kv, 0, 0))],
        out_specs=pl.BlockSpec((1, TQ, gw), lambda b, kv, i: (b, i, kv)),
        out_shape=jax.ShapeDtypeStruct((bn, t, B_QHEADS * B_HD), BF16),
        compiler_params=_cp(("arbitrary", "arbitrary", "arbitrary"), VMEM_LIMIT),
        name="swa_attention",
    )(qk, qk, qk, qk, qk, v2, v2, v2, v2, sinkcol)


LOG2E = 1.4426950408889634
SAFE_BOUND = 60.0


def _diff_kernel(lam_ref, gsub_ref, q_ref, k_ref, v_ref, o_ref, kn_ref, *, nct, l, t, ck, lam_init):
    i = pl.program_id(2)
    q = q_ref[0]
    lo = _iota((TQ, LANES), 1) < 64
    zero = jnp.zeros((TQ, LANES), BF16)
    q2 = jnp.concatenate([jnp.where(lo, q, zero), jnp.where(lo, zero, q)], axis=0)

    @pl.when(i == 0)
    def _():
        grp = (_iota((LANES, LANES), 0) // 64 == _iota((LANES, LANES), 1) // 64).astype(BF16)
        mx = jnp.zeros((1, LANES), F32)
        for c0 in range(0, t, ck):
            kf = k_ref[0, c0:c0 + ck, :].astype(F32)
            sq_hi, sq_lo = _split2(kf * kf)
            mx = jnp.maximum(mx, jnp.max(_dot(sq_hi, grp) + _dot(sq_lo, grp), axis=0, keepdims=True))
        kn_ref[...] = mx

    q2f = q2.astype(F32)
    qn = jnp.sqrt(jnp.sum(q2f * q2f, axis=1, keepdims=True))
    kn2 = kn_ref[...]
    kn = jnp.sqrt(jnp.where(_iota((2 * TQ, 1), 0) < TQ, kn2[:, 0:1], kn2[:, 64:65]))
    bound = qn * kn * 1.01 + 1e-6
    safe = jnp.max(bound) <= SAFE_BOUND

    def finish(den, acc):
        lv = lam_ref[...]
        lam = (jnp.exp(jnp.sum(lv[0:1] * lv[1:2], axis=-1, keepdims=True))
               - jnp.exp(jnp.sum(lv[2:3] * lv[3:4], axis=-1, keepdims=True)) + lam_init)
        o = acc[0:TQ] / den[0:TQ] - lam * (acc[TQ:] / den[TQ:])
        ms = jnp.mean(o * o, axis=-1, keepdims=True)
        o_ref[0] = (o * lax.rsqrt(ms + EPS) * gsub_ref[...] * (1.0 - lam_init)).astype(BF16)

    def bounded(nkeys):
        step = min(ck, nkeys)
        psum = jnp.zeros((2 * TQ, LANES), F32)
        acc = jnp.zeros((2 * TQ, LANES), F32)
        for c0 in range(0, nkeys, step):
            p = jnp.exp2(_dot_nt(q2, k_ref[0, c0:c0 + step, :]) - bound)
            for j in range(step // LANES):
                psum = psum + p[:, j * LANES:(j + 1) * LANES]
            acc = acc + _dot(p.astype(BF16), v_ref[0, c0:c0 + step, :])
        finish(jnp.sum(psum, axis=1, keepdims=True), acc)

    def online(nkeys):
        step = min(ck, nkeys)

        def body(j, carry):
            m, den, acc = carry
            st = pl.multiple_of(j * step, step)
            s = _dot_nt(q2, k_ref[0, pl.ds(st, step), :])
            m2 = jnp.maximum(m, jnp.max(s, axis=-1, keepdims=True))
            a = jnp.exp2(m - m2)
            p = jnp.exp2(s - m2)
            den = a * den + jnp.sum(p, axis=-1, keepdims=True)
            acc = a * acc + _dot(p.astype(BF16), v_ref[0, pl.ds(st, step), :])
            return m2, den, acc

        init = (jnp.full((2 * TQ, 1), NEG, F32), jnp.zeros((2 * TQ, 1), F32), jnp.zeros((2 * TQ, LANES), F32))
        _, den, acc = lax.fori_loop(0, nkeys // step, body, init)
        finish(den, acc)

    for is_ctx, nkeys in ((True, l), (False, t)):
        seg = (i < nct) if is_ctx else (i >= nct)
        pl.when(seg & safe)(functools.partial(bounded, nkeys))
        pl.when(seg & jnp.logical_not(safe))(functools.partial(online, nkeys))


def _diff_attention(qk, v, lam_vecs, g_sub, l, lam_init):
    bn, t, _ = qk.shape
    nt = t // TQ
    ck = 768 if t % 768 == 0 else TQ
    kcol = C_HEADS * 2 * C_HD // LANES
    return pl.pallas_call(
        functools.partial(_diff_kernel, nct=l // TQ, l=l, t=t, ck=ck, lam_init=lam_init),
        scratch_shapes=[pltpu.VMEM((1, LANES), F32)],
        grid=(bn, C_HEADS, nt),
        in_specs=[pl.BlockSpec((4, C_HD), lambda b, h, i: (0, 0)),
                  pl.BlockSpec((1, LANES), lambda b, h, i: (0, 0)),
                  pl.BlockSpec((1, TQ, LANES), lambda b, h, i: (b, i, h)),
                  pl.BlockSpec((1, t, LANES), lambda b, h, i: (b, 0, kcol + h)),
                  pl.BlockSpec((1, t, LANES), lambda b, h, i: (b, 0, h))],
        out_specs=pl.BlockSpec((1, TQ, LANES), lambda b, h, i: (b, i, h)),
        out_shape=jax.ShapeDtypeStruct((bn, t, C_HEADS * 2 * C_HD), BF16),
        compiler_params=_cp(("arbitrary", "arbitrary", "arbitrary"), VMEM_LIMIT),
        name="diff_attention",
    )(lam_vecs.astype(F32), g_sub.reshape(1, LANES).astype(F32), qk, qk, v)


def _proj_delta_kernel(x_ref, mod_ref, g_ref, w_ref, wab_ref, wabt_ref, alog_ref, dtb_ref, alogt_ref, dtbt_ref,
                       p_ref, z_ref, gc_ref, gct_ref, beta_ref, *, nqkv):
    h = _norm_mod(x_ref[0], g_ref[...], mod_ref[0, 0], 0, 1)
    h_hi, h_lo = _split2(h)
    p = _dot(h_hi, w_ref[...])
    p_ref[0] = p[:, :nqkv]
    z_ref[0] = p[:, nqkv:].astype(BF16)
    ab = _dot(h_hi, wab_ref[0]) + _dot(h_hi, wab_ref[1]) + _dot(h_lo, wab_ref[0])
    abt = _dot_nt(wabt_ref[0], h_hi) + _dot_nt(wabt_ref[1], h_hi) + _dot_nt(wabt_ref[0], h_lo)
    nd = 2 * A_HEADS
    g = -jnp.exp(alog_ref[...]) * _softplus(ab[:, :nd] + dtb_ref[...])
    beta_ref[0] = _sigmoid(ab[:, nd:])
    gt = -jnp.exp(alogt_ref[...]) * _softplus(abt[:nd] + dtbt_ref[...])
    ri = _iota((TM, TM), 0)
    ci = _iota((TM, TM), 1)
    same = (ri // A_CHUNK) == (ci // A_CHUNK)
    lbd = (same & (ci <= ri)).astype(BF16)
    ubd = (same & (ci >= ri)).astype(BF16)
    g_hi, g_lo = _split2(g)
    pre = _dot(lbd, g_hi) + _dot(lbd, g_lo)
    suf = _dot(ubd, g_hi) + _dot(ubd, g_lo)
    gc_ref[0] = jnp.where(_iota((TM, nd), 1) < A_HEADS, pre, suf)
    gt_hi, gt_lo = _split2(gt)
    pre_t = _dot(gt_hi, ubd) + _dot(gt_lo, ubd)
    suf_t = _dot(gt_hi, lbd) + _dot(gt_lo, lbd)
    gct_ref[0] = jnp.where(_iota((nd, TM), 0) < A_HEADS, pre_t, suf_t)


def _proj_delta(xs, mod, g, w_main, wab, wabt, a_log, dt_bias, nct_m):
    bn, t, d = xs.shape
    nout = w_main.shape[1]
    nqkv = 2 * A_HEADS * A_DK + A_HEADS * A_DV
    nd = 2 * A_HEADS
    row = lambda a: a.reshape(1, nd).astype(F32)
    col = lambda a: a.reshape(nd, 1).astype(F32)
    full = lambda shape: pl.BlockSpec(shape, lambda b, i: (0,) * len(shape))
    return pl.pallas_call(
        functools.partial(_proj_delta_kernel, nqkv=nqkv),
        grid=(bn, t // TM),
        in_specs=[pl.BlockSpec((1, TM, d), lambda b, i: (b, i, 0)),
                  pl.BlockSpec((1, 1, 6, d), lambda b, i: (b, _seg(i, nct_m), 0, 0)),
                  full((1, d)), full((d, nout)), full((2, d, 2 * nd)), full((2, 2 * nd, d)),
                  full((1, nd)), full((1, nd)), full((nd, 1)), full((nd, 1))],
        out_specs=[pl.BlockSpec((1, TM, nqkv), lambda b, i: (b, i, 0)),
                   pl.BlockSpec((1, TM, nout - nqkv), lambda b, i: (b, i, 0)),
                   pl.BlockSpec((1, TM, nd), lambda b, i: (b, i, 0)),
                   pl.BlockSpec((1, nd, TM), lambda b, i: (b, 0, i)),
                   pl.BlockSpec((1, TM, nd), lambda b, i: (b, i, 0))],
        out_shape=[jax.ShapeDtypeStruct((bn, t, nqkv), F32),
                   jax.ShapeDtypeStruct((bn, t, nout - nqkv), BF16),
                   jax.ShapeDtypeStruct((bn, t, nd), F32),
                   jax.ShapeDtypeStruct((bn, nd, t), F32),
                   jax.ShapeDtypeStruct((bn, t, nd), F32)],
        compiler_params=_cp(("arbitrary", "arbitrary"), VMEM_LIMIT),
        name="proj_delta",
    )(xs, mod, g, w_main, wab, wabt, row(a_log), row(dt_bias), col(a_log), col(dt_bias))


def _conv_kernel(p_ref, w_ref, o_ref, scr, *, l, t, ch, pad):
    c = pl.program_id(1)
    scr[0:pad, :] = jnp.zeros((pad, LANES), F32)
    scr[t + pad:t + 2 * pad, :] = jnp.zeros((pad, LANES), F32)
    scr[pad:t + pad, :] = p_ref[0]
    w = w_ref[...]
    half = w.shape[0] // 2
    for r0 in range(0, t, ch):
        near = (r0 <= l + half) and (r0 + ch >= l - half)
        tt = r0 + _iota((ch, 1), 0)
        acc = None
        for d in range(-half, half + 1):
            xd = scr[pad + r0 + d:pad + r0 + d + ch, :]
            if near and d != 0:
                xd = jnp.where(((tt + d) < l) == (tt < l), xd, 0.0)
            term = xd * w[d + half:d + half + 1]
            acc = term if acc is None else acc + term
        y = _silu(acc)
        nrm = y * lax.rsqrt(jnp.sum(y * y, axis=-1, keepdims=True) + EPS)
        out = jnp.where(c < A_HEADS, nrm * (A_DK ** -0.5), jnp.where(c < 2 * A_HEADS, nrm, y))
        o_ref[0, r0:r0 + ch, :] = out


def _delta_conv(p, conv_w, l):
    bn, t, nq = p.shape
    ch = 384 if t % 384 == 0 else TQ
    pad = 8
    kw = conv_w.shape[0]
    return pl.pallas_call(
        functools.partial(_conv_kernel, l=l, t=t, ch=ch, pad=pad),
        grid=(bn, nq // LANES),
        in_specs=[pl.BlockSpec((1, t, LANES), lambda b, c: (b, 0, c)),
                  pl.BlockSpec((kw, LANES), lambda b, c: (0, c))],
        out_specs=pl.BlockSpec((1, t, LANES), lambda b, c: (b, 0, c)),
        out_shape=jax.ShapeDtypeStruct((bn, t, nq), F32),
        scratch_shapes=[pltpu.VMEM((t + 2 * pad, LANES), F32)],
        compiler_params=_cp(("arbitrary", "arbitrary"), VMEM_LIMIT),
        name="delta_conv",
    )(p, conv_w.astype(F32))


def _merge_masks(ii, jj, lower):
    masks = []
    s = 1
    while s < A_CHUNK:
        grp = (ii // (2 * s)) == (jj // (2 * s))
        odd_i, odd_j = (ii // s) % 2 == 1, (jj // s) % 2 == 1
        masks.append(grp & odd_i & ~odd_j if lower else grp & ~odd_i & odd_j)
        s *= 2
    return masks


def _delta_prep_kernel(q_ref, k_ref, v_ref, gc_ref, gct_ref, beta_ref,
                       uf, ub, wf, wb, qdf, qdb, qkf, qkb, kdtf, kdtb, eg_ref, *, hb):
    hblk = pl.program_id(1)
    ii = _iota((TQ, TQ), 0)
    jj = _iota((TQ, TQ), 1)
    same = (ii // A_CHUNK) == (jj // A_CHUNK)
    eye = (ii == jj).astype(F32)
    merge = (_merge_masks(ii, jj, True), _merge_masks(ii, jj, False))
    tri = ((same & (ii >= jj), same & (ii > jj), (ii // A_CHUNK) * A_CHUNK + (A_CHUNK - 1)),
           (same & (ii <= jj), same & (ii < jj), (ii // A_CHUNK) * A_CHUNK))
    lane_d = _iota((TQ, 2 * A_HEADS), 1)
    gc_all = gc_ref[0]
    beta_all = beta_ref[0]
    outs = ((uf, wf, qdf, qkf, kdtf), (ub, wb, qdb, qkb, kdtb))
    heads = []
    for hh in range(hb):
        sl = slice(hh * LANES, (hh + 1) * LANES)
        q, k, v = q_ref[0, :, sl], k_ref[0, :, sl], v_ref[0, :, sl]
        kb = k.astype(BF16)
        heads.append((sl, q, k, v, _dot_nt(kb, kb), _dot_nt(q.astype(BF16), kb)))
    probs = []
    for hh, (sl, q, k, v, kk, qk) in enumerate(heads):
        for d in range(2):
            incl, strict, last = tri[d]
            idx = d * A_HEADS + hblk * hb + hh
            gcc = jnp.sum(jnp.where(lane_d == idx, gc_all, 0.0), axis=1, keepdims=True)
            bet = jnp.sum(jnp.where(lane_d == idx, beta_all, 0.0), axis=1, keepdims=True)
            gcr = gct_ref[0, pl.ds(idx, 1), :]
            dm = jnp.exp(jnp.where(incl, gcc - gcr, NEG))
            glast = jnp.sum(jnp.where(jj == last, gcr, 0.0), axis=1, keepdims=True)
            a = jnp.where(strict, bet * kk * dm, 0.0)
            probs.append((hh, d, gcc, bet, dm, glast, a))
    ts = [eye - jnp.where(merge[p[1]][0], p[6], 0.0) for p in probs]
    for lvl in range(1, len(merge[0])):
        tbs = [t.astype(BF16) for t in ts]
        ys = [_dot(tb, jnp.where(merge[p[1]][lvl], p[6], 0.0).astype(BF16)) for tb, p in zip(tbs, probs)]
        xs = [_dot(y.astype(BF16), tb) for y, tb in zip(ys, tbs)]
        ts = [t - x for t, x in zip(ts, xs)]
    sols = []
    for t, (hh, d, gcc, bet, dm, glast, a) in zip(ts, probs):
        _, q, k, v, kk, qk = heads[hh]
        rhs = jnp.concatenate([v * bet, k * (bet * jnp.exp(gcc))], axis=1).astype(BF16)
        sols.append(_dot(t.astype(BF16), rhs))
    for sol, (hh, d, gcc, bet, dm, glast, a) in zip(sols, probs):
        sl, q, k, v, kk, qk = heads[hh]
        u_o, w_o, qd_o, qk_o, kdt_o = outs[d]
        u_o[0, :, sl] = sol[:, :LANES]
        w_o[0, :, sl] = sol[:, LANES:].astype(BF16)
        qd_o[0, :, sl] = (q * jnp.exp(gcc)).astype(BF16)
        qk_o[0, :, sl] = (qk * dm).astype(BF16)
        kdt_o[0, sl, :] = (k * jnp.exp(glast - gcc)).T.astype(BF16)
        eglast = jnp.exp(glast)
        for cch in range(TQ // A_CHUNK):
            row = (d * hb + hh) * (TQ // A_CHUNK) + cch
            eg_ref[0, 0, 0, row:row + 1, :] = jnp.broadcast_to(eglast[cch * A_CHUNK:cch * A_CHUNK + 1], (1, LANES))


def _delta_prep(qkv, gc, gct, beta, hb):
    bn, t, _ = qkv.shape
    nt = t // TQ
    nhb = A_HEADS // hb
    wdt = A_HEADS * A_DV
    tok = lambda off: pl.BlockSpec((1, TQ, hb * LANES), lambda b, h, i: (b, i, off + h))
    nd = 2 * A_HEADS
    tok_shape = lambda dt: jax.ShapeDtypeStruct((bn, t, wdt), dt)
    return pl.pallas_call(
        functools.partial(_delta_prep_kernel, hb=hb),
        grid=(bn, nhb, nt),
        in_specs=[tok(0), tok(nhb), tok(2 * nhb),
                  pl.BlockSpec((1, TQ, nd), lambda b, h, i: (b, i, 0)),
                  pl.BlockSpec((1, nd, TQ), lambda b, h, i: (b, 0, i)),
                  pl.BlockSpec((1, TQ, nd), lambda b, h, i: (b, i, 0))],
        out_specs=[tok(0)] * 8 + [pl.BlockSpec((1, hb * LANES, TQ), lambda b, h, i: (b, h, i))] * 2
                  + [pl.BlockSpec((1, 1, 1, 4 * hb, LANES), lambda b, h, i: (b, h, i, 0, 0))],
        out_shape=[tok_shape(F32), tok_shape(F32)] + [tok_shape(BF16)] * 6
                  + [jax.ShapeDtypeStruct((bn, wdt, t), BF16)] * 2
                  + [jax.ShapeDtypeStruct((bn, nhb, nt, 4 * hb, LANES), F32)],
        compiler_params=_cp(("arbitrary", "arbitrary", "arbitrary"), VMEM_LIMIT),
        name="delta_prep",
    )(qkv, qkv, qkv, gc, gct, beta)


def _delta_scan_kernel(uf, wf, qdf, qkf, kdtf, egf, ub, wb, qdb, qkb, kdtb, egb, of_ref, ob_ref, s_ref, *, hb):
    @pl.when(pl.program_id(1) == 0)
    def _():
        s_ref[...] = jnp.zeros(s_ref.shape, F32)

    zeros = jnp.zeros((A_CHUNK, LANES), BF16)
    dirs = ((uf, wf, qdf, qkf, kdtf, egf, of_ref, (0, 1)), (ub, wb, qdb, qkb, kdtb, egb, ob_ref, (1, 0)))
    chains = [(d, head) for d in range(2) for head in range(A_HEADS)]
    states = [s_ref[d, head] for d, head in chains]
    for step in range(TQ // A_CHUNK):
        ws_all = []
        for (d, head), s in zip(chains, states):
            u, w, qd, qk, kdt, eg, o_ref, order = dirs[d]
            rs = slice(order[step] * A_CHUNK, (order[step] + 1) * A_CHUNK)
            sl = slice(head * LANES, (head + 1) * LANES)
            ws_all.append(_dot(jnp.concatenate([w[0, rs, sl], qd[0, rs, sl]], axis=0), s.astype(BF16)))
        vfulls = []
        for (d, head), ws in zip(chains, ws_all):
            u, w, qd, qk, kdt, eg, o_ref, order = dirs[d]
            cch = order[step]
            rs = slice(cch * A_CHUNK, (cch + 1) * A_CHUNK)
            sl = slice(head * LANES, (head + 1) * LANES)
            vn = (u[0, rs, sl] - ws[:A_CHUNK]).astype(BF16)
            vfull = jnp.concatenate([vn, zeros] if cch == 0 else [zeros, vn], axis=0)
            vfulls.append(vfull)
            o_ref[0, rs, sl] = ws[A_CHUNK:] + _dot(qk[0, rs, sl], vfull)
        new_states = []
        for (d, head), s, vfull in zip(chains, states, vfulls):
            u, w, qd, qk, kdt, eg, o_ref, order = dirs[d]
            hblk, hh = divmod(head, hb)
            row = (d * hb + hh) * 2 + order[step]
            sl = slice(head * LANES, (head + 1) * LANES)
            new_states.append(s * eg[0, hblk, 0, row:row + 1, :] + _dot(kdt[0, sl, :], vfull))
        states = new_states
    for (d, head), s in zip(chains, states):
        s_ref[d, head] = s


def _delta_scan(prep, l, hb):
    uf, ub, wf, wb, qdf, qdb, qkf, qkb, kdtf, kdtb, eg = prep
    bn, t, wdt = uf.shape
    nt = t // TQ
    nct = l // TQ
    nhb = A_HEADS // hb
    fwd = lambda s: s
    bwd = lambda s: jnp.where(s < nct, nct - 1 - s, nt - 1 - (s - nct))
    tok = lambda f: pl.BlockSpec((1, TQ, wdt), lambda b, s: (b, f(s), 0))
    tr = lambda f: pl.BlockSpec((1, wdt, TQ), lambda b, s: (b, 0, f(s)))
    egs = lambda f: pl.BlockSpec((1, nhb, 1, 4 * hb, LANES), lambda b, s: (b, 0, f(s), 0, 0))
    return pl.pallas_call(
        functools.partial(_delta_scan_kernel, hb=hb),
        grid=(bn, nt),
        in_specs=[tok(fwd)] * 4 + [tr(fwd), egs(fwd)] + [tok(bwd)] * 4 + [tr(bwd), egs(bwd)],
        out_specs=[tok(fwd), tok(bwd)],
        out_shape=[jax.ShapeDtypeStruct((bn, t, wdt), F32)] * 2,
        scratch_shapes=[pltpu.VMEM((2, A_HEADS, A_DK, A_DV), F32)],
        compiler_params=_cp(("arbitrary", "arbitrary"), VMEM_LIMIT),
        name="delta_scan",
    )(uf, wf, qdf, qkf, kdtf, eg, ub, wb, qdb, qkb, kdtb, eg)


def _residual_router(y, w_ref, x_ref, mod_ref, gffn_ref, wr_ref, xo_ref, h_ref, aff_ref):
    mod = mod_ref[0, 0]
    xn = x_ref[0] + mod[2:3] * _dot(y, w_ref[...])
    xo_ref[0] = xn
    h = _norm_mod(xn, gffn_ref[...], mod, 3, 4)
    h_hi, h_lo = _split2(h)
    h_ref[0] = h_hi
    lg = _dot_nt(wr_ref[0], h_hi) + _dot_nt(wr_ref[1], h_hi) + _dot_nt(wr_ref[0], h_lo)
    e = jnp.exp(lg - jnp.max(lg, axis=0, keepdims=True))
    aff_ref[0] = e / jnp.sum(e, axis=0, keepdims=True)


def _out_kernel(y_ref, *rest):
    _residual_router(y_ref[0], *rest)


def _out_delta_kernel(of_ref, ob_ref, z_ref, gout_ref, *rest):
    o = of_ref[0] + ob_ref[0]
    parts = []
    for hd in range(A_HEADS):
        sl = slice(hd * A_DV, (hd + 1) * A_DV)
        oh = o[:, sl]
        ms = jnp.mean(oh * oh, axis=-1, keepdims=True)
        parts.append((oh * lax.rsqrt(ms + EPS) * gout_ref[...] * _silu(z_ref[0, :, sl].astype(F32))).astype(BF16))
    _residual_router(jnp.concatenate(parts, axis=1), *rest)


def _out_proj(pre, w_out, xs, mod, g_ffn, wr, nct_m, delta):
    bn, t, d = xs.shape
    k = w_out.shape[0]
    ne = wr.shape[1]
    tok = lambda width: pl.BlockSpec((1, TM, width), lambda b, i: (b, i, 0))
    full = lambda shape: pl.BlockSpec(shape, lambda b, i: (0,) * len(shape))
    if delta:
        of, ob, z, gout = pre
        head_specs = [tok(k), tok(k), tok(k), full((1, A_DV))]
        head_args = (of, ob, z, gout.reshape(1, A_DV).astype(F32))
        body = _out_delta_kernel
    else:
        head_specs = [tok(k)]
        head_args = (pre,)
        body = _out_kernel
    return pl.pallas_call(
        body,
        grid=(bn, t // TM),
        in_specs=head_specs + [full((k, d)), tok(d),
                               pl.BlockSpec((1, 1, 6, d), lambda b, i: (b, _seg(i, nct_m), 0, 0)),
                               full((1, d)), full((2, ne, d))],
        out_specs=[tok(d), tok(d), pl.BlockSpec((1, ne, TM), lambda b, i: (b, 0, i))],
        out_shape=[jax.ShapeDtypeStruct((bn, t, d), F32), jax.ShapeDtypeStruct((bn, t, d), BF16),
                   jax.ShapeDtypeStruct((bn, ne, t), F32)],
        compiler_params=_cp(("arbitrary", "arbitrary"), VMEM_LIMIT),
        name="out_delta" if delta else "out_proj",
    )(*head_args, w_out, xs, mod, g_ffn, wr)


def _kth_largest_bits(bits, k):
    def body(it, thr):
        cand = thr | jnp.left_shift(jnp.int32(1), 30 - it)
        cnt = jnp.sum((bits >= cand).astype(I32), axis=1, keepdims=True)
        return jnp.where(cnt >= k, cand, thr)
    return lax.fori_loop(0, 31, body, jnp.zeros((bits.shape[0], 1), I32))


def _select_kernel(aff_ref, posd_ref, offs_ref, posc_ref, gatec_ref, *, l, t, cap_c, cap_x):
    a = aff_ref[0]
    ne = a.shape[0]
    bits = pltpu.bitcast(a, I32)
    upper = (_iota((LANES, LANES), 0) <= _iota((LANES, LANES), 1)).astype(BF16)
    ident = (_iota((LANES, LANES), 0) == _iota((LANES, LANES), 1)).astype(BF16)
    nt = t // LANES
    sel = [None] * nt
    for s0, s1, cap in ((0, l, cap_c), (l, t, cap_x)):
        bseg = bits[:, s0:s1]
        thr = _kth_largest_bits(bseg, cap)
        gtf = jnp.where(bseg > thr, 1.0, 0.0)
        eqf = jnp.where(bseg == thr, 1.0, 0.0)
        need = cap - jnp.sum(gtf, axis=1, keepdims=True)
        run = jnp.zeros((ne, 1), F32)
        for j in range((s1 - s0) // LANES):
            ej = eqf[:, j * LANES:(j + 1) * LANES]
            inc = _dot(ej.astype(BF16), upper)
            keep = jnp.where(inc - ej + run < need, ej, 0.0)
            sel[s0 // LANES + j] = jnp.maximum(gtf[:, j * LANES:(j + 1) * LANES], keep)
            run = run + inc[:, LANES - 1:LANES]
    run = jnp.zeros((ne, 1), F32)
    offs = jnp.zeros((ne, LANES), I32)
    lane = _iota((ne, LANES), 1)
    for j in range(nt):
        sj = sel[j]
        inc = _dot(sj.astype(BF16), upper)
        pos = jnp.where(sj > 0.0, inc - sj + run, -1.0)
        posd_ref[0, j] = pos.astype(I32)
        offs = jnp.where(lane == j, run.astype(I32), offs)
        p_hi, p_lo = _split2(pos)
        posc_ref[0, j * LANES:(j + 1) * LANES, :] = (_dot_nt(ident, p_hi) + _dot_nt(ident, p_lo)).astype(I32)
        gj = jnp.where(sj > 0.0, a[:, j * LANES:(j + 1) * LANES], 0.0)
        g1 = gj.astype(BF16)
        r1 = gj - g1.astype(F32)
        g2 = r1.astype(BF16)
        g3 = (r1 - g2.astype(F32)).astype(BF16)
        gatec_ref[0, j * LANES:(j + 1) * LANES, :] = _dot_nt(ident, g1) + _dot_nt(ident, g2) + _dot_nt(ident, g3)
        run = run + inc[:, LANES - 1:LANES]
    offs_ref[0] = jnp.where(lane == nt, run.astype(I32), offs)


def _select(aff, l, cap_c, cap_x):
    bn, ne, t = aff.shape
    nt = t // LANES
    return pl.pallas_call(
        functools.partial(_select_kernel, l=l, t=t, cap_c=cap_c, cap_x=cap_x),
        grid=(bn,),
        in_specs=[pl.BlockSpec((1, ne, t), lambda b: (b, 0, 0))],
        out_specs=[pl.BlockSpec((1, nt, ne, LANES), lambda b: (b, 0, 0, 0)),
                   pl.BlockSpec((1, ne, LANES), lambda b: (b, 0, 0)),
                   pl.BlockSpec((1, t, ne), lambda b: (b, 0, 0)),
                   pl.BlockSpec((1, t, ne), lambda b: (b, 0, 0))],
        out_shape=[jax.ShapeDtypeStruct((bn, nt, ne, LANES), I32), jax.ShapeDtypeStruct((bn, ne, LANES), I32),
                   jax.ShapeDtypeStruct((bn, t, ne), I32), jax.ShapeDtypeStruct((bn, t, ne), F32)],
        compiler_params=_cp(("arbitrary",), VMEM_LIMIT),
        name="route_select",
    )(aff)


GATHER_TOK = 2 * TQ
GATHER_WIN = 64
COMBINE_WIN = TQ + 16


def _expert_kernel(offs_ref, h_ref, pos_ref, wgu_ref, wd_ref, y_ref, xg_ref, acc_ref, *, nt, r, fc):
    b = pl.program_id(0)
    e = pl.program_id(1)
    ne = pl.num_programs(1)
    xg_ref[...] = jnp.zeros(xg_ref.shape, F32)
    riota = _iota((GATHER_WIN, GATHER_TOK), 0)
    tpt = GATHER_TOK // LANES
    base = (b * ne + e) * LANES

    def tile(j, carry):
        off = offs_ref[base + j * tpt]
        end = offs_ref[base + (j + 1) * tpt]
        aoff = (off // 8) * 8
        pos = jnp.concatenate([pos_ref[0, j * tpt + c, pl.ds(e, 1), :] for c in range(tpt)], axis=1)
        hj = h_ref[0, pl.ds(pl.multiple_of(j * GATHER_TOK, GATHER_TOK), GATHER_TOK), :]

        def window(w, carry2):
            start = pl.multiple_of(aoff + w * GATHER_WIN, 8)
            onehot = jnp.where(riota == pos - start, 1.0, 0.0).astype(BF16)
            xg_ref[pl.ds(start, GATHER_WIN), :] += _dot(onehot, hj)
            return carry2

        lax.fori_loop(0, (end - aoff + GATHER_WIN - 1) // GATHER_WIN, window, 0)
        return carry

    lax.fori_loop(0, nt // tpt, tile, 0)
    xg = xg_ref[0:r, :].astype(BF16)
    f = wd_ref.shape[1]
    for c in range(f // fc):
        g = _dot(xg, wgu_ref[0, :, c * fc:(c + 1) * fc])
        u = _dot(xg, wgu_ref[0, :, f + c * fc:f + (c + 1) * fc])
        part = _dot((_silu(g) * u).astype(BF16), wd_ref[0, c * fc:(c + 1) * fc, :])
        if c == 0:
            acc_ref[...] = part
        else:
            acc_ref[...] += part
    y_ref[0, 0] = acc_ref[...].astype(BF16)


def _experts(offs, hf, posd, wgu, wd, r):
    bn, t, d = hf.shape
    ne, _, f2 = wgu.shape
    nt = t // TQ
    xg_rows = -(-(r + GATHER_WIN) // 8) * 8
    return pl.pallas_call(
        functools.partial(_expert_kernel, nt=nt, r=r, fc=min(512, f2 // 2)),
        grid_spec=pltpu.PrefetchScalarGridSpec(
            num_scalar_prefetch=1,
            grid=(bn, ne),
            in_specs=[pl.BlockSpec((1, t, d), lambda b, e, o: (b, 0, 0), pipeline_mode=pl.Buffered(1)),
                      pl.BlockSpec((1, nt, ne, LANES), lambda b, e, o: (b, 0, 0, 0)),
                      pl.BlockSpec((1, d, f2), lambda b, e, o: (e, 0, 0)),
                      pl.BlockSpec((1, f2 // 2, d), lambda b, e, o: (e, 0, 0))],
            out_specs=pl.BlockSpec((1, 1, r, d), lambda b, e, o: (b, e, 0, 0)),
            scratch_shapes=[pltpu.VMEM((xg_rows, d), F32), pltpu.VMEM((r, d), F32)]),
        out_shape=jax.ShapeDtypeStruct((bn, ne, r, d), BF16),
        compiler_params=_cp(("arbitrary", "arbitrary"), VMEM_LIMIT),
        name="experts",
    )(offs, hf, posd, wgu, wd)


def _combine_kernel(offs_ref, y_ref, posc_ref, gate_ref, x_ref, mod_ref, o_ref, *, r, win):
    b = pl.program_id(0)
    j = pl.program_id(1)
    ne = y_ref.shape[1]
    liota = _iota((TQ, win), 1)
    pc = posc_ref[0]
    gt = gate_ref[0]
    acc = jnp.zeros(x_ref.shape[1:], F32)
    for e in range(ne):
        off = offs_ref[(b * ne + e) * LANES + j]
        aoff = pl.multiple_of(jnp.minimum((off // 16) * 16, r - win), 16)
        onehot = jnp.where(liota == pc[:, e:e + 1] - aoff, 1.0, 0.0).astype(BF16)
        acc = acc + gt[:, e:e + 1] * _dot(onehot, y_ref[0, e, pl.ds(aoff, win), :])
    o_ref[0] = x_ref[0] + mod_ref[0, 0][5:6] * acc


def _combine(offs, y, posc, gatec, xs, mod, nct):
    bn, t, d = xs.shape
    ne, r = y.shape[1], y.shape[2]
    win = min(COMBINE_WIN, r)
    tok = lambda width: pl.BlockSpec((1, TQ, width), lambda b, j, o: (b, j, 0))
    return pl.pallas_call(
        functools.partial(_combine_kernel, r=r, win=win),
        grid_spec=pltpu.PrefetchScalarGridSpec(
            num_scalar_prefetch=1,
            grid=(bn, t // TQ),
            in_specs=[pl.BlockSpec((1, ne, r, d), lambda b, j, o: (b, 0, 0, 0), pipeline_mode=pl.Buffered(1)),
                      tok(ne), tok(ne), tok(d),
                      pl.BlockSpec((1, 1, 6, d), lambda b, j, o: (b, _seg(j, nct), 0, 0))],
            out_specs=tok(d)),
        out_shape=jax.ShapeDtypeStruct((bn, t, d), F32),
        compiler_params=_cp(("arbitrary", "arbitrary"), VMEM_LIMIT),
        name="moe_combine",
    )(offs, y, posc, gatec, xs, mod)


def _moe(xs, hf, aff, mod, wgu, wd, l):
    bn, t, _ = xs.shape
    cap_c = EC_CAPACITY * l // N_EXPERTS
    cap_x = EC_CAPACITY * (t - l) // N_EXPERTS
    posd, offs, posc, gatec = _select(aff, l, cap_c, cap_x)
    offs = offs.reshape(-1)
    y = _experts(offs, hf, posd, wgu, wd, cap_c + cap_x)
    return _combine(offs, y, posc, gatec, xs, mod, l // TQ)


def _hi_lo(w):
    hi = w.astype(BF16)
    return jnp.stack([hi, (w - hi.astype(F32)).astype(BF16)])


def _mixer_delta(xs, mod, g_mix, w_in, conv_w, a_log, dt_bias, l, hb=4):
    nqkvz = 2 * A_HEADS * A_DK + 2 * A_HEADS * A_DV
    wab = w_in[:, nqkvz:]
    p, z, gc, gct, beta = _proj_delta(xs, mod, g_mix, w_in[:, :nqkvz].astype(BF16), _hi_lo(wab), _hi_lo(wab.T),
                                      a_log, dt_bias, l // TM)
    qkv = _delta_conv(p, conv_w, l)
    of, ob = _delta_scan(_delta_prep(qkv, gc, gct, beta, hb), l, hb)
    return of, ob, z


def _mixer_swa(xs, mod, g_mix, w_in, qn, kn, sink, cosf, sinf, l):
    nq = B_QHEADS * B_HD
    nk = B_KVHEADS * B_HD
    dup = lambda w: jnp.concatenate([w.reshape(-1, B_KVHEADS, 1, B_HD)] * 2, axis=2).reshape(-1, 2 * nk)
    w = jnp.concatenate([w_in[:, :nq], dup(w_in[:, nq:nq + nk]), dup(w_in[:, nq + nk:])], axis=1).astype(BF16)
    gain = jnp.concatenate([jnp.tile(qn, B_QHEADS) * (B_HD ** -0.5), jnp.tile(kn, 2 * B_KVHEADS)])[None].astype(F32)
    qk, v2 = _proj_qk(xs, mod, g_mix, w, gain, cosf, sinf, nq + 2 * nk, l // TM)
    return _swa_attention(qk, v2, sink, l)


def _mixer_diff(xs, mod, g_mix, w_in, qn, kn, lam_vecs, g_sub, cosf, sinf, l, lam_init):
    nqk = C_HEADS * 2 * C_HD
    gain = jnp.concatenate([jnp.tile(qn, 2 * C_HEADS) * (C_HD ** -0.5 * LOG2E),
                            jnp.tile(kn, 2 * C_HEADS)])[None].astype(F32)
    qk, v = _proj_qk(xs, mod, g_mix, w_in.astype(BF16), gain, cosf, sinf, 2 * nqk, l // TM)
    return _diff_attention(qk, v, lam_vecs, g_sub, l, lam_init)


def kernel(x, c, ctx, c_ctx, w_ada, b_ada, g_mix, g_ffn, a_w_in, a_conv, a_log, a_dt_bias, a_g_out, a_w_out,
           b_w_in, b_q_norm, b_k_norm, b_sink, b_w_out, c_w_in, c_q_norm, c_k_norm, c_lambda, c_g_sub, c_w_out,
           w_router, w_gate_up, w_down):
    depth = w_ada.shape[0]
    n = x.shape[1]
    l = ctx.shape[1]
    assert l % TM == 0 and n % TM == 0
    xs = jnp.concatenate([ctx, x], axis=1)
    mods = _adaln(c, c_ctx, w_ada, b_ada)
    cos_b, sin_b = _rope_tables(n, l, B_HD)
    cos_c, sin_c = _rope_tables(n, l, C_HD)
    nct_m = l // TM
    for layer in range(depth):
        kind, j = layer % N_MIXERS, layer // N_MIXERS
        mod = mods[layer]
        gm = g_mix[layer][None].astype(F32)
        gf = g_ffn[layer][None].astype(F32)
        wr = _hi_lo(w_router[layer].T)
        if kind == 0:
            pre = _mixer_delta(xs, mod, gm, a_w_in[j], a_conv[j], a_log[j], a_dt_bias[j], l)
            xs, hf, aff = _out_proj(pre + (a_g_out[j],), a_w_out[j].astype(BF16), xs, mod, gf, wr, nct_m, True)
        elif kind == 1:
            pre = _mixer_swa(xs, mod, gm, b_w_in[j], b_q_norm[j], b_k_norm[j], b_sink[j], cos_b, sin_b, l)
            xs, hf, aff = _out_proj(pre, b_w_out[j].astype(BF16), xs, mod, gf, wr, nct_m, False)
        else:
            lam_init = 0.8 - 0.6 * math.exp(-0.3 * layer)
            pre = _mixer_diff(xs, mod, gm, c_w_in[j], c_q_norm[j], c_k_norm[j], c_lambda[j], c_g_sub[j],
                              cos_c, sin_c, l, lam_init)
            xs, hf, aff = _out_proj(pre, c_w_out[j].astype(BF16), xs, mod, gf, wr, nct_m, False)
        xs = _moe(xs, hf, aff, mod, w_gate_up[layer].astype(BF16), w_down[layer].astype(BF16), l)
    return xs[:, l:]
```

```python
import functools
import math

import jax
import jax.numpy as jnp
from jax import lax
from jax.experimental import pallas as pl
from jax.experimental.pallas import tpu as pltpu

F32 = jnp.float32
BF16 = jnp.bfloat16
I32 = jnp.int32

EPS = 1e-6
ROPE_BASE = 10000.0
GRID_W = 64
N_MIXERS = 3

A_HEADS, A_DK, A_DV, A_CHUNK = 8, 128, 128, 64
B_QHEADS, B_KVHEADS, B_HD = 16, 4, 64
B_GROUP = B_QHEADS // B_KVHEADS
C_HEADS, C_HD = 8, 64
N_EXPERTS, EC_CAPACITY = 16, 2

LANES = 128
TM = 256
TQ = 128
NEG = -1e30
VMEM_LIMIT = 56 * 1024 * 1024


def _cp(sem, vmem=None):
    return pltpu.CompilerParams(dimension_semantics=sem, vmem_limit_bytes=vmem)


def _dot(a, b):
    return jnp.dot(a, b, preferred_element_type=F32)


def _dot_nt(a, b):
    return lax.dot_general(a, b, (((1,), (1,)), ((), ())), preferred_element_type=F32)


def _split2(a):
    hi = a.astype(BF16)
    return hi, (a - hi.astype(F32)).astype(BF16)


def _sigmoid(x):
    return 1.0 / (1.0 + jnp.exp(-x))


def _silu(x):
    return x * _sigmoid(x)


def _softplus(x):
    return jnp.maximum(x, 0.0) + jnp.log(1.0 + jnp.exp(-jnp.abs(x)))


def _norm_mod(x, g, mod, i_shift, i_scale):
    ms = jnp.mean(x * x, axis=-1, keepdims=True)
    y = x * lax.rsqrt(ms + EPS) * g
    return y * (1.0 + mod[i_scale:i_scale + 1]) + mod[i_shift:i_shift + 1]


def _iota(shape, dim):
    return lax.broadcasted_iota(I32, shape, dim)


def _ada_kernel(s_ref, w_ref, b_ref, o_ref):
    s = _silu(s_ref[...])
    s_hi, s_lo = _split2(s)
    w_hi, w_lo = _split2(w_ref[0])
    o_ref[0] = _dot(s_hi, w_hi) + _dot(s_hi, w_lo) + _dot(s_lo, w_hi) + b_ref[0]


def _adaln(c, c_ctx, w_ada, b_ada):
    depth, d, d6 = w_ada.shape
    bn = c.shape[0]
    rows_n = -(-(bn + 1) // 8) * 8
    rows = jnp.zeros((rows_n, d), F32).at[:bn].set(c).at[bn].set(c_ctx)
    nb = d6 // 4
    out = pl.pallas_call(
        _ada_kernel,
        grid=(depth, d6 // nb),
        in_specs=[pl.BlockSpec((rows_n, d), lambda l, j: (0, 0)),
                  pl.BlockSpec((1, d, nb), lambda l, j: (l, 0, j)),
                  pl.BlockSpec((1, 1, nb), lambda l, j: (l, 0, j))],
        out_specs=pl.BlockSpec((1, rows_n, nb), lambda l, j: (l, 0, j)),
        out_shape=jax.ShapeDtypeStruct((depth, rows_n, d6), F32),
        compiler_params=_cp(("arbitrary", "arbitrary"), VMEM_LIMIT),
        name="adaln",
    )(rows, w_ada, b_ada.reshape(depth, 1, d6))
    mx = out[:, :bn].reshape(depth, bn, 6, d)
    mc = jnp.broadcast_to(out[:, bn].reshape(depth, 1, 6, d), (depth, bn, 6, d))
    return jnp.stack([mc, mx], axis=2)


def _seg(i, nct):
    return jnp.where(i < nct, 0, 1)


def _proj_qk_kernel(x_ref, mod_ref, g_ref, w_ref, gain_ref, cos_ref, sin_ref, qk_ref, v_ref, *, nqk):
    h = _norm_mod(x_ref[0], g_ref[...], mod_ref[0, 0], 0, 1)
    p = _dot(h.astype(BF16), w_ref[...])
    grp = (_iota((LANES, LANES), 0) // 64 == _iota((LANES, LANES), 1) // 64).astype(BF16)
    first = (_iota((1, LANES), 1) % 64) < 32
    cs = cos_ref[...]
    sn = sin_ref[...]
    for t in range(nqk // LANES):
        sl = slice(t * LANES, (t + 1) * LANES)
        xt = p[:, sl]
        sq_hi, sq_lo = _split2(xt * xt)
        ms = (_dot(sq_hi, grp) + _dot(sq_lo, grp)) * (1.0 / 64)
        y = xt * lax.rsqrt(ms + EPS) * gain_ref[:, sl]
        rot = jnp.where(first, pltpu.roll(y, 96, 1), pltpu.roll(y, 32, 1))
        qk_ref[0, :, sl] = (y * cs + rot * sn).astype(BF16)
    v_ref[0] = p[:, nqk:].astype(BF16)


def _proj_qk(xs, mod, g, w, gain, cosf, sinf, nqk, nct_m):
    bn, t, d = xs.shape
    nout = w.shape[1]
    return pl.pallas_call(
        functools.partial(_proj_qk_kernel, nqk=nqk),
        grid=(bn, t // TM),
        in_specs=[pl.BlockSpec((1, TM, d), lambda b, i: (b, i, 0)),
                  pl.BlockSpec((1, 1, 6, d), lambda b, i: (b, _seg(i, nct_m), 0, 0)),
                  pl.BlockSpec((1, d), lambda b, i: (0, 0)),
                  pl.BlockSpec((d, nout), lambda b, i: (0, 0)),
                  pl.BlockSpec((1, nqk), lambda b, i: (0, 0)),
                  pl.BlockSpec((TM, LANES), lambda b, i: (i, 0)),
                  pl.BlockSpec((TM, LANES), lambda b, i: (i, 0))],
        out_specs=[pl.BlockSpec((1, TM, nqk), lambda b, i: (b, i, 0)),
                   pl.BlockSpec((1, TM, nout - nqk), lambda b, i: (b, i, 0))],
        out_shape=[jax.ShapeDtypeStruct((bn, t, nqk), BF16),
                   jax.ShapeDtypeStruct((bn, t, nout - nqk), BF16)],
        compiler_params=_cp(("arbitrary", "arbitrary"), VMEM_LIMIT),
        name="proj_qk",
    )(xs, mod, g, w, gain, cosf, sinf)


def _rope_tables(n, l, head_dim):
    t = jnp.arange(n)
    n_freq = head_dim // 4
    inv = ROPE_BASE ** (-jnp.arange(n_freq, dtype=F32) / n_freq)
    ang = jnp.concatenate([(t // GRID_W).astype(F32)[:, None] * inv, (t % GRID_W).astype(F32)[:, None] * inv], -1)
    cs, sn = jnp.cos(ang), jnp.sin(ang)
    reps = LANES // head_dim
    cosf = jnp.tile(jnp.concatenate([cs, cs], -1), (1, reps))
    sinf = jnp.tile(jnp.concatenate([-sn, sn], -1), (1, reps))
    cosf = jnp.concatenate([jnp.ones((l, LANES), F32), cosf], 0)
    sinf = jnp.concatenate([jnp.zeros((l, LANES), F32), sinf], 0)
    return cosf, sinf


def _swa_kernel(q_ref, kp_ref, kc_ref, kn_ref, kx_ref, vp_ref, vc_ref, vn_ref, vx_ref, sink_ref, bias_ref, o_ref,
                *, nct, ntx, l):
    xi = pl.program_id(2) - nct
    q = q_ref[0]
    lo = _iota((TQ, LANES), 1) < 64
    zero = jnp.zeros((TQ, LANES), BF16)
    qa, qb = q[:, :LANES], q[:, LANES:]
    q4 = jnp.concatenate([jnp.where(lo, qa, zero), jnp.where(lo, zero, qa),
                          jnp.where(lo, qb, zero), jnp.where(lo, zero, qb)], axis=0)
    kcat = jnp.concatenate([kp_ref[0], kc_ref[0], kn_ref[0], kx_ref[0]], axis=0)
    c = _iota((1, 3 * TQ + l), 1)
    dead = (((c < TQ) & (xi < 1)) | ((c >= 2 * TQ) & (c < 3 * TQ) & (xi + 1 >= ntx)) | ((c < 3 * TQ) & (xi < 0)))
    bias = bias_ref[...] + jnp.where(dead, NEG, 0.0)
    s = _dot_nt(q4, kcat) + jnp.concatenate([bias] * B_GROUP, axis=0)
    sk = sink_ref[0]
    m = jnp.maximum(jnp.max(s, axis=-1, keepdims=True), sk)
    p = jnp.exp(s - m)
    den = jnp.sum(p, axis=-1, keepdims=True) + jnp.exp(sk - m)
    vcat = jnp.concatenate([vp_ref[0], vc_ref[0], vn_ref[0], vx_ref[0]], axis=0)
    o4 = _dot(p.astype(BF16), vcat) / den
    oa = jnp.where(lo, o4[0:TQ], o4[TQ:2 * TQ])
    ob = jnp.where(lo, o4[2 * TQ:3 * TQ], o4[3 * TQ:4 * TQ])
    o_ref[0] = jnp.concatenate([oa, ob], axis=1).astype(BF16)


def _swa_attention(qk, v2, sink, l):
    bn, t, _ = qk.shape
    nt = t // TQ
    nct = l // TQ
    gw = B_GROUP * B_HD
    kcol = B_QHEADS * B_HD // LANES
    prev = lambda b, kv, i: jnp.maximum(i - 1, 0)
    nxt = lambda b, kv, i: jnp.minimum(i + 1, nt - 1)
    kspec = lambda f: pl.BlockSpec((1, TQ, LANES), lambda b, kv, i: (b, f(b, kv, i), kcol + kv))
    vspec = lambda f: pl.BlockSpec((1, TQ, LANES), lambda b, kv, i: (b, f(b, kv, i), kv))
    cur = lambda b, kv, i: i
    sinkcol = jnp.repeat(sink.reshape(B_KVHEADS, B_GROUP), TQ, axis=1).reshape(B_KVHEADS, B_GROUP * TQ, 1).astype(F32)
    r = jnp.arange(TQ)[:, None]
    c = jnp.arange(3 * TQ + l)[None, :]
    band = ((c < TQ) & (c >= r)) | ((c >= TQ) & (c < 2 * TQ)) | ((c >= 2 * TQ) & (c - 2 * TQ <= r)) | (c >= 3 * TQ)
    bias = jnp.where(band, 0.0, NEG).astype(F32)
    return pl.pallas_call(
        functools.partial(_swa_kernel, nct=nct, ntx=nt - nct, l=l),
        grid=(bn, B_KVHEADS, nt),
        in_specs=[pl.BlockSpec((1, TQ, gw), lambda b, kv, i: (b, i, kv)),
                  kspec(prev), kspec(cur), kspec(nxt),
                  pl.BlockSpec((1, l, LANES), lambda b, kv, i: (b, 0, kcol + kv)),
                  vspec(prev), vspec(cur), vspec(nxt),
                  pl.BlockSpec((1, l, LANES), lambda b, kv, i: (b, 0, kv)),
                  pl.BlockSpec((1, B_GROUP * TQ, 1), lambda b, kv, i: (kv, 0, 0)),
                  pl.BlockSpec((TQ, 3 * TQ + l), lambda b, kv, i: (0, 0))],
        out_specs=pl.BlockSpec((1, TQ, gw), lambda b, kv, i: (b, i, kv)),
        out_shape=jax.ShapeDtypeStruct((bn, t, B_QHEADS * B_HD), BF16),
        compiler_params=_cp(("arbitrary", "arbitrary", "arbitrary"), VMEM_LIMIT),
        name="swa_attention",
    )(qk, qk, qk, qk, qk, v2, v2, v2, v2, sinkcol, bias)


LOG2E = 1.4426950408889634
SAFE_BOUND = 60.0


def _diff_kernel(lam_ref, gsub_ref, q_ref, k_ref, v_ref, o_ref, kn_ref, *, nct, l, t, ck, lam_init):
    i = pl.program_id(2)
    q = q_ref[0]
    tq = q.shape[0]
    lo = _iota((tq, LANES), 1) < 64
    zero = jnp.zeros((tq, LANES), BF16)
    q2 = jnp.concatenate([jnp.where(lo, q, zero), jnp.where(lo, zero, q)], axis=0)

    @pl.when(i == 0)
    def _():
        grp = (_iota((LANES, LANES), 0) // 64 == _iota((LANES, LANES), 1) // 64).astype(BF16)
        mx = jnp.zeros((1, LANES), F32)
        for c0 in range(0, t, ck):
            kf = k_ref[0, c0:c0 + ck, :].astype(F32)
            sq_hi, sq_lo = _split2(kf * kf)
            mx = jnp.maximum(mx, jnp.max(_dot(sq_hi, grp) + _dot(sq_lo, grp), axis=0, keepdims=True))
        kn_ref[...] = mx

    q2f = q2.astype(F32)
    qn = jnp.sqrt(jnp.sum(q2f * q2f, axis=1, keepdims=True))
    kn2 = kn_ref[...]
    kn = jnp.sqrt(jnp.where(_iota((2 * tq, 1), 0) < tq, kn2[:, 0:1], kn2[:, 64:65]))
    bound = qn * kn * 1.01 + 1e-6
    safe = jnp.max(bound) <= SAFE_BOUND

    def finish(den, acc):
        lv = lam_ref[...]
        lam = (jnp.exp(jnp.sum(lv[0:1] * lv[1:2], axis=-1, keepdims=True))
               - jnp.exp(jnp.sum(lv[2:3] * lv[3:4], axis=-1, keepdims=True)) + lam_init)
        o = acc[0:tq] / den[0:tq] - lam * (acc[tq:] / den[tq:])
        ms = jnp.mean(o * o, axis=-1, keepdims=True)
        o_ref[0] = (o * lax.rsqrt(ms + EPS) * gsub_ref[...] * (1.0 - lam_init)).astype(BF16)

    def bounded(nkeys):
        step = min(ck, nkeys)
        psum = jnp.zeros((2 * tq, LANES), F32)
        acc = jnp.zeros((2 * tq, LANES), F32)
        for c0 in range(0, nkeys, step):
            p = jnp.exp2(_dot_nt(q2, k_ref[0, c0:c0 + step, :]) - bound)
            for j in range(step // LANES):
                psum = psum + p[:, j * LANES:(j + 1) * LANES]
            acc = acc + _dot(p.astype(BF16), v_ref[0, c0:c0 + step, :])
        finish(jnp.sum(psum, axis=1, keepdims=True), acc)

    def online(nkeys):
        step = min(ck, nkeys)

        def body(j, carry):
            m, den, acc = carry
            st = pl.multiple_of(j * step, step)
            s = _dot_nt(q2, k_ref[0, pl.ds(st, step), :])
            m2 = jnp.maximum(m, jnp.max(s, axis=-1, keepdims=True))
            a = jnp.exp2(m - m2)
            p = jnp.exp2(s - m2)
            den = a * den + jnp.sum(p, axis=-1, keepdims=True)
            acc = a * acc + _dot(p.astype(BF16), v_ref[0, pl.ds(st, step), :])
            return m2, den, acc

        init = (jnp.full((2 * tq, 1), NEG, F32), jnp.zeros((2 * tq, 1), F32), jnp.zeros((2 * tq, LANES), F32))
        _, den, acc = lax.fori_loop(0, nkeys // step, body, init)
        finish(den, acc)

    for is_ctx, nkeys in ((True, l), (False, t)):
        seg = (i < nct) if is_ctx else (i >= nct)
        pl.when(seg & safe)(functools.partial(bounded, nkeys))
        pl.when(seg & jnp.logical_not(safe))(functools.partial(online, nkeys))


def _diff_attention(qk, v, lam_vecs, g_sub, l, lam_init):
    bn, t, _ = qk.shape
    tq = TM
    ck = 768 if t % 768 == 0 else TQ
    kcol = C_HEADS * 2 * C_HD // LANES
    return pl.pallas_call(
        functools.partial(_diff_kernel, nct=l // tq, l=l, t=t, ck=ck, lam_init=lam_init),
        scratch_shapes=[pltpu.VMEM((1, LANES), F32)],
        grid=(bn, C_HEADS, t // tq),
        in_specs=[pl.BlockSpec((4, C_HD), lambda b, h, i: (0, 0)),
                  pl.BlockSpec((1, LANES), lambda b, h, i: (0, 0)),
                  pl.BlockSpec((1, tq, LANES), lambda b, h, i: (b, i, h)),
                  pl.BlockSpec((1, t, LANES), lambda b, h, i: (b, 0, kcol + h)),
                  pl.BlockSpec((1, t, LANES), lambda b, h, i: (b, 0, h))],
        out_specs=pl.BlockSpec((1, tq, LANES), lambda b, h, i: (b, i, h)),
        out_shape=jax.ShapeDtypeStruct((bn, t, C_HEADS * 2 * C_HD), BF16),
        compiler_params=_cp(("arbitrary", "arbitrary", "arbitrary"), VMEM_LIMIT),
        name="diff_attention",
    )(lam_vecs.astype(F32), g_sub.reshape(1, LANES).astype(F32), qk, qk, v)


def _proj_delta_kernel(x_ref, mod_ref, g_ref, w_ref, wab_ref, wabt_ref, alog_ref, dtb_ref, alogt_ref, dtbt_ref,
                       p_ref, z_ref, gc_ref, gct_ref, beta_ref, *, nqkv):
    h = _norm_mod(x_ref[0], g_ref[...], mod_ref[0, 0], 0, 1)
    h_hi, h_lo = _split2(h)
    p = _dot(h_hi, w_ref[...])
    p_ref[0] = p[:, :nqkv].astype(BF16)
    z_ref[0] = p[:, nqkv:].astype(BF16)
    ab = _dot(h_hi, wab_ref[0]) + _dot(h_hi, wab_ref[1]) + _dot(h_lo, wab_ref[0])
    abt = _dot_nt(wabt_ref[0], h_hi) + _dot_nt(wabt_ref[1], h_hi) + _dot_nt(wabt_ref[0], h_lo)
    nd = 2 * A_HEADS
    g = -jnp.exp(alog_ref[...]) * _softplus(ab[:, :nd] + dtb_ref[...])
    beta_ref[0] = _sigmoid(ab[:, nd:])
    gt = -jnp.exp(alogt_ref[...]) * _softplus(abt[:nd] + dtbt_ref[...])
    ri = _iota((TM, TM), 0)
    ci = _iota((TM, TM), 1)
    same = (ri // A_CHUNK) == (ci // A_CHUNK)
    lbd = (same & (ci <= ri)).astype(BF16)
    ubd = (same & (ci >= ri)).astype(BF16)
    g_hi, g_lo = _split2(g)
    pre = _dot(lbd, g_hi) + _dot(lbd, g_lo)
    suf = _dot(ubd, g_hi) + _dot(ubd, g_lo)
    gc_ref[0] = jnp.where(_iota((TM, nd), 1) < A_HEADS, pre, suf)
    gt_hi, gt_lo = _split2(gt)
    pre_t = _dot(gt_hi, ubd) + _dot(gt_lo, ubd)
    suf_t = _dot(gt_hi, lbd) + _dot(gt_lo, lbd)
    gct_ref[0] = jnp.where(_iota((nd, TM), 0) < A_HEADS, pre_t, suf_t)


def _proj_delta(xs, mod, g, w_main, wab, wabt, a_log, dt_bias, nct_m):
    bn, t, d = xs.shape
    nout = w_main.shape[1]
    nqkv = 2 * A_HEADS * A_DK + A_HEADS * A_DV
    nd = 2 * A_HEADS
    row = lambda a: a.reshape(1, nd).astype(F32)
    col = lambda a: a.reshape(nd, 1).astype(F32)
    full = lambda shape: pl.BlockSpec(shape, lambda b, i: (0,) * len(shape))
    return pl.pallas_call(
        functools.partial(_proj_delta_kernel, nqkv=nqkv),
        grid=(bn, t // TM),
        in_specs=[pl.BlockSpec((1, TM, d), lambda b, i: (b, i, 0)),
                  pl.BlockSpec((1, 1, 6, d), lambda b, i: (b, _seg(i, nct_m), 0, 0)),
                  full((1, d)), full((d, nout)), full((2, d, 2 * nd)), full((2, 2 * nd, d)),
                  full((1, nd)), full((1, nd)), full((nd, 1)), full((nd, 1))],
        out_specs=[pl.BlockSpec((1, TM, nqkv), lambda b, i: (b, i, 0)),
                   pl.BlockSpec((1, TM, nout - nqkv), lambda b, i: (b, i, 0)),
                   pl.BlockSpec((1, TM, nd), lambda b, i: (b, i, 0)),
                   pl.BlockSpec((1, nd, TM), lambda b, i: (b, 0, i)),
                   pl.BlockSpec((1, TM, nd), lambda b, i: (b, i, 0))],
        out_shape=[jax.ShapeDtypeStruct((bn, t, nqkv), BF16),
                   jax.ShapeDtypeStruct((bn, t, nout - nqkv), BF16),
                   jax.ShapeDtypeStruct((bn, t, nd), F32),
                   jax.ShapeDtypeStruct((bn, nd, t), F32),
                   jax.ShapeDtypeStruct((bn, t, nd), F32)],
        compiler_params=_cp(("arbitrary", "arbitrary"), VMEM_LIMIT),
        name="proj_delta",
    )(xs, mod, g, w_main, wab, wabt, row(a_log), row(dt_bias), col(a_log), col(dt_bias))


def _conv_kernel(p_ref, w_ref, o_ref, scr, *, l, t, ch, pad):
    c = pl.program_id(1)
    scr[0:pad, :] = jnp.zeros((pad, LANES), F32)
    scr[t + pad:t + 2 * pad, :] = jnp.zeros((pad, LANES), F32)
    scr[pad:t + pad, :] = p_ref[0].astype(F32)
    w = w_ref[...]
    half = w.shape[0] // 2
    for r0 in range(0, t, ch):
        near = (r0 <= l + half) and (r0 + ch >= l - half)
        tt = r0 + _iota((ch, 1), 0)
        acc = None
        for d in range(-half, half + 1):
            xd = scr[pad + r0 + d:pad + r0 + d + ch, :]
            if near and d != 0:
                xd = jnp.where(((tt + d) < l) == (tt < l), xd, 0.0)
            term = xd * w[d + half:d + half + 1]
            acc = term if acc is None else acc + term
        y = _silu(acc)
        nrm = y * lax.rsqrt(jnp.sum(y * y, axis=-1, keepdims=True) + EPS)
        out = jnp.where(c < A_HEADS, nrm * (A_DK ** -0.5), jnp.where(c < 2 * A_HEADS, nrm, y))
        o_ref[0, r0:r0 + ch, :] = out.astype(BF16)


def _delta_conv(p, conv_w, l):
    bn, t, nq = p.shape
    ch = 384 if t % 384 == 0 else TQ
    pad = 8
    kw = conv_w.shape[0]
    return pl.pallas_call(
        functools.partial(_conv_kernel, l=l, t=t, ch=ch, pad=pad),
        grid=(bn, nq // LANES),
        in_specs=[pl.BlockSpec((1, t, LANES), lambda b, c: (b, 0, c)),
                  pl.BlockSpec((kw, LANES), lambda b, c: (0, c))],
        out_specs=pl.BlockSpec((1, t, LANES), lambda b, c: (b, 0, c)),
        out_shape=jax.ShapeDtypeStruct((bn, t, nq), BF16),
        scratch_shapes=[pltpu.VMEM((t + 2 * pad, LANES), F32)],
        compiler_params=_cp(("arbitrary", "arbitrary"), VMEM_LIMIT),
        name="delta_conv",
    )(p, conv_w.astype(F32))


def _merge_masks(ii, jj, lower):
    masks = []
    s = 1
    while s < A_CHUNK:
        grp = (ii // (2 * s)) == (jj // (2 * s))
        odd_i, odd_j = (ii // s) % 2 == 1, (jj // s) % 2 == 1
        masks.append(grp & odd_i & ~odd_j if lower else grp & ~odd_i & odd_j)
        s *= 2
    return masks


def _delta_prep_kernel(q_ref, k_ref, v_ref, gc_ref, gct_ref, beta_ref,
                       uf, ub, wf, wb, qdf, qdb, qkf, qkb, kdtf, kdtb, eg_ref, *, hb):
    hblk = pl.program_id(1)
    ii = _iota((TQ, TQ), 0)
    jj = _iota((TQ, TQ), 1)
    same = (ii // A_CHUNK) == (jj // A_CHUNK)
    eye = (ii == jj).astype(F32)
    merge = (_merge_masks(ii, jj, True), _merge_masks(ii, jj, False))
    tri = ((same & (ii >= jj), same & (ii > jj), (ii // A_CHUNK) * A_CHUNK + (A_CHUNK - 1)),
           (same & (ii <= jj), same & (ii < jj), (ii // A_CHUNK) * A_CHUNK))
    lane_d = _iota((TQ, 2 * A_HEADS), 1)
    gc_all = gc_ref[0]
    beta_all = beta_ref[0]
    outs = ((uf, wf, qdf, qkf, kdtf), (ub, wb, qdb, qkb, kdtb))
    heads = []
    for hh in range(hb):
        sl = slice(hh * LANES, (hh + 1) * LANES)
        qb, kb, vb = q_ref[0, :, sl], k_ref[0, :, sl], v_ref[0, :, sl]
        heads.append((sl, qb.astype(F32), kb.astype(F32), vb.astype(F32), _dot_nt(kb, kb), _dot_nt(qb, kb)))
    probs = []
    for hh, (sl, q, k, v, kk, qk) in enumerate(heads):
        for d in range(2):
            incl, strict, last = tri[d]
            idx = d * A_HEADS + hblk * hb + hh
            gcc = jnp.sum(jnp.where(lane_d == idx, gc_all, 0.0), axis=1, keepdims=True)
            bet = jnp.sum(jnp.where(lane_d == idx, beta_all, 0.0), axis=1, keepdims=True)
            gcr = gct_ref[0, pl.ds(idx, 1), :]
            dm = jnp.exp(jnp.where(incl, gcc - gcr, NEG))
            glast = jnp.sum(jnp.where(jj == last, gcr, 0.0), axis=1, keepdims=True)
            a = jnp.where(strict, bet * kk * dm, 0.0)
            probs.append((hh, d, gcc, bet, dm, glast, a))
    ts = [eye - jnp.where(merge[p[1]][0], p[6], 0.0) for p in probs]
    for lvl in range(1, len(merge[0])):
        tbs = [t.astype(BF16) for t in ts]
        ys = [_dot(tb, jnp.where(merge[p[1]][lvl], p[6], 0.0).astype(BF16)) for tb, p in zip(tbs, probs)]
        xs = [_dot(y.astype(BF16), tb) for y, tb in zip(ys, tbs)]
        ts = [t - x for t, x in zip(ts, xs)]
    sols = []
    for t, (hh, d, gcc, bet, dm, glast, a) in zip(ts, probs):
        _, q, k, v, kk, qk = heads[hh]
        rhs = jnp.concatenate([v * bet, k * (bet * jnp.exp(gcc))], axis=1).astype(BF16)
        sols.append(_dot(t.astype(BF16), rhs))
    for sol, (hh, d, gcc, bet, dm, glast, a) in zip(sols, probs):
        sl, q, k, v, kk, qk = heads[hh]
        u_o, w_o, qd_o, qk_o, kdt_o = outs[d]
        u_o[0, :, sl] = sol[:, :LANES]
        w_o[0, :, sl] = sol[:, LANES:].astype(BF16)
        qd_o[0, :, sl] = (q * jnp.exp(gcc)).astype(BF16)
        qk_o[0, :, sl] = (qk * dm).astype(BF16)
        kdt_o[0, sl, :] = (k * jnp.exp(glast - gcc)).T.astype(BF16)
        eglast = jnp.exp(glast)
        for cch in range(TQ // A_CHUNK):
            row = (d * hb + hh) * (TQ // A_CHUNK) + cch
            eg_ref[0, 0, 0, row:row + 1, :] = jnp.broadcast_to(eglast[cch * A_CHUNK:cch * A_CHUNK + 1], (1, LANES))


def _delta_prep(qkv, gc, gct, beta, hb):
    bn, t, _ = qkv.shape
    nt = t // TQ
    nhb = A_HEADS // hb
    wdt = A_HEADS * A_DV
    tok = lambda off: pl.BlockSpec((1, TQ, hb * LANES), lambda b, h, i: (b, i, off + h))
    nd = 2 * A_HEADS
    tok_shape = lambda dt: jax.ShapeDtypeStruct((bn, t, wdt), dt)
    return pl.pallas_call(
        functools.partial(_delta_prep_kernel, hb=hb),
        grid=(bn, nhb, nt),
        in_specs=[tok(0), tok(nhb), tok(2 * nhb),
                  pl.BlockSpec((1, TQ, nd), lambda b, h, i: (b, i, 0)),
                  pl.BlockSpec((1, nd, TQ), lambda b, h, i: (b, 0, i)),
                  pl.BlockSpec((1, TQ, nd), lambda b, h, i: (b, i, 0))],
        out_specs=[tok(0)] * 8 + [pl.BlockSpec((1, hb * LANES, TQ), lambda b, h, i: (b, h, i))] * 2
                  + [pl.BlockSpec((1, 1, 1, 4 * hb, LANES), lambda b, h, i: (b, h, i, 0, 0))],
        out_shape=[tok_shape(F32), tok_shape(F32)] + [tok_shape(BF16)] * 6
                  + [jax.ShapeDtypeStruct((bn, wdt, t), BF16)] * 2
                  + [jax.ShapeDtypeStruct((bn, nhb, nt, 4 * hb, LANES), F32)],
        compiler_params=_cp(("arbitrary", "arbitrary", "arbitrary"), VMEM_LIMIT),
        name="delta_prep",
    )(qkv, qkv, qkv, gc, gct, beta)


def _delta_scan_kernel(uf, wf, qdf, qkf, kdtf, egf, ub, wb, qdb, qkb, kdtb, egb, of_ref, ob_ref, s_ref, *, hb):
    @pl.when(pl.program_id(1) == 0)
    def _():
        s_ref[...] = jnp.zeros(s_ref.shape, F32)

    zeros = jnp.zeros((A_CHUNK, LANES), BF16)
    dirs = ((uf, wf, qdf, qkf, kdtf, egf, of_ref, (0, 1)), (ub, wb, qdb, qkb, kdtb, egb, ob_ref, (1, 0)))
    chains = [(d, head) for d in range(2) for head in range(A_HEADS)]
    states = [s_ref[d, head] for d, head in chains]
    for step in range(TQ // A_CHUNK):
        ws_all = []
        for (d, head), s in zip(chains, states):
            u, w, qd, qk, kdt, eg, o_ref, order = dirs[d]
            rs = slice(order[step] * A_CHUNK, (order[step] + 1) * A_CHUNK)
            sl = slice(head * LANES, (head + 1) * LANES)
            ws_all.append(_dot(jnp.concatenate([w[0, rs, sl], qd[0, rs, sl]], axis=0), s.astype(BF16)))
        vfulls = []
        for (d, head), ws in zip(chains, ws_all):
            u, w, qd, qk, kdt, eg, o_ref, order = dirs[d]
            cch = order[step]
            rs = slice(cch * A_CHUNK, (cch + 1) * A_CHUNK)
            sl = slice(head * LANES, (head + 1) * LANES)
            vn = (u[0, rs, sl] - ws[:A_CHUNK]).astype(BF16)
            vfull = jnp.concatenate([vn, zeros] if cch == 0 else [zeros, vn], axis=0)
            vfulls.append(vfull)
            o_ref[0, rs, sl] = ws[A_CHUNK:] + _dot(qk[0, rs, sl], vfull)
        new_states = []
        for (d, head), s, vfull in zip(chains, states, vfulls):
            u, w, qd, qk, kdt, eg, o_ref, order = dirs[d]
            hblk, hh = divmod(head, hb)
            row = (d * hb + hh) * 2 + order[step]
            sl = slice(head * LANES, (head + 1) * LANES)
            new_states.append(s * eg[0, hblk, 0, row:row + 1, :] + _dot(kdt[0, sl, :], vfull))
        states = new_states
    for (d, head), s in zip(chains, states):
        s_ref[d, head] = s


def _delta_scan(prep, l, hb):
    uf, ub, wf, wb, qdf, qdb, qkf, qkb, kdtf, kdtb, eg = prep
    bn, t, wdt = uf.shape
    nt = t // TQ
    nct = l // TQ
    nhb = A_HEADS // hb
    fwd = lambda s: s
    bwd = lambda s: jnp.where(s < nct, nct - 1 - s, nt - 1 - (s - nct))
    tok = lambda f: pl.BlockSpec((1, TQ, wdt), lambda b, s: (b, f(s), 0))
    tr = lambda f: pl.BlockSpec((1, wdt, TQ), lambda b, s: (b, 0, f(s)))
    egs = lambda f: pl.BlockSpec((1, nhb, 1, 4 * hb, LANES), lambda b, s: (b, 0, f(s), 0, 0))
    return pl.pallas_call(
        functools.partial(_delta_scan_kernel, hb=hb),
        grid=(bn, nt),
        in_specs=[tok(fwd)] * 4 + [tr(fwd), egs(fwd)] + [tok(bwd)] * 4 + [tr(bwd), egs(bwd)],
        out_specs=[tok(fwd), tok(bwd)],
        out_shape=[jax.ShapeDtypeStruct((bn, t, wdt), F32)] * 2,
        scratch_shapes=[pltpu.VMEM((2, A_HEADS, A_DK, A_DV), F32)],
        compiler_params=_cp(("arbitrary", "arbitrary"), VMEM_LIMIT),
        name="delta_scan",
    )(uf, wf, qdf, qkf, kdtf, eg, ub, wb, qdb, qkb, kdtb, eg)


def _residual_router(y, w_ref, x_ref, mod_ref, gffn_ref, wr_ref, xo_ref, h_ref, aff_ref):
    mod = mod_ref[0, 0]
    xn = x_ref[0] + mod[2:3] * _dot(y, w_ref[...])
    xo_ref[0] = xn
    h = _norm_mod(xn, gffn_ref[...], mod, 3, 4)
    h_hi, h_lo = _split2(h)
    h_ref[0] = h_hi
    lg = _dot_nt(wr_ref[0], h_hi) + _dot_nt(wr_ref[1], h_hi) + _dot_nt(wr_ref[0], h_lo)
    e = jnp.exp(lg - jnp.max(lg, axis=0, keepdims=True))
    aff_ref[0] = e / jnp.sum(e, axis=0, keepdims=True)


def _out_kernel(y_ref, *rest):
    _residual_router(y_ref[0], *rest)


def _out_delta_kernel(of_ref, ob_ref, z_ref, gout_ref, *rest):
    o = of_ref[0] + ob_ref[0]
    parts = []
    for hd in range(A_HEADS):
        sl = slice(hd * A_DV, (hd + 1) * A_DV)
        oh = o[:, sl]
        ms = jnp.mean(oh * oh, axis=-1, keepdims=True)
        parts.append((oh * lax.rsqrt(ms + EPS) * gout_ref[...] * _silu(z_ref[0, :, sl].astype(F32))).astype(BF16))
    _residual_router(jnp.concatenate(parts, axis=1), *rest)


def _out_proj(pre, w_out, xs, mod, g_ffn, wr, nct_m, delta):
    bn, t, d = xs.shape
    k = w_out.shape[0]
    ne = wr.shape[1]
    tok = lambda width: pl.BlockSpec((1, TM, width), lambda b, i: (b, i, 0))
    full = lambda shape: pl.BlockSpec(shape, lambda b, i: (0,) * len(shape))
    if delta:
        of, ob, z, gout = pre
        head_specs = [tok(k), tok(k), tok(k), full((1, A_DV))]
        head_args = (of, ob, z, gout.reshape(1, A_DV).astype(F32))
        body = _out_delta_kernel
    else:
        head_specs = [tok(k)]
        head_args = (pre,)
        body = _out_kernel
    return pl.pallas_call(
        body,
        grid=(bn, t // TM),
        in_specs=head_specs + [full((k, d)), tok(d),
                               pl.BlockSpec((1, 1, 6, d), lambda b, i: (b, _seg(i, nct_m), 0, 0)),
                               full((1, d)), full((2, ne, d))],
        out_specs=[tok(d), tok(d), pl.BlockSpec((1, ne, TM), lambda b, i: (b, 0, i))],
        out_shape=[jax.ShapeDtypeStruct((bn, t, d), F32), jax.ShapeDtypeStruct((bn, t, d), BF16),
                   jax.ShapeDtypeStruct((bn, ne, t), F32)],
        compiler_params=_cp(("arbitrary", "arbitrary"), VMEM_LIMIT),
        name="out_delta" if delta else "out_proj",
    )(*head_args, w_out, xs, mod, g_ffn, wr)


def _kth_largest_bits(bits, k):
    def body(it, thr):
        cand = thr | jnp.left_shift(jnp.int32(1), 30 - it)
        cnt = jnp.sum((bits >= cand).astype(I32), axis=1, keepdims=True)
        return jnp.where(cnt >= k, cand, thr)
    return lax.fori_loop(0, 31, body, jnp.zeros((bits.shape[0], 1), I32))


def _select_kernel(aff_ref, posd_ref, offs_ref, posc_ref, gated_ref, *, l, t, cap_c, cap_x):
    a = aff_ref[0]
    ne = a.shape[0]
    bits = pltpu.bitcast(a, I32)
    upper = (_iota((LANES, LANES), 0) <= _iota((LANES, LANES), 1)).astype(BF16)
    ident = (_iota((LANES, LANES), 0) == _iota((LANES, LANES), 1)).astype(BF16)
    nt = t // LANES
    sel = [None] * nt
    for s0, s1, cap in ((0, l, cap_c), (l, t, cap_x)):
        bseg = bits[:, s0:s1]
        thr = _kth_largest_bits(bseg, cap)
        gtf = jnp.where(bseg > thr, 1.0, 0.0)
        eqf = jnp.where(bseg == thr, 1.0, 0.0)
        need = cap - jnp.sum(gtf, axis=1, keepdims=True)
        run = jnp.zeros((ne, 1), F32)
        for j in range((s1 - s0) // LANES):
            ej = eqf[:, j * LANES:(j + 1) * LANES]
            inc = _dot(ej.astype(BF16), upper)
            keep = jnp.where(inc - ej + run < need, ej, 0.0)
            sel[s0 // LANES + j] = jnp.maximum(gtf[:, j * LANES:(j + 1) * LANES], keep)
            run = run + inc[:, LANES - 1:LANES]
    run = jnp.zeros((ne, 1), F32)
    offs = jnp.zeros((ne, LANES), I32)
    lane = _iota((ne, LANES), 1)
    for j in range(nt):
        sj = sel[j]
        inc = _dot(sj.astype(BF16), upper)
        pos = jnp.where(sj > 0.0, inc - sj + run, -1.0)
        posd_ref[0, j] = pos.astype(I32)
        offs = jnp.where(lane == j, run.astype(I32), offs)
        p_hi, p_lo = _split2(pos)
        posc_ref[0, j * LANES:(j + 1) * LANES, :] = (_dot_nt(ident, p_hi) + _dot_nt(ident, p_lo)).astype(I32)
        gated_ref[0, j] = jnp.where(sj > 0.0, a[:, j * LANES:(j + 1) * LANES], 0.0)
        run = run + inc[:, LANES - 1:LANES]
    offs_ref[0] = jnp.where(lane == nt, run.astype(I32), offs)


def _select(aff, l, cap_c, cap_x):
    bn, ne, t = aff.shape
    nt = t // LANES
    return pl.pallas_call(
        functools.partial(_select_kernel, l=l, t=t, cap_c=cap_c, cap_x=cap_x),
        grid=(bn,),
        in_specs=[pl.BlockSpec((1, ne, t), lambda b: (b, 0, 0))],
        out_specs=[pl.BlockSpec((1, nt, ne, LANES), lambda b: (b, 0, 0, 0)),
                   pl.BlockSpec((1, ne, LANES), lambda b: (b, 0, 0)),
                   pl.BlockSpec((1, t, ne), lambda b: (b, 0, 0)),
                   pl.BlockSpec((1, nt, ne, LANES), lambda b: (b, 0, 0, 0))],
        out_shape=[jax.ShapeDtypeStruct((bn, nt, ne, LANES), I32), jax.ShapeDtypeStruct((bn, ne, LANES), I32),
                   jax.ShapeDtypeStruct((bn, t, ne), I32), jax.ShapeDtypeStruct((bn, nt, ne, LANES), F32)],
        compiler_params=_cp(("arbitrary",), VMEM_LIMIT),
        name="route_select",
    )(aff)


GATHER_TOK = 2 * TQ
GATHER_WIN = 64
EXPERT_GROUP = 4
COMBINE_WIN = TQ + 16
PACK_WIN = 48
EXPERT_VMEM_LIMIT = 60 * 1024 * 1024


def _gather_group(offs_ref, h_ref, pos_ref, gate_ref, xg_ref, gr_ref, b, e0, ne, nt):
    xg_ref[...] = jnp.zeros(xg_ref.shape, BF16)
    gr_ref[...] = jnp.zeros(gr_ref.shape, F32)
    tpt = GATHER_TOK // LANES
    riota = _iota((GATHER_WIN, GATHER_TOK), 0)
    stacked = (EXPERT_GROUP * GATHER_WIN, ne)
    pick = _iota(stacked, 1) == e0 + _iota(stacked, 0) // GATHER_WIN

    def bounds(j, k):
        base = (b * ne + e0 + k) * LANES
        aoff = (offs_ref[base + j * tpt] // 16) * 16
        return aoff, (offs_ref[base + (j + 1) * tpt] - aoff + GATHER_WIN - 1) // GATHER_WIN

    def add_windows(j, w):
        tok = pl.ds(pl.multiple_of(j * GATHER_TOK, GATHER_TOK), GATHER_TOK)
        starts, hots = [], []
        for k in range(EXPERT_GROUP):
            aoff, nwin = bounds(j, k)
            live = w < jnp.maximum(nwin, 1)
            pos = jnp.concatenate([pos_ref[0, j * tpt + c, pl.ds(e0 + k, 1), :] for c in range(tpt)], axis=1)
            first = jnp.where(live, aoff + w * GATHER_WIN, -(1 << 20))
            hots.append(jnp.where(riota == pos - first, 1.0, 0.0).astype(BF16))
            starts.append(pl.multiple_of(aoff + jnp.where(live, w, 0) * GATHER_WIN, 16))
        lhs = jnp.concatenate(hots, axis=0)
        rows = _dot(lhs, h_ref[0, tok, :])
        gd = jnp.concatenate([gate_ref[0, j * tpt + c] for c in range(tpt)], axis=1)
        g1 = gd.astype(BF16)
        r1 = gd - g1.astype(F32)
        g2 = r1.astype(BF16)
        g3 = (r1 - g2.astype(F32)).astype(BF16)
        gall = _dot_nt(lhs, g1) + _dot_nt(lhs, g2) + _dot_nt(lhs, g3)
        gcol = jnp.sum(jnp.where(pick, gall, 0.0), axis=1, keepdims=True)
        for k in range(EXPERT_GROUP):
            rs = slice(k * GATHER_WIN, (k + 1) * GATHER_WIN)
            dst = pl.ds(starts[k], GATHER_WIN)
            xg_ref[k, dst, :] = (xg_ref[k, dst, :].astype(F32) + rows[rs]).astype(BF16)
            gr_ref[k, dst, :] += jnp.broadcast_to(gcol[rs], (GATHER_WIN, LANES))

    def tile(j, carry):
        add_windows(j, 0)
        most = bounds(j, 0)[1]
        for k in range(1, EXPERT_GROUP):
            most = jnp.maximum(most, bounds(j, k)[1])

        def extra(w, carry2):
            add_windows(j, w)
            return carry2

        lax.fori_loop(1, most, extra, 0)
        return carry

    lax.fori_loop(0, nt // tpt, tile, 0)


def _expert_kernel(offs_ref, h_ref, pos_ref, gate_ref, wgu_ref, wd_ref, y_ref, xg_ref, gr_ref, acc_ref, *, nt, r, fc):
    b = pl.program_id(0)
    e = pl.program_id(1)
    slot = e % EXPERT_GROUP

    @pl.when(slot == 0)
    def _():
        _gather_group(offs_ref, h_ref, pos_ref, gate_ref, xg_ref, gr_ref, b, e, pl.num_programs(1), nt)

    xg = xg_ref[slot, pl.ds(0, r), :]
    f = wd_ref.shape[1]
    for c in range(f // fc):
        g = _dot(xg, wgu_ref[0, :, c * fc:(c + 1) * fc])
        u = _dot(xg, wgu_ref[0, :, f + c * fc:f + (c + 1) * fc])
        part = _dot((_silu(g) * u).astype(BF16), wd_ref[0, c * fc:(c + 1) * fc, :])
        if c == 0:
            acc_ref[...] = part
        else:
            acc_ref[...] += part
    y_ref[0, 0] = (acc_ref[...] * gr_ref[slot, pl.ds(0, r), :][:, 0:1]).astype(BF16)


def _experts(offs, hf, posd, gated, wgu, wd, r):
    bn, t, d = hf.shape
    ne, _, f2 = wgu.shape
    nt = t // TQ
    rows = r + GATHER_WIN
    assert ne % EXPERT_GROUP == 0 and rows % 16 == 0
    return pl.pallas_call(
        functools.partial(_expert_kernel, nt=nt, r=r, fc=min(256, f2 // 2)),
        grid_spec=pltpu.PrefetchScalarGridSpec(
            num_scalar_prefetch=1,
            grid=(bn, ne),
            in_specs=[pl.BlockSpec((1, t, d), lambda b, e, o: (b, 0, 0), pipeline_mode=pl.Buffered(1)),
                      pl.BlockSpec((1, nt, ne, LANES), lambda b, e, o: (b, 0, 0, 0)),
                      pl.BlockSpec((1, nt, ne, LANES), lambda b, e, o: (b, 0, 0, 0)),
                      pl.BlockSpec((1, d, f2), lambda b, e, o: (e, 0, 0)),
                      pl.BlockSpec((1, f2 // 2, d), lambda b, e, o: (e, 0, 0))],
            out_specs=pl.BlockSpec((1, 1, r, d), lambda b, e, o: (b, e, 0, 0)),
            scratch_shapes=[pltpu.VMEM((EXPERT_GROUP, rows, d), BF16), pltpu.VMEM((EXPERT_GROUP, rows, LANES), F32),
                            pltpu.VMEM((r, d), F32)]),
        out_shape=jax.ShapeDtypeStruct((bn, ne, r, d), BF16),
        compiler_params=_cp(("arbitrary", "arbitrary"), EXPERT_VMEM_LIMIT),
        name="experts",
    )(offs, hf, posd, gated, wgu, wd)


def _combine_kernel(offs_ref, y_ref, posc_ref, x_ref, mod_ref, o_ref, *, r, win, pack):
    b = pl.program_id(0)
    j = pl.program_id(1)
    ne = y_ref.shape[1]
    pc = posc_ref[0]
    offs = [offs_ref[(b * ne + e) * LANES + j] for e in range(ne)]
    ends = [offs_ref[(b * ne + e) * LANES + j + 1] for e in range(ne)]
    los = [jnp.minimum((offs[e] // 16) * 16, r - pack) for e in range(ne)]
    fits = ends[0] - los[0] <= pack
    for e in range(1, ne):
        fits = fits & (ends[e] - los[e] <= pack)
    gate_mod = mod_ref[0, 0][5:6]

    @pl.when(fits)
    def _():
        lane_e = _iota((1, ne), 1)
        lo_row = jnp.zeros((1, ne), I32)
        for e in range(ne):
            lo_row = jnp.where(lane_e == e, los[e], lo_row)
        rel = pc - lo_row
        rel = jnp.where((pc >= 0) & (rel >= 0) & (rel < pack), rel, -1).astype(F32).astype(BF16)
        spread = (_iota((ne, ne * pack), 0) == _iota((ne, ne * pack), 1) // pack).astype(BF16)
        want = (_iota((TQ, ne * pack), 1) % pack).astype(F32)
        onehot = jnp.where(_dot(rel, spread) == want, 1.0, 0.0).astype(BF16)
        ycat = jnp.concatenate([y_ref[0, e, pl.ds(pl.multiple_of(los[e], 16), pack), :] for e in range(ne)], axis=0)
        o_ref[0] = x_ref[0] + gate_mod * _dot(onehot, ycat)

    @pl.when(jnp.logical_not(fits))
    def _():
        liota = _iota((TQ, win), 1)
        acc = jnp.zeros(x_ref.shape[1:], F32)
        for e in range(ne):
            aoff = pl.multiple_of(jnp.minimum((offs[e] // 16) * 16, r - win), 16)
            onehot = jnp.where(liota == pc[:, e:e + 1] - aoff, 1.0, 0.0).astype(BF16)
            acc = acc + _dot(onehot, y_ref[0, e, pl.ds(aoff, win), :])
        o_ref[0] = x_ref[0] + gate_mod * acc


def _combine(offs, y, posc, xs, mod, nct):
    bn, t, d = xs.shape
    ne, r = y.shape[1], y.shape[2]
    win = min(COMBINE_WIN, r)
    pack = min(PACK_WIN, r)
    assert (r - win) % 16 == 0 and (r - pack) % 16 == 0
    tok = lambda width: pl.BlockSpec((1, TQ, width), lambda b, j, o: (b, j, 0))
    return pl.pallas_call(
        functools.partial(_combine_kernel, r=r, win=win, pack=pack),
        grid_spec=pltpu.PrefetchScalarGridSpec(
            num_scalar_prefetch=1,
            grid=(bn, t // TQ),
            in_specs=[pl.BlockSpec((1, ne, r, d), lambda b, j, o: (b, 0, 0, 0), pipeline_mode=pl.Buffered(1)),
                      tok(ne), tok(d),
                      pl.BlockSpec((1, 1, 6, d), lambda b, j, o: (b, _seg(j, nct), 0, 0))],
            out_specs=tok(d)),
        out_shape=jax.ShapeDtypeStruct((bn, t, d), F32),
        compiler_params=_cp(("arbitrary", "arbitrary"), VMEM_LIMIT),
        name="moe_combine",
    )(offs, y, posc, xs, mod)


def _moe(xs, hf, aff, mod, wgu, wd, l):
    bn, t, _ = xs.shape
    cap_c = EC_CAPACITY * l // N_EXPERTS
    cap_x = EC_CAPACITY * (t - l) // N_EXPERTS
    posd, offs, posc, gated = _select(aff, l, cap_c, cap_x)
    offs = offs.reshape(-1)
    y = _experts(offs, hf, posd, gated, wgu, wd, cap_c + cap_x)
    return _combine(offs, y, posc, xs, mod, l // TQ)


def _hi_lo(w):
    hi = w.astype(BF16)
    return jnp.stack([hi, (w - hi.astype(F32)).astype(BF16)])


def _mixer_delta(xs, mod, g_mix, w_in, conv_w, a_log, dt_bias, l, hb=4):
    nqkvz = 2 * A_HEADS * A_DK + 2 * A_HEADS * A_DV
    wab = w_in[:, nqkvz:]
    p, z, gc, gct, beta = _proj_delta(xs, mod, g_mix, w_in[:, :nqkvz].astype(BF16), _hi_lo(wab), _hi_lo(wab.T),
                                      a_log, dt_bias, l // TM)
    qkv = _delta_conv(p, conv_w, l)
    of, ob = _delta_scan(_delta_prep(qkv, gc, gct, beta, hb), l, hb)
    return of, ob, z


def _mixer_swa(xs, mod, g_mix, w_in, qn, kn, sink, cosf, sinf, l):
    nq = B_QHEADS * B_HD
    nk = B_KVHEADS * B_HD
    dup = lambda w: jnp.concatenate([w.reshape(-1, B_KVHEADS, 1, B_HD)] * 2, axis=2).reshape(-1, 2 * nk)
    w = jnp.concatenate([w_in[:, :nq], dup(w_in[:, nq:nq + nk]), dup(w_in[:, nq + nk:])], axis=1).astype(BF16)
    gain = jnp.concatenate([jnp.tile(qn, B_QHEADS) * (B_HD ** -0.5), jnp.tile(kn, 2 * B_KVHEADS)])[None].astype(F32)
    qk, v2 = _proj_qk(xs, mod, g_mix, w, gain, cosf, sinf, nq + 2 * nk, l // TM)
    return _swa_attention(qk, v2, sink, l)


def _mixer_diff(xs, mod, g_mix, w_in, qn, kn, lam_vecs, g_sub, cosf, sinf, l, lam_init):
    nqk = C_HEADS * 2 * C_HD
    gain = jnp.concatenate([jnp.tile(qn, 2 * C_HEADS) * (C_HD ** -0.5 * LOG2E),
                            jnp.tile(kn, 2 * C_HEADS)])[None].astype(F32)
    qk, v = _proj_qk(xs, mod, g_mix, w_in.astype(BF16), gain, cosf, sinf, 2 * nqk, l // TM)
    return _diff_attention(qk, v, lam_vecs, g_sub, l, lam_init)


def kernel(x, c, ctx, c_ctx, w_ada, b_ada, g_mix, g_ffn, a_w_in, a_conv, a_log, a_dt_bias, a_g_out, a_w_out,
           b_w_in, b_q_norm, b_k_norm, b_sink, b_w_out, c_w_in, c_q_norm, c_k_norm, c_lambda, c_g_sub, c_w_out,
           w_router, w_gate_up, w_down):
    depth = w_ada.shape[0]
    n = x.shape[1]
    l = ctx.shape[1]
    assert l % TM == 0 and n % TM == 0
    xs = jnp.concatenate([ctx, x], axis=1)
    mods = _adaln(c, c_ctx, w_ada, b_ada)
    cos_b, sin_b = _rope_tables(n, l, B_HD)
    cos_c, sin_c = _rope_tables(n, l, C_HD)
    nct_m = l // TM
    for layer in range(depth):
        kind, j = layer % N_MIXERS, layer // N_MIXERS
        mod = mods[layer]
        gm = g_mix[layer][None].astype(F32)
        gf = g_ffn[layer][None].astype(F32)
        wr = _hi_lo(w_router[layer].T)
        if kind == 0:
            pre = _mixer_delta(xs, mod, gm, a_w_in[j], a_conv[j], a_log[j], a_dt_bias[j], l)
            xs, hf, aff = _out_proj(pre + (a_g_out[j],), a_w_out[j].astype(BF16), xs, mod, gf, wr, nct_m, True)
        elif kind == 1:
            pre = _mixer_swa(xs, mod, gm, b_w_in[j], b_q_norm[j], b_k_norm[j], b_sink[j], cos_b, sin_b, l)
            xs, hf, aff = _out_proj(pre, b_w_out[j].astype(BF16), xs, mod, gf, wr, nct_m, False)
        else:
            lam_init = 0.8 - 0.6 * math.exp(-0.3 * layer)
            pre = _mixer_diff(xs, mod, gm, c_w_in[j], c_q_norm[j], c_k_norm[j], c_lambda[j], c_g_sub[j],
                              cos_c, sin_c, l, lam_init)
            xs, hf, aff = _out_proj(pre, c_w_out[j].astype(BF16), xs, mod, gf, wr, nct_m, False)
        xs = _moe(xs, hf, aff, mod, w_gate_up[layer].astype(BF16), w_down[layer].astype(BF16), l)
    return xs[:, l:]
```

```python
import functools
import math

import jax
import jax.numpy as jnp
from jax import lax
from jax.experimental import pallas as pl
from jax.experimental.pallas import tpu as pltpu

F32 = jnp.float32
BF16 = jnp.bfloat16
I32 = jnp.int32

EPS = 1e-6
ROPE_BASE = 10000.0
GRID_W = 64
N_MIXERS = 3

A_HEADS, A_DK, A_DV, A_CHUNK = 8, 128, 128, 64
B_QHEADS, B_KVHEADS, B_HD = 16, 4, 64
B_GROUP = B_QHEADS // B_KVHEADS
C_HEADS, C_HD = 8, 64
N_EXPERTS, EC_CAPACITY = 16, 2

LANES = 128
TM = 256
TQ = 128
NEG = -1e30
VMEM_LIMIT = 56 * 1024 * 1024


def _cp(sem, vmem=None):
    return pltpu.CompilerParams(dimension_semantics=sem, vmem_limit_bytes=vmem)


def _dot(a, b):
    return jnp.dot(a, b, preferred_element_type=F32)


def _dot_nt(a, b):
    return lax.dot_general(a, b, (((1,), (1,)), ((), ())), preferred_element_type=F32)


def _split2(a):
    hi = a.astype(BF16)
    return hi, (a - hi.astype(F32)).astype(BF16)


def _sigmoid(x):
    return 1.0 / (1.0 + jnp.exp(-x))


def _silu(x):
    return x * _sigmoid(x)


def _softplus(x):
    return jnp.maximum(x, 0.0) + jnp.log(1.0 + jnp.exp(-jnp.abs(x)))


def _norm_mod(x, g, mod, i_shift, i_scale):
    ms = jnp.mean(x * x, axis=-1, keepdims=True)
    y = x * lax.rsqrt(ms + EPS) * g
    return y * (1.0 + mod[i_scale:i_scale + 1]) + mod[i_shift:i_shift + 1]


def _iota(shape, dim):
    return lax.broadcasted_iota(I32, shape, dim)


def _ada_kernel(s_ref, w_ref, b_ref, o_ref):
    s = _silu(s_ref[...])
    s_hi, s_lo = _split2(s)
    w_hi, w_lo = _split2(w_ref[0])
    o_ref[0] = _dot(s_hi, w_hi) + _dot(s_hi, w_lo) + _dot(s_lo, w_hi) + b_ref[0]


def _adaln(c, c_ctx, w_ada, b_ada):
    depth, d, d6 = w_ada.shape
    bn = c.shape[0]
    rows_n = -(-(bn + 1) // 8) * 8
    rows = jnp.zeros((rows_n, d), F32).at[:bn].set(c).at[bn].set(c_ctx)
    nb = d6 // 4
    out = pl.pallas_call(
        _ada_kernel,
        grid=(depth, d6 // nb),
        in_specs=[pl.BlockSpec((rows_n, d), lambda l, j: (0, 0)),
                  pl.BlockSpec((1, d, nb), lambda l, j: (l, 0, j)),
                  pl.BlockSpec((1, 1, nb), lambda l, j: (l, 0, j))],
        out_specs=pl.BlockSpec((1, rows_n, nb), lambda l, j: (l, 0, j)),
        out_shape=jax.ShapeDtypeStruct((depth, rows_n, d6), F32),
        compiler_params=_cp(("arbitrary", "arbitrary"), VMEM_LIMIT),
        name="adaln",
    )(rows, w_ada, b_ada.reshape(depth, 1, d6))
    mx = out[:, :bn].reshape(depth, bn, 6, d)
    mc = jnp.broadcast_to(out[:, bn].reshape(depth, 1, 6, d), (depth, bn, 6, d))
    return jnp.stack([mc, mx], axis=2)


def _seg(i, nct):
    return jnp.where(i < nct, 0, 1)


COL_CHUNK = 1024


def _batch_group(bn):
    return 4 if bn % 4 == 0 else (2 if bn % 2 == 0 else 1)


def _rows(a, k):
    return a[k * TM:(k + 1) * TM]


def _proj_qk_kernel(x_ref, mod_ref, g_ref, w_ref, gain_ref, cos_ref, sin_ref, qk_ref, v_ref, *, nqk):
    nb = x_ref.shape[0]
    h = jnp.concatenate([_norm_mod(x_ref[k], g_ref[...], mod_ref[k, 0], 0, 1).astype(BF16) for k in range(nb)], axis=0)
    grp = (_iota((LANES, LANES), 0) // 64 == _iota((LANES, LANES), 1) // 64).astype(BF16)
    first = (_iota((1, LANES), 1) % 64) < 32
    cs = jnp.concatenate([cos_ref[...]] * nb, axis=0)
    sn = jnp.concatenate([sin_ref[...]] * nb, axis=0)
    nout = w_ref.shape[1]
    for c0 in range(0, nout, COL_CHUNK):
        p = _dot(h, w_ref[:, c0:min(c0 + COL_CHUNK, nout)])
        for t0 in range(c0, min(c0 + COL_CHUNK, nout), LANES):
            xt = p[:, t0 - c0:t0 - c0 + LANES]
            if t0 < nqk:
                sq_hi, sq_lo = _split2(xt * xt)
                ms = (_dot(sq_hi, grp) + _dot(sq_lo, grp)) * (1.0 / 64)
                y = xt * lax.rsqrt(ms + EPS) * gain_ref[:, t0:t0 + LANES]
                rot = jnp.where(first, pltpu.roll(y, 96, 1), pltpu.roll(y, 32, 1))
                out = (y * cs + rot * sn).astype(BF16)
                for k in range(nb):
                    qk_ref[k, :, t0:t0 + LANES] = _rows(out, k)
            else:
                for k in range(nb):
                    v_ref[k, :, t0 - nqk:t0 - nqk + LANES] = _rows(xt, k).astype(BF16)


def _proj_qk(xs, mod, g, w, gain, cosf, sinf, nqk, nct_m):
    bn, t, d = xs.shape
    nout = w.shape[1]
    nb = _batch_group(bn)
    return pl.pallas_call(
        functools.partial(_proj_qk_kernel, nqk=nqk),
        grid=(bn // nb, t // TM),
        in_specs=[pl.BlockSpec((nb, TM, d), lambda b, i: (b, i, 0)),
                  pl.BlockSpec((nb, 1, 6, d), lambda b, i: (b, _seg(i, nct_m), 0, 0)),
                  pl.BlockSpec((1, d), lambda b, i: (0, 0)),
                  pl.BlockSpec((d, nout), lambda b, i: (0, 0)),
                  pl.BlockSpec((1, nqk), lambda b, i: (0, 0)),
                  pl.BlockSpec((TM, LANES), lambda b, i: (i, 0)),
                  pl.BlockSpec((TM, LANES), lambda b, i: (i, 0))],
        out_specs=[pl.BlockSpec((nb, TM, nqk), lambda b, i: (b, i, 0)),
                   pl.BlockSpec((nb, TM, nout - nqk), lambda b, i: (b, i, 0))],
        out_shape=[jax.ShapeDtypeStruct((bn, t, nqk), BF16),
                   jax.ShapeDtypeStruct((bn, t, nout - nqk), BF16)],
        compiler_params=_cp(("arbitrary", "arbitrary"), VMEM_LIMIT),
        name="proj_qk",
    )(xs, mod, g, w, gain, cosf, sinf)


def _rope_tables(n, l, head_dim):
    t = jnp.arange(n)
    n_freq = head_dim // 4
    inv = ROPE_BASE ** (-jnp.arange(n_freq, dtype=F32) / n_freq)
    ang = jnp.concatenate([(t // GRID_W).astype(F32)[:, None] * inv, (t % GRID_W).astype(F32)[:, None] * inv], -1)
    cs, sn = jnp.cos(ang), jnp.sin(ang)
    reps = LANES // head_dim
    cosf = jnp.tile(jnp.concatenate([cs, cs], -1), (1, reps))
    sinf = jnp.tile(jnp.concatenate([-sn, sn], -1), (1, reps))
    cosf = jnp.concatenate([jnp.ones((l, LANES), F32), cosf], 0)
    sinf = jnp.concatenate([jnp.zeros((l, LANES), F32), sinf], 0)
    return cosf, sinf


def _swa_kernel(q_ref, kp_ref, kc_ref, kn_ref, kx_ref, vp_ref, vc_ref, vn_ref, vx_ref, sink_ref, bias_ref, o_ref,
                *, nct, ntx, l):
    xi = pl.program_id(1) - nct
    lo = _iota((TQ, LANES), 1) < 64
    zero = jnp.zeros((TQ, LANES), BF16)
    c = _iota((1, 3 * TQ + l), 1)
    dead = (((c < TQ) & (xi < 1)) | ((c >= 2 * TQ) & (c < 3 * TQ) & (xi + 1 >= ntx)) | ((c < 3 * TQ) & (xi < 0)))
    bias = bias_ref[...] + jnp.where(dead, NEG, 0.0)
    bias4 = jnp.concatenate([bias] * B_GROUP, axis=0)
    gw = B_GROUP * B_HD
    scores = []
    for kv in range(B_KVHEADS):
        ks = slice(kv * LANES, (kv + 1) * LANES)
        qa, qb = q_ref[0, :, kv * gw:kv * gw + LANES], q_ref[0, :, kv * gw + LANES:(kv + 1) * gw]
        q4 = jnp.concatenate([jnp.where(lo, qa, zero), jnp.where(lo, zero, qa),
                              jnp.where(lo, qb, zero), jnp.where(lo, zero, qb)], axis=0)
        kcat = jnp.concatenate([kp_ref[0, :, ks], kc_ref[0, :, ks], kn_ref[0, :, ks], kx_ref[0, :, ks]], axis=0)
        scores.append(_dot_nt(q4, kcat) + bias4)
    probs = []
    for kv, s in enumerate(scores):
        sk = sink_ref[kv]
        m = jnp.maximum(jnp.max(s, axis=-1, keepdims=True), sk)
        p = jnp.exp(s - m)
        probs.append((p.astype(BF16), jnp.sum(p, axis=-1, keepdims=True) + jnp.exp(sk - m)))
    for kv, (p, den) in enumerate(probs):
        ks = slice(kv * LANES, (kv + 1) * LANES)
        vcat = jnp.concatenate([vp_ref[0, :, ks], vc_ref[0, :, ks], vn_ref[0, :, ks], vx_ref[0, :, ks]], axis=0)
        o4 = _dot(p, vcat) / den
        oa = jnp.where(lo, o4[0:TQ], o4[TQ:2 * TQ])
        ob = jnp.where(lo, o4[2 * TQ:3 * TQ], o4[3 * TQ:4 * TQ])
        o_ref[0, :, kv * gw:(kv + 1) * gw] = jnp.concatenate([oa, ob], axis=1).astype(BF16)


def _swa_attention(qk, v2, sink, l):
    bn, t, _ = qk.shape
    nt = t // TQ
    nct = l // TQ
    nq = B_QHEADS * B_HD
    kw = B_KVHEADS * LANES
    prev = lambda i: jnp.maximum(i - 1, 0)
    nxt = lambda i: jnp.minimum(i + 1, nt - 1)
    cur = lambda i: i
    kspec = lambda f: pl.BlockSpec((1, TQ, kw), lambda b, i: (b, f(i), nq // kw))
    vspec = lambda f: pl.BlockSpec((1, TQ, kw), lambda b, i: (b, f(i), 0))
    sinkcol = jnp.repeat(sink.reshape(B_KVHEADS, B_GROUP), TQ, axis=1).reshape(B_KVHEADS, B_GROUP * TQ, 1).astype(F32)
    r = jnp.arange(TQ)[:, None]
    c = jnp.arange(3 * TQ + l)[None, :]
    band = ((c < TQ) & (c >= r)) | ((c >= TQ) & (c < 2 * TQ)) | ((c >= 2 * TQ) & (c - 2 * TQ <= r)) | (c >= 3 * TQ)
    bias = jnp.where(band, 0.0, NEG).astype(F32)
    return pl.pallas_call(
        functools.partial(_swa_kernel, nct=nct, ntx=nt - nct, l=l),
        grid=(bn, nt),
        in_specs=[pl.BlockSpec((1, TQ, nq), lambda b, i: (b, i, 0)),
                  kspec(prev), kspec(cur), kspec(nxt),
                  pl.BlockSpec((1, l, kw), lambda b, i: (b, 0, nq // kw)),
                  vspec(prev), vspec(cur), vspec(nxt),
                  pl.BlockSpec((1, l, kw), lambda b, i: (b, 0, 0)),
                  pl.BlockSpec((B_KVHEADS, B_GROUP * TQ, 1), lambda b, i: (0, 0, 0)),
                  pl.BlockSpec((TQ, 3 * TQ + l), lambda b, i: (0, 0))],
        out_specs=pl.BlockSpec((1, TQ, nq), lambda b, i: (b, i, 0)),
        out_shape=jax.ShapeDtypeStruct((bn, t, nq), BF16),
        compiler_params=_cp(("arbitrary", "arbitrary"), VMEM_LIMIT),
        name="swa_attention",
    )(qk, qk, qk, qk, qk, v2, v2, v2, v2, sinkcol, bias)


LOG2E = 1.4426950408889634
SAFE_BOUND = 60.0


def _diff_kernel(lam_ref, gsub_ref, q_ref, k_ref, v_ref, o_ref, kn_ref, *, nct, l, t, ck, lam_init):
    i = pl.program_id(2)
    q = q_ref[0]
    tq = q.shape[0]
    lo = _iota((tq, LANES), 1) < 64
    zero = jnp.zeros((tq, LANES), BF16)
    q2 = jnp.concatenate([jnp.where(lo, q, zero), jnp.where(lo, zero, q)], axis=0)

    @pl.when(i == 0)
    def _():
        grp = (_iota((LANES, LANES), 0) // 64 == _iota((LANES, LANES), 1) // 64).astype(BF16)
        mx = jnp.zeros((1, LANES), F32)
        for c0 in range(0, t, ck):
            kf = k_ref[0, c0:c0 + ck, :].astype(F32)
            sq_hi, sq_lo = _split2(kf * kf)
            mx = jnp.maximum(mx, jnp.max(_dot(sq_hi, grp) + _dot(sq_lo, grp), axis=0, keepdims=True))
        kn_ref[...] = mx

    q2f = q2.astype(F32)
    qn = jnp.sqrt(jnp.sum(q2f * q2f, axis=1, keepdims=True))
    kn2 = kn_ref[...]
    kn = jnp.sqrt(jnp.where(_iota((2 * tq, 1), 0) < tq, kn2[:, 0:1], kn2[:, 64:65]))
    bound = qn * kn * 1.01 + 1e-6
    safe = jnp.max(bound) <= SAFE_BOUND

    def finish(den, acc):
        lv = lam_ref[...]
        lam = (jnp.exp(jnp.sum(lv[0:1] * lv[1:2], axis=-1, keepdims=True))
               - jnp.exp(jnp.sum(lv[2:3] * lv[3:4], axis=-1, keepdims=True)) + lam_init)
        o = acc[0:tq] / den[0:tq] - lam * (acc[tq:] / den[tq:])
        ms = jnp.mean(o * o, axis=-1, keepdims=True)
        o_ref[0] = (o * lax.rsqrt(ms + EPS) * gsub_ref[...] * (1.0 - lam_init)).astype(BF16)

    def bounded(nkeys):
        step = min(ck, nkeys)
        psum = jnp.zeros((2 * tq, LANES), F32)
        acc = jnp.zeros((2 * tq, LANES), F32)
        for c0 in range(0, nkeys, step):
            p = jnp.exp2(_dot_nt(q2, k_ref[0, c0:c0 + step, :]) - bound)
            for j in range(step // LANES):
                psum = psum + p[:, j * LANES:(j + 1) * LANES]
            acc = acc + _dot(p.astype(BF16), v_ref[0, c0:c0 + step, :])
        finish(jnp.sum(psum, axis=1, keepdims=True), acc)

    def online(nkeys):
        step = min(ck, nkeys)

        def body(j, carry):
            m, den, acc = carry
            st = pl.multiple_of(j * step, step)
            s = _dot_nt(q2, k_ref[0, pl.ds(st, step), :])
            m2 = jnp.maximum(m, jnp.max(s, axis=-1, keepdims=True))
            a = jnp.exp2(m - m2)
            p = jnp.exp2(s - m2)
            den = a * den + jnp.sum(p, axis=-1, keepdims=True)
            acc = a * acc + _dot(p.astype(BF16), v_ref[0, pl.ds(st, step), :])
            return m2, den, acc

        init = (jnp.full((2 * tq, 1), NEG, F32), jnp.zeros((2 * tq, 1), F32), jnp.zeros((2 * tq, LANES), F32))
        _, den, acc = lax.fori_loop(0, nkeys // step, body, init)
        finish(den, acc)

    for is_ctx, nkeys in ((True, l), (False, t)):
        seg = (i < nct) if is_ctx else (i >= nct)
        pl.when(seg & safe)(functools.partial(bounded, nkeys))
        pl.when(seg & jnp.logical_not(safe))(functools.partial(online, nkeys))


def _diff_attention(qk, v, lam_vecs, g_sub, l, lam_init):
    bn, t, _ = qk.shape
    tq = TM
    ck = 768 if t % 768 == 0 else TQ
    kcol = C_HEADS * 2 * C_HD // LANES
    return pl.pallas_call(
        functools.partial(_diff_kernel, nct=l // tq, l=l, t=t, ck=ck, lam_init=lam_init),
        scratch_shapes=[pltpu.VMEM((1, LANES), F32)],
        grid=(bn, C_HEADS, t // tq),
        in_specs=[pl.BlockSpec((4, C_HD), lambda b, h, i: (0, 0)),
                  pl.BlockSpec((1, LANES), lambda b, h, i: (0, 0)),
                  pl.BlockSpec((1, tq, LANES), lambda b, h, i: (b, i, h)),
                  pl.BlockSpec((1, t, LANES), lambda b, h, i: (b, 0, kcol + h)),
                  pl.BlockSpec((1, t, LANES), lambda b, h, i: (b, 0, h))],
        out_specs=pl.BlockSpec((1, tq, LANES), lambda b, h, i: (b, i, h)),
        out_shape=jax.ShapeDtypeStruct((bn, t, C_HEADS * 2 * C_HD), BF16),
        compiler_params=_cp(("arbitrary", "arbitrary", "arbitrary"), VMEM_LIMIT),
        name="diff_attention",
    )(lam_vecs.astype(F32), g_sub.reshape(1, LANES).astype(F32), qk, qk, v)


def _transpose_exact(x):
    n = x.shape[1]
    ident = (_iota((n, n), 0) == _iota((n, n), 1)).astype(BF16)
    x1 = x.astype(BF16)
    r1 = x - x1.astype(F32)
    x2 = r1.astype(BF16)
    x3 = (r1 - x2.astype(F32)).astype(BF16)
    return _dot_nt(ident, x1) + _dot_nt(ident, x2) + _dot_nt(ident, x3)


def _proj_delta_kernel(x_ref, mod_ref, g_ref, w_ref, wab_ref, alog_ref, dtb_ref,
                       p_ref, z_ref, gc_ref, gct_ref, beta_ref, *, nqkv):
    nb = x_ref.shape[0]
    hs = [_norm_mod(x_ref[k], g_ref[...], mod_ref[k, 0], 0, 1) for k in range(nb)]
    h_all = jnp.concatenate([h.astype(BF16) for h in hs], axis=0)
    nout = w_ref.shape[1]
    for c0 in range(0, nout, COL_CHUNK):
        p = _dot(h_all, w_ref[:, c0:c0 + COL_CHUNK]).astype(BF16)
        for k in range(nb):
            if c0 < nqkv:
                p_ref[k, :, c0:c0 + COL_CHUNK] = _rows(p, k)
            else:
                z_ref[k, :, c0 - nqkv:c0 - nqkv + COL_CHUNK] = _rows(p, k)
    nd = 2 * A_HEADS
    ri = _iota((TM, TM), 0)
    ci = _iota((TM, TM), 1)
    same = (ri // A_CHUNK) == (ci // A_CHUNK)
    lbd = (same & (ci <= ri)).astype(BF16)
    ubd = (same & (ci >= ri)).astype(BF16)
    fwd_cols = _iota((TM, nd), 1) < A_HEADS
    for k in range(nb):
        h_hi, h_lo = _split2(hs[k])
        ab = _dot(h_hi, wab_ref[0]) + _dot(h_hi, wab_ref[1]) + _dot(h_lo, wab_ref[0])
        g = -jnp.exp(alog_ref[...]) * _softplus(ab[:, :nd] + dtb_ref[...])
        beta_ref[k] = _sigmoid(ab[:, nd:])
        g_hi, g_lo = _split2(g)
        gc = jnp.where(fwd_cols, _dot(lbd, g_hi) + _dot(lbd, g_lo), _dot(ubd, g_hi) + _dot(ubd, g_lo))
        gc_ref[k] = gc
        gct_ref[k] = _transpose_exact(gc)


def _proj_delta(xs, mod, g, w_main, wab, a_log, dt_bias, nct_m):
    bn, t, d = xs.shape
    nout = w_main.shape[1]
    nqkv = 2 * A_HEADS * A_DK + A_HEADS * A_DV
    assert nqkv % COL_CHUNK == 0 and nout % COL_CHUNK == 0
    nd = 2 * A_HEADS
    nb = _batch_group(bn)
    row = lambda a: a.reshape(1, nd).astype(F32)
    full = lambda shape: pl.BlockSpec(shape, lambda b, i: (0,) * len(shape))
    return pl.pallas_call(
        functools.partial(_proj_delta_kernel, nqkv=nqkv),
        grid=(bn // nb, t // TM),
        in_specs=[pl.BlockSpec((nb, TM, d), lambda b, i: (b, i, 0)),
                  pl.BlockSpec((nb, 1, 6, d), lambda b, i: (b, _seg(i, nct_m), 0, 0)),
                  full((1, d)), full((d, nout)), full((2, d, 2 * nd)), full((1, nd)), full((1, nd))],
        out_specs=[pl.BlockSpec((nb, TM, nqkv), lambda b, i: (b, i, 0)),
                   pl.BlockSpec((nb, TM, nout - nqkv), lambda b, i: (b, i, 0)),
                   pl.BlockSpec((nb, TM, nd), lambda b, i: (b, i, 0)),
                   pl.BlockSpec((nb, nd, TM), lambda b, i: (b, 0, i)),
                   pl.BlockSpec((nb, TM, nd), lambda b, i: (b, i, 0))],
        out_shape=[jax.ShapeDtypeStruct((bn, t, nqkv), BF16),
                   jax.ShapeDtypeStruct((bn, t, nout - nqkv), BF16),
                   jax.ShapeDtypeStruct((bn, t, nd), F32),
                   jax.ShapeDtypeStruct((bn, nd, t), F32),
                   jax.ShapeDtypeStruct((bn, t, nd), F32)],
        compiler_params=_cp(("arbitrary", "arbitrary"), VMEM_LIMIT),
        name="proj_delta",
    )(xs, mod, g, w_main, wab, row(a_log), row(dt_bias))


def _conv_kernel(p_ref, w_ref, o_ref, scr, *, l, t, ch, pad):
    c = pl.program_id(1)
    scr[0:pad, :] = jnp.zeros((pad, LANES), F32)
    scr[t + pad:t + 2 * pad, :] = jnp.zeros((pad, LANES), F32)
    scr[pad:t + pad, :] = p_ref[0].astype(F32)
    w = w_ref[...]
    half = w.shape[0] // 2
    for r0 in range(0, t, ch):
        near = (r0 <= l + half) and (r0 + ch >= l - half)
        tt = r0 + _iota((ch, 1), 0)
        acc = None
        for d in range(-half, half + 1):
            xd = scr[pad + r0 + d:pad + r0 + d + ch, :]
            if near and d != 0:
                xd = jnp.where(((tt + d) < l) == (tt < l), xd, 0.0)
            term = xd * w[d + half:d + half + 1]
            acc = term if acc is None else acc + term
        y = _silu(acc)
        nrm = y * lax.rsqrt(jnp.sum(y * y, axis=-1, keepdims=True) + EPS)
        out = jnp.where(c < A_HEADS, nrm * (A_DK ** -0.5), jnp.where(c < 2 * A_HEADS, nrm, y))
        o_ref[0, r0:r0 + ch, :] = out.astype(BF16)


def _delta_conv(p, conv_w, l):
    bn, t, nq = p.shape
    ch = 384 if t % 384 == 0 else TQ
    pad = 8
    kw = conv_w.shape[0]
    return pl.pallas_call(
        functools.partial(_conv_kernel, l=l, t=t, ch=ch, pad=pad),
        grid=(bn, nq // LANES),
        in_specs=[pl.BlockSpec((1, t, LANES), lambda b, c: (b, 0, c)),
                  pl.BlockSpec((kw, LANES), lambda b, c: (0, c))],
        out_specs=pl.BlockSpec((1, t, LANES), lambda b, c: (b, 0, c)),
        out_shape=jax.ShapeDtypeStruct((bn, t, nq), BF16),
        scratch_shapes=[pltpu.VMEM((t + 2 * pad, LANES), F32)],
        compiler_params=_cp(("arbitrary", "arbitrary"), VMEM_LIMIT),
        name="delta_conv",
    )(p, conv_w.astype(F32))


def _merge_masks(ii, jj, lower):
    masks = []
    s = 1
    while s < A_CHUNK:
        grp = (ii // (2 * s)) == (jj // (2 * s))
        odd_i, odd_j = (ii // s) % 2 == 1, (jj // s) % 2 == 1
        masks.append(grp & odd_i & ~odd_j if lower else grp & ~odd_i & odd_j)
        s *= 2
    return masks


def _delta_prep_kernel(q_ref, k_ref, v_ref, gc_ref, gct_ref, beta_ref,
                       uf, ub, wf, wb, qdf, qdb, qkf, qkb, kdtf, kdtb, eg_ref, *, hb):
    hblk = pl.program_id(1)
    ii = _iota((TQ, TQ), 0)
    jj = _iota((TQ, TQ), 1)
    same = (ii // A_CHUNK) == (jj // A_CHUNK)
    eye = (ii == jj).astype(F32)
    merge = (_merge_masks(ii, jj, True), _merge_masks(ii, jj, False))
    tri = ((same & (ii >= jj), same & (ii > jj), (ii // A_CHUNK) * A_CHUNK + (A_CHUNK - 1)),
           (same & (ii <= jj), same & (ii < jj), (ii // A_CHUNK) * A_CHUNK))
    lane_d = _iota((TQ, 2 * A_HEADS), 1)
    gc_all = gc_ref[0]
    beta_all = beta_ref[0]
    outs = ((uf, wf, qdf, qkf, kdtf), (ub, wb, qdb, qkb, kdtb))
    heads = []
    for hh in range(hb):
        sl = slice(hh * LANES, (hh + 1) * LANES)
        qb, kb, vb = q_ref[0, :, sl], k_ref[0, :, sl], v_ref[0, :, sl]
        heads.append((sl, qb.astype(F32), kb.astype(F32), vb.astype(F32), _dot_nt(kb, kb), _dot_nt(qb, kb)))
    probs = []
    for hh, (sl, q, k, v, kk, qk) in enumerate(heads):
        for d in range(2):
            incl, strict, last = tri[d]
            idx = d * A_HEADS + hblk * hb + hh
            gcc = jnp.sum(jnp.where(lane_d == idx, gc_all, 0.0), axis=1, keepdims=True)
            bet = jnp.sum(jnp.where(lane_d == idx, beta_all, 0.0), axis=1, keepdims=True)
            gcr = gct_ref[0, pl.ds(idx, 1), :]
            dm = jnp.exp(jnp.where(incl, gcc - gcr, NEG))
            glast = jnp.sum(jnp.where(jj == last, gcr, 0.0), axis=1, keepdims=True)
            a = jnp.where(strict, bet * kk * dm, 0.0)
            probs.append((hh, d, gcc, bet, dm, glast, a))
    ts = [eye - jnp.where(merge[p[1]][0], p[6], 0.0) for p in probs]
    for lvl in range(1, len(merge[0])):
        tbs = [t.astype(BF16) for t in ts]
        ys = [_dot(tb, jnp.where(merge[p[1]][lvl], p[6], 0.0).astype(BF16)) for tb, p in zip(tbs, probs)]
        xs = [_dot(y.astype(BF16), tb) for y, tb in zip(ys, tbs)]
        ts = [t - x for t, x in zip(ts, xs)]
    sols = []
    for t, (hh, d, gcc, bet, dm, glast, a) in zip(ts, probs):
        _, q, k, v, kk, qk = heads[hh]
        rhs = jnp.concatenate([v * bet, k * (bet * jnp.exp(gcc))], axis=1).astype(BF16)
        sols.append(_dot(t.astype(BF16), rhs))
    for sol, (hh, d, gcc, bet, dm, glast, a) in zip(sols, probs):
        sl, q, k, v, kk, qk = heads[hh]
        u_o, w_o, qd_o, qk_o, kdt_o = outs[d]
        u_o[0, :, sl] = sol[:, :LANES]
        w_o[0, :, sl] = sol[:, LANES:].astype(BF16)
        qd_o[0, :, sl] = (q * jnp.exp(gcc)).astype(BF16)
        qk_o[0, :, sl] = (qk * dm).astype(BF16)
        kdt_o[0, sl, :] = (k * jnp.exp(glast - gcc)).T.astype(BF16)
        eglast = jnp.exp(glast)
        for cch in range(TQ // A_CHUNK):
            row = (d * hb + hh) * (TQ // A_CHUNK) + cch
            eg_ref[0, 0, 0, row:row + 1, :] = jnp.broadcast_to(eglast[cch * A_CHUNK:cch * A_CHUNK + 1], (1, LANES))


def _delta_prep(qkv, gc, gct, beta, hb):
    bn, t, _ = qkv.shape
    nt = t // TQ
    nhb = A_HEADS // hb
    wdt = A_HEADS * A_DV
    tok = lambda off: pl.BlockSpec((1, TQ, hb * LANES), lambda b, h, i: (b, i, off + h))
    nd = 2 * A_HEADS
    tok_shape = lambda dt: jax.ShapeDtypeStruct((bn, t, wdt), dt)
    return pl.pallas_call(
        functools.partial(_delta_prep_kernel, hb=hb),
        grid=(bn, nhb, nt),
        in_specs=[tok(0), tok(nhb), tok(2 * nhb),
                  pl.BlockSpec((1, TQ, nd), lambda b, h, i: (b, i, 0)),
                  pl.BlockSpec((1, nd, TQ), lambda b, h, i: (b, 0, i)),
                  pl.BlockSpec((1, TQ, nd), lambda b, h, i: (b, i, 0))],
        out_specs=[tok(0)] * 8 + [pl.BlockSpec((1, hb * LANES, TQ), lambda b, h, i: (b, h, i))] * 2
                  + [pl.BlockSpec((1, 1, 1, 4 * hb, LANES), lambda b, h, i: (b, h, i, 0, 0))],
        out_shape=[tok_shape(F32), tok_shape(F32)] + [tok_shape(BF16)] * 6
                  + [jax.ShapeDtypeStruct((bn, wdt, t), BF16)] * 2
                  + [jax.ShapeDtypeStruct((bn, nhb, nt, 4 * hb, LANES), F32)],
        compiler_params=_cp(("arbitrary", "arbitrary", "arbitrary"), VMEM_LIMIT),
        name="delta_prep",
    )(qkv, qkv, qkv, gc, gct, beta)


def _delta_scan_kernel(uf, wf, qdf, qkf, kdtf, egf, ub, wb, qdb, qkb, kdtb, egb, of_ref, ob_ref, s_ref, *, hb):
    @pl.when(pl.program_id(1) == 0)
    def _():
        s_ref[...] = jnp.zeros(s_ref.shape, F32)

    zeros = jnp.zeros((A_CHUNK, LANES), BF16)
    dirs = ((uf, wf, qdf, qkf, kdtf, egf, of_ref, (0, 1)), (ub, wb, qdb, qkb, kdtb, egb, ob_ref, (1, 0)))
    chains = [(d, head) for d in range(2) for head in range(A_HEADS)]
    states = [s_ref[d, head] for d, head in chains]
    for step in range(TQ // A_CHUNK):
        ws_all = []
        for (d, head), s in zip(chains, states):
            u, w, qd, qk, kdt, eg, o_ref, order = dirs[d]
            rs = slice(order[step] * A_CHUNK, (order[step] + 1) * A_CHUNK)
            sl = slice(head * LANES, (head + 1) * LANES)
            ws_all.append(_dot(jnp.concatenate([w[0, rs, sl], qd[0, rs, sl]], axis=0), s.astype(BF16)))
        vfulls = []
        for (d, head), ws in zip(chains, ws_all):
            u, w, qd, qk, kdt, eg, o_ref, order = dirs[d]
            cch = order[step]
            rs = slice(cch * A_CHUNK, (cch + 1) * A_CHUNK)
            sl = slice(head * LANES, (head + 1) * LANES)
            vn = (u[0, rs, sl] - ws[:A_CHUNK]).astype(BF16)
            vfull = jnp.concatenate([vn, zeros] if cch == 0 else [zeros, vn], axis=0)
            vfulls.append(vfull)
            o_ref[0, rs, sl] = ws[A_CHUNK:] + _dot(qk[0, rs, sl], vfull)
        new_states = []
        for (d, head), s, vfull in zip(chains, states, vfulls):
            u, w, qd, qk, kdt, eg, o_ref, order = dirs[d]
            hblk, hh = divmod(head, hb)
            row = (d * hb + hh) * 2 + order[step]
            sl = slice(head * LANES, (head + 1) * LANES)
            new_states.append(s * eg[0, hblk, 0, row:row + 1, :] + _dot(kdt[0, sl, :], vfull))
        states = new_states
    for (d, head), s in zip(chains, states):
        s_ref[d, head] = s


def _delta_scan(prep, l, hb):
    uf, ub, wf, wb, qdf, qdb, qkf, qkb, kdtf, kdtb, eg = prep
    bn, t, wdt = uf.shape
    nt = t // TQ
    nct = l // TQ
    nhb = A_HEADS // hb
    fwd = lambda s: s
    bwd = lambda s: jnp.where(s < nct, nct - 1 - s, nt - 1 - (s - nct))
    tok = lambda f: pl.BlockSpec((1, TQ, wdt), lambda b, s: (b, f(s), 0))
    tr = lambda f: pl.BlockSpec((1, wdt, TQ), lambda b, s: (b, 0, f(s)))
    egs = lambda f: pl.BlockSpec((1, nhb, 1, 4 * hb, LANES), lambda b, s: (b, 0, f(s), 0, 0))
    return pl.pallas_call(
        functools.partial(_delta_scan_kernel, hb=hb),
        grid=(bn, nt),
        in_specs=[tok(fwd)] * 4 + [tr(fwd), egs(fwd)] + [tok(bwd)] * 4 + [tr(bwd), egs(bwd)],
        out_specs=[tok(fwd), tok(bwd)],
        out_shape=[jax.ShapeDtypeStruct((bn, t, wdt), F32)] * 2,
        scratch_shapes=[pltpu.VMEM((2, A_HEADS, A_DK, A_DV), F32)],
        compiler_params=_cp(("arbitrary", "arbitrary"), VMEM_LIMIT),
        name="delta_scan",
    )(uf, wf, qdf, qkf, kdtf, eg, ub, wb, qdb, qkb, kdtb, eg)


def _residual_router(y, w_ref, x_ref, mod_ref, gffn_ref, wr_ref, xo_ref, h_ref, aff_ref):
    o = _dot(y, w_ref[...])
    for k in range(x_ref.shape[0]):
        mod = mod_ref[k, 0]
        xn = x_ref[k] + mod[2:3] * _rows(o, k)
        xo_ref[k] = xn
        h = _norm_mod(xn, gffn_ref[...], mod, 3, 4)
        h_hi, h_lo = _split2(h)
        h_ref[k] = h_hi
        lg = _dot(h_hi, wr_ref[0]) + _dot(h_hi, wr_ref[1]) + _dot(h_lo, wr_ref[0])
        e = jnp.exp(lg - jnp.max(lg, axis=-1, keepdims=True))
        aff_ref[k] = _transpose_exact(e / jnp.sum(e, axis=-1, keepdims=True))


def _out_kernel(y_ref, *rest):
    _residual_router(jnp.concatenate([y_ref[k] for k in range(y_ref.shape[0])], axis=0), *rest)


def _out_delta_kernel(of_ref, ob_ref, z_ref, gout_ref, *rest):
    rows = []
    for k in range(of_ref.shape[0]):
        o = of_ref[k] + ob_ref[k]
        parts = []
        for hd in range(A_HEADS):
            sl = slice(hd * A_DV, (hd + 1) * A_DV)
            oh = o[:, sl]
            ms = jnp.mean(oh * oh, axis=-1, keepdims=True)
            parts.append((oh * lax.rsqrt(ms + EPS) * gout_ref[...] * _silu(z_ref[k, :, sl].astype(F32))).astype(BF16))
        rows.append(jnp.concatenate(parts, axis=1))
    _residual_router(jnp.concatenate(rows, axis=0), *rest)


def _out_proj(pre, w_out, xs, mod, g_ffn, wr, nct_m, delta):
    bn, t, d = xs.shape
    k = w_out.shape[0]
    ne = wr.shape[2]
    nb = _batch_group(bn)
    tok = lambda width: pl.BlockSpec((nb, TM, width), lambda b, i: (b, i, 0))
    full = lambda shape: pl.BlockSpec(shape, lambda b, i: (0,) * len(shape))
    if delta:
        of, ob, z, gout = pre
        head_specs = [tok(k), tok(k), tok(k), full((1, A_DV))]
        head_args = (of, ob, z, gout.reshape(1, A_DV).astype(F32))
        body = _out_delta_kernel
    else:
        head_specs = [tok(k)]
        head_args = (pre,)
        body = _out_kernel
    return pl.pallas_call(
        body,
        grid=(bn // nb, t // TM),
        in_specs=head_specs + [full((k, d)), tok(d),
                               pl.BlockSpec((nb, 1, 6, d), lambda b, i: (b, _seg(i, nct_m), 0, 0)),
                               full((1, d)), full((2, d, ne))],
        out_specs=[tok(d), tok(d), pl.BlockSpec((nb, ne, TM), lambda b, i: (b, 0, i))],
        out_shape=[jax.ShapeDtypeStruct((bn, t, d), F32), jax.ShapeDtypeStruct((bn, t, d), BF16),
                   jax.ShapeDtypeStruct((bn, ne, t), F32)],
        compiler_params=_cp(("arbitrary", "arbitrary"), VMEM_LIMIT),
        name="out_delta" if delta else "out_proj",
    )(*head_args, w_out, xs, mod, g_ffn, wr)


def _kth_largest_bits(bits, k):
    def body(it, thr):
        cand = thr | jnp.left_shift(jnp.int32(1), 30 - it)
        cnt = jnp.sum((bits >= cand).astype(I32), axis=1, keepdims=True)
        return jnp.where(cnt >= k, cand, thr)
    return lax.fori_loop(0, 31, body, jnp.zeros((bits.shape[0], 1), I32))


def _select_kernel(aff_ref, posd_ref, offs_ref, posc_ref, gated_ref, *, l, t, cap_c, cap_x):
    a = aff_ref[0]
    ne = a.shape[0]
    bits = pltpu.bitcast(a, I32)
    upper = (_iota((LANES, LANES), 0) <= _iota((LANES, LANES), 1)).astype(BF16)
    ident = (_iota((LANES, LANES), 0) == _iota((LANES, LANES), 1)).astype(BF16)
    nt = t // LANES
    sel = [None] * nt
    for s0, s1, cap in ((0, l, cap_c), (l, t, cap_x)):
        bseg = bits[:, s0:s1]
        thr = _kth_largest_bits(bseg, cap)
        gtf = jnp.where(bseg > thr, 1.0, 0.0)
        eqf = jnp.where(bseg == thr, 1.0, 0.0)
        need = cap - jnp.sum(gtf, axis=1, keepdims=True)
        run = jnp.zeros((ne, 1), F32)
        for j in range((s1 - s0) // LANES):
            ej = eqf[:, j * LANES:(j + 1) * LANES]
            inc = _dot(ej.astype(BF16), upper)
            keep = jnp.where(inc - ej + run < need, ej, 0.0)
            sel[s0 // LANES + j] = jnp.maximum(gtf[:, j * LANES:(j + 1) * LANES], keep)
            run = run + inc[:, LANES - 1:LANES]
    run = jnp.zeros((ne, 1), F32)
    offs = jnp.zeros((ne, LANES), I32)
    lane = _iota((ne, LANES), 1)
    for j in range(nt):
        sj = sel[j]
        inc = _dot(sj.astype(BF16), upper)
        pos = jnp.where(sj > 0.0, inc - sj + run, -1.0)
        posd_ref[0, j] = pos.astype(I32)
        offs = jnp.where(lane == j, run.astype(I32), offs)
        p_hi, p_lo = _split2(pos)
        posc_ref[0, j * LANES:(j + 1) * LANES, :] = (_dot_nt(ident, p_hi) + _dot_nt(ident, p_lo)).astype(I32)
        gated_ref[0, j] = jnp.where(sj > 0.0, a[:, j * LANES:(j + 1) * LANES], 0.0)
        run = run + inc[:, LANES - 1:LANES]
    offs_ref[0] = jnp.where(lane == nt, run.astype(I32), offs)


def _select(aff, l, cap_c, cap_x):
    bn, ne, t = aff.shape
    nt = t // LANES
    return pl.pallas_call(
        functools.partial(_select_kernel, l=l, t=t, cap_c=cap_c, cap_x=cap_x),
        grid=(bn,),
        in_specs=[pl.BlockSpec((1, ne, t), lambda b: (b, 0, 0))],
        out_specs=[pl.BlockSpec((1, nt, ne, LANES), lambda b: (b, 0, 0, 0)),
                   pl.BlockSpec((1, ne, LANES), lambda b: (b, 0, 0)),
                   pl.BlockSpec((1, t, ne), lambda b: (b, 0, 0)),
                   pl.BlockSpec((1, nt, ne, LANES), lambda b: (b, 0, 0, 0))],
        out_shape=[jax.ShapeDtypeStruct((bn, nt, ne, LANES), I32), jax.ShapeDtypeStruct((bn, ne, LANES), I32),
                   jax.ShapeDtypeStruct((bn, t, ne), I32), jax.ShapeDtypeStruct((bn, nt, ne, LANES), F32)],
        compiler_params=_cp(("arbitrary",), VMEM_LIMIT),
        name="route_select",
    )(aff)


GATHER_TOK = 2 * TQ
GATHER_WIN = 64
EXPERT_GROUP = 4
COMBINE_WIN = TQ + 16
PACK_WIN = 48
EXPERT_VMEM_LIMIT = 60 * 1024 * 1024


def _gather_group(offs_ref, h_ref, pos_ref, gate_ref, xg_ref, gr_ref, b, e0, ne, nt):
    xg_ref[...] = jnp.zeros(xg_ref.shape, BF16)
    gr_ref[...] = jnp.zeros(gr_ref.shape, F32)
    tpt = GATHER_TOK // LANES
    riota = _iota((GATHER_WIN, GATHER_TOK), 0)
    stacked = (EXPERT_GROUP * GATHER_WIN, ne)
    pick = _iota(stacked, 1) == e0 + _iota(stacked, 0) // GATHER_WIN

    def bounds(j, k):
        base = (b * ne + e0 + k) * LANES
        aoff = (offs_ref[base + j * tpt] // 16) * 16
        return aoff, (offs_ref[base + (j + 1) * tpt] - aoff + GATHER_WIN - 1) // GATHER_WIN

    def add_windows(j, w):
        tok = pl.ds(pl.multiple_of(j * GATHER_TOK, GATHER_TOK), GATHER_TOK)
        starts, hots = [], []
        for k in range(EXPERT_GROUP):
            aoff, nwin = bounds(j, k)
            live = w < jnp.maximum(nwin, 1)
            pos = jnp.concatenate([pos_ref[0, j * tpt + c, pl.ds(e0 + k, 1), :] for c in range(tpt)], axis=1)
            first = jnp.where(live, aoff + w * GATHER_WIN, -(1 << 20))
            hots.append(jnp.where(riota == pos - first, 1.0, 0.0).astype(BF16))
            starts.append(pl.multiple_of(aoff + jnp.where(live, w, 0) * GATHER_WIN, 16))
        lhs = jnp.concatenate(hots, axis=0)
        rows = _dot(lhs, h_ref[0, tok, :])
        gd = jnp.concatenate([gate_ref[0, j * tpt + c] for c in range(tpt)], axis=1)
        g1 = gd.astype(BF16)
        r1 = gd - g1.astype(F32)
        g2 = r1.astype(BF16)
        g3 = (r1 - g2.astype(F32)).astype(BF16)
        gall = _dot_nt(lhs, g1) + _dot_nt(lhs, g2) + _dot_nt(lhs, g3)
        gcol = jnp.sum(jnp.where(pick, gall, 0.0), axis=1, keepdims=True)
        for k in range(EXPERT_GROUP):
            rs = slice(k * GATHER_WIN, (k + 1) * GATHER_WIN)
            dst = pl.ds(starts[k], GATHER_WIN)
            xg_ref[k, dst, :] = (xg_ref[k, dst, :].astype(F32) + rows[rs]).astype(BF16)
            gr_ref[k, dst, :] += jnp.broadcast_to(gcol[rs], (GATHER_WIN, LANES))

    def tile(j, carry):
        add_windows(j, 0)
        most = bounds(j, 0)[1]
        for k in range(1, EXPERT_GROUP):
            most = jnp.maximum(most, bounds(j, k)[1])

        def extra(w, carry2):
            add_windows(j, w)
            return carry2

        lax.fori_loop(1, most, extra, 0)
        return carry

    lax.fori_loop(0, nt // tpt, tile, 0)


def _expert_kernel(offs_ref, h_ref, pos_ref, gate_ref, wgu_ref, wd_ref, y_ref, xg_ref, gr_ref, acc_ref, *, nt, r, fc):
    b = pl.program_id(0)
    e = pl.program_id(1)
    slot = e % EXPERT_GROUP

    @pl.when(slot == 0)
    def _():
        _gather_group(offs_ref, h_ref, pos_ref, gate_ref, xg_ref, gr_ref, b, e, pl.num_programs(1), nt)

    xg = xg_ref[slot, pl.ds(0, r), :]
    f = wd_ref.shape[1]
    for c in range(f // fc):
        g = _dot(xg, wgu_ref[0, :, c * fc:(c + 1) * fc])
        u = _dot(xg, wgu_ref[0, :, f + c * fc:f + (c + 1) * fc])
        part = _dot((_silu(g) * u).astype(BF16), wd_ref[0, c * fc:(c + 1) * fc, :])
        if c == 0:
            acc_ref[...] = part
        else:
            acc_ref[...] += part
    y_ref[0, 0] = (acc_ref[...] * gr_ref[slot, pl.ds(0, r), :][:, 0:1]).astype(BF16)


def _experts(offs, hf, posd, gated, wgu, wd, r):
    bn, t, d = hf.shape
    ne, _, f2 = wgu.shape
    nt = t // TQ
    rows = r + GATHER_WIN
    assert ne % EXPERT_GROUP == 0 and rows % 16 == 0
    return pl.pallas_call(
        functools.partial(_expert_kernel, nt=nt, r=r, fc=min(256, f2 // 2)),
        grid_spec=pltpu.PrefetchScalarGridSpec(
            num_scalar_prefetch=1,
            grid=(bn, ne),
            in_specs=[pl.BlockSpec((1, t, d), lambda b, e, o: (b, 0, 0), pipeline_mode=pl.Buffered(1)),
                      pl.BlockSpec((1, nt, ne, LANES), lambda b, e, o: (b, 0, 0, 0)),
                      pl.BlockSpec((1, nt, ne, LANES), lambda b, e, o: (b, 0, 0, 0)),
                      pl.BlockSpec((1, d, f2), lambda b, e, o: (e, 0, 0)),
                      pl.BlockSpec((1, f2 // 2, d), lambda b, e, o: (e, 0, 0))],
            out_specs=pl.BlockSpec((1, 1, r, d), lambda b, e, o: (b, e, 0, 0)),
            scratch_shapes=[pltpu.VMEM((EXPERT_GROUP, rows, d), BF16), pltpu.VMEM((EXPERT_GROUP, rows, LANES), F32),
                            pltpu.VMEM((r, d), F32)]),
        out_shape=jax.ShapeDtypeStruct((bn, ne, r, d), BF16),
        compiler_params=_cp(("arbitrary", "arbitrary"), EXPERT_VMEM_LIMIT),
        name="experts",
    )(offs, hf, posd, gated, wgu, wd)


def _combine_kernel(offs_ref, y_ref, posc_ref, x_ref, mod_ref, o_ref, *, r, win, pack):
    b = pl.program_id(0)
    j = pl.program_id(1)
    ne = y_ref.shape[1]
    pc = posc_ref[0]
    offs = [offs_ref[(b * ne + e) * LANES + j] for e in range(ne)]
    ends = [offs_ref[(b * ne + e) * LANES + j + 1] for e in range(ne)]
    los = [jnp.minimum((offs[e] // 16) * 16, r - pack) for e in range(ne)]
    fits = ends[0] - los[0] <= pack
    for e in range(1, ne):
        fits = fits & (ends[e] - los[e] <= pack)
    gate_mod = mod_ref[0, 0][5:6]

    @pl.when(fits)
    def _():
        lane_e = _iota((1, ne), 1)
        lo_row = jnp.zeros((1, ne), I32)
        for e in range(ne):
            lo_row = jnp.where(lane_e == e, los[e], lo_row)
        rel = pc - lo_row
        rel = jnp.where((pc >= 0) & (rel >= 0) & (rel < pack), rel, -1).astype(F32).astype(BF16)
        spread = (_iota((ne, ne * pack), 0) == _iota((ne, ne * pack), 1) // pack).astype(BF16)
        want = (_iota((TQ, ne * pack), 1) % pack).astype(F32)
        onehot = jnp.where(_dot(rel, spread) == want, 1.0, 0.0).astype(BF16)
        ycat = jnp.concatenate([y_ref[0, e, pl.ds(pl.multiple_of(los[e], 16), pack), :] for e in range(ne)], axis=0)
        o_ref[0] = x_ref[0] + gate_mod * _dot(onehot, ycat)

    @pl.when(jnp.logical_not(fits))
    def _():
        liota = _iota((TQ, win), 1)
        acc = jnp.zeros(x_ref.shape[1:], F32)
        for e in range(ne):
            aoff = pl.multiple_of(jnp.minimum((offs[e] // 16) * 16, r - win), 16)
            onehot = jnp.where(liota == pc[:, e:e + 1] - aoff, 1.0, 0.0).astype(BF16)
            acc = acc + _dot(onehot, y_ref[0, e, pl.ds(aoff, win), :])
        o_ref[0] = x_ref[0] + gate_mod * acc


def _combine(offs, y, posc, xs, mod, nct):
    bn, t, d = xs.shape
    ne, r = y.shape[1], y.shape[2]
    win = min(COMBINE_WIN, r)
    pack = min(PACK_WIN, r)
    assert (r - win) % 16 == 0 and (r - pack) % 16 == 0
    tok = lambda width: pl.BlockSpec((1, TQ, width), lambda b, j, o: (b, j, 0))
    return pl.pallas_call(
        functools.partial(_combine_kernel, r=r, win=win, pack=pack),
        grid_spec=pltpu.PrefetchScalarGridSpec(
            num_scalar_prefetch=1,
            grid=(bn, t // TQ),
            in_specs=[pl.BlockSpec((1, ne, r, d), lambda b, j, o: (b, 0, 0, 0), pipeline_mode=pl.Buffered(1)),
                      tok(ne), tok(d),
                      pl.BlockSpec((1, 1, 6, d), lambda b, j, o: (b, _seg(j, nct), 0, 0))],
            out_specs=tok(d)),
        out_shape=jax.ShapeDtypeStruct((bn, t, d), F32),
        compiler_params=_cp(("arbitrary", "arbitrary"), VMEM_LIMIT),
        name="moe_combine",
    )(offs, y, posc, xs, mod)


def _moe(xs, hf, aff, mod, wgu, wd, l):
    bn, t, _ = xs.shape
    cap_c = EC_CAPACITY * l // N_EXPERTS
    cap_x = EC_CAPACITY * (t - l) // N_EXPERTS
    posd, offs, posc, gated = _select(aff, l, cap_c, cap_x)
    offs = offs.reshape(-1)
    y = _experts(offs, hf, posd, gated, wgu, wd, cap_c + cap_x)
    return _combine(offs, y, posc, xs, mod, l // TQ)


def _hi_lo(w):
    hi = w.astype(BF16)
    return jnp.stack([hi, (w - hi.astype(F32)).astype(BF16)])


def _mixer_delta(xs, mod, g_mix, w_in, conv_w, a_log, dt_bias, l, hb=4):
    nqkvz = 2 * A_HEADS * A_DK + 2 * A_HEADS * A_DV
    wab = w_in[:, nqkvz:]
    p, z, gc, gct, beta = _proj_delta(xs, mod, g_mix, w_in[:, :nqkvz].astype(BF16), _hi_lo(wab), a_log, dt_bias, l // TM)
    qkv = _delta_conv(p, conv_w, l)
    of, ob = _delta_scan(_delta_prep(qkv, gc, gct, beta, hb), l, hb)
    return of, ob, z


def _mixer_swa(xs, mod, g_mix, w_in, qn, kn, sink, cosf, sinf, l):
    nq = B_QHEADS * B_HD
    nk = B_KVHEADS * B_HD
    dup = lambda w: jnp.concatenate([w.reshape(-1, B_KVHEADS, 1, B_HD)] * 2, axis=2).reshape(-1, 2 * nk)
    w = jnp.concatenate([w_in[:, :nq], dup(w_in[:, nq:nq + nk]), dup(w_in[:, nq + nk:])], axis=1).astype(BF16)
    gain = jnp.concatenate([jnp.tile(qn, B_QHEADS) * (B_HD ** -0.5), jnp.tile(kn, 2 * B_KVHEADS)])[None].astype(F32)
    qk, v2 = _proj_qk(xs, mod, g_mix, w, gain, cosf, sinf, nq + 2 * nk, l // TM)
    return _swa_attention(qk, v2, sink, l)


def _mixer_diff(xs, mod, g_mix, w_in, qn, kn, lam_vecs, g_sub, cosf, sinf, l, lam_init):
    nqk = C_HEADS * 2 * C_HD
    gain = jnp.concatenate([jnp.tile(qn, 2 * C_HEADS) * (C_HD ** -0.5 * LOG2E),
                            jnp.tile(kn, 2 * C_HEADS)])[None].astype(F32)
    qk, v = _proj_qk(xs, mod, g_mix, w_in.astype(BF16), gain, cosf, sinf, 2 * nqk, l // TM)
    return _diff_attention(qk, v, lam_vecs, g_sub, l, lam_init)


def kernel(x, c, ctx, c_ctx, w_ada, b_ada, g_mix, g_ffn, a_w_in, a_conv, a_log, a_dt_bias, a_g_out, a_w_out,
           b_w_in, b_q_norm, b_k_norm, b_sink, b_w_out, c_w_in, c_q_norm, c_k_norm, c_lambda, c_g_sub, c_w_out,
           w_router, w_gate_up, w_down):
    depth = w_ada.shape[0]
    n = x.shape[1]
    l = ctx.shape[1]
    assert l % TM == 0 and n % TM == 0
    xs = jnp.concatenate([ctx, x], axis=1)
    mods = _adaln(c, c_ctx, w_ada, b_ada)
    cos_b, sin_b = _rope_tables(n, l, B_HD)
    cos_c, sin_c = _rope_tables(n, l, C_HD)
    nct_m = l // TM
    for layer in range(depth):
        kind, j = layer % N_MIXERS, layer // N_MIXERS
        mod = mods[layer]
        gm = g_mix[layer][None].astype(F32)
        gf = g_ffn[layer][None].astype(F32)
        wr = _hi_lo(w_router[layer])
        if kind == 0:
            pre = _mixer_delta(xs, mod, gm, a_w_in[j], a_conv[j], a_log[j], a_dt_bias[j], l)
            xs, hf, aff = _out_proj(pre + (a_g_out[j],), a_w_out[j].astype(BF16), xs, mod, gf, wr, nct_m, True)
        elif kind == 1:
            pre = _mixer_swa(xs, mod, gm, b_w_in[j], b_q_norm[j], b_k_norm[j], b_sink[j], cos_b, sin_b, l)
            xs, hf, aff = _out_proj(pre, b_w_out[j].astype(BF16), xs, mod, gf, wr, nct_m, False)
        else:
            lam_init = 0.8 - 0.6 * math.exp(-0.3 * layer)
            pre = _mixer_diff(xs, mod, gm, c_w_in[j], c_q_norm[j], c_k_norm[j], c_lambda[j], c_g_sub[j],
                              cos_c, sin_c, l, lam_init)
            xs, hf, aff = _out_proj(pre, c_w_out[j].astype(BF16), xs, mod, gf, wr, nct_m, False)
        xs = _moe(xs, hf, aff, mod, w_gate_up[layer].astype(BF16), w_down[layer].astype(BF16), l)
    return xs[:, l:]
```

```python
import functools
import math

import jax
import jax.numpy as jnp
from jax import lax
from jax.experimental import pallas as pl
from jax.experimental.pallas import tpu as pltpu

F32 = jnp.float32
BF16 = jnp.bfloat16
I32 = jnp.int32

EPS = 1e-6
ROPE_BASE = 10000.0
GRID_W = 64
N_MIXERS = 3

A_HEADS, A_DK, A_DV, A_CHUNK = 8, 128, 128, 64
B_QHEADS, B_KVHEADS, B_HD = 16, 4, 64
B_GROUP = B_QHEADS // B_KVHEADS
C_HEADS, C_HD = 8, 64
N_EXPERTS, EC_CAPACITY = 16, 2

LANES = 128
TM = 256
TQ = 128
NEG = -1e30
VMEM_LIMIT = 56 * 1024 * 1024


def _cp(sem, vmem=None):
    return pltpu.CompilerParams(dimension_semantics=sem, vmem_limit_bytes=vmem)


def _dot(a, b):
    return jnp.dot(a, b, preferred_element_type=F32)


def _dot_nt(a, b):
    return lax.dot_general(a, b, (((1,), (1,)), ((), ())), preferred_element_type=F32)


def _split2(a):
    hi = a.astype(BF16)
    return hi, (a - hi.astype(F32)).astype(BF16)


def _sigmoid(x):
    return 1.0 / (1.0 + jnp.exp(-x))


def _silu(x):
    return x * _sigmoid(x)


def _softplus(x):
    return jnp.maximum(x, 0.0) + jnp.log(1.0 + jnp.exp(-jnp.abs(x)))


def _norm_mod(x, g, mod, i_shift, i_scale):
    ms = jnp.mean(x * x, axis=-1, keepdims=True)
    y = x * lax.rsqrt(ms + EPS) * g
    return y * (1.0 + mod[i_scale:i_scale + 1]) + mod[i_shift:i_shift + 1]


def _iota(shape, dim):
    return lax.broadcasted_iota(I32, shape, dim)


def _ada_kernel(s_ref, w_ref, b_ref, o_ref):
    s = _silu(s_ref[...])
    s_hi, s_lo = _split2(s)
    w_hi, w_lo = _split2(w_ref[0])
    o_ref[0] = _dot(s_hi, w_hi) + _dot(s_hi, w_lo) + _dot(s_lo, w_hi) + b_ref[0]


def _adaln(c, c_ctx, w_ada, b_ada):
    depth, d, d6 = w_ada.shape
    bn = c.shape[0]
    rows_n = -(-(bn + 1) // 8) * 8
    rows = jnp.zeros((rows_n, d), F32).at[:bn].set(c).at[bn].set(c_ctx)
    nb = d6 // 4
    out = pl.pallas_call(
        _ada_kernel,
        grid=(depth, d6 // nb),
        in_specs=[pl.BlockSpec((rows_n, d), lambda l, j: (0, 0)),
                  pl.BlockSpec((1, d, nb), lambda l, j: (l, 0, j)),
                  pl.BlockSpec((1, 1, nb), lambda l, j: (l, 0, j))],
        out_specs=pl.BlockSpec((1, rows_n, nb), lambda l, j: (l, 0, j)),
        out_shape=jax.ShapeDtypeStruct((depth, rows_n, d6), F32),
        compiler_params=_cp(("arbitrary", "arbitrary"), VMEM_LIMIT),
        name="adaln",
    )(rows, w_ada, b_ada.reshape(depth, 1, d6))
    mx = out[:, :bn].reshape(depth, bn, 6, d)
    mc = jnp.broadcast_to(out[:, bn].reshape(depth, 1, 6, d), (depth, bn, 6, d))
    return jnp.stack([mc, mx], axis=2)


def _seg(i, nct):
    return jnp.where(i < nct, 0, 1)


COL_CHUNK = 1024


def _batch_group(bn):
    return 4 if bn % 4 == 0 else (2 if bn % 2 == 0 else 1)


def _rows(a, k):
    return a[k * TM:(k + 1) * TM]


def _proj_qk_kernel(x_ref, mod_ref, g_ref, w_ref, gain_ref, cos_ref, sin_ref, qk_ref, v_ref, *, nqk):
    nb = x_ref.shape[0]
    h = jnp.concatenate([_norm_mod(x_ref[k], g_ref[...], mod_ref[k, 0], 0, 1).astype(BF16) for k in range(nb)], axis=0)
    grp = (_iota((LANES, LANES), 0) // 64 == _iota((LANES, LANES), 1) // 64).astype(BF16)
    first = (_iota((1, LANES), 1) % 64) < 32
    cs = jnp.concatenate([cos_ref[...]] * nb, axis=0)
    sn = jnp.concatenate([sin_ref[...]] * nb, axis=0)
    nout = w_ref.shape[1]
    for c0 in range(0, nout, COL_CHUNK):
        p = _dot(h, w_ref[:, c0:min(c0 + COL_CHUNK, nout)])
        for t0 in range(c0, min(c0 + COL_CHUNK, nout), LANES):
            xt = p[:, t0 - c0:t0 - c0 + LANES]
            if t0 < nqk:
                sq_hi, sq_lo = _split2(xt * xt)
                ms = (_dot(sq_hi, grp) + _dot(sq_lo, grp)) * (1.0 / 64)
                y = xt * lax.rsqrt(ms + EPS) * gain_ref[:, t0:t0 + LANES]
                rot = jnp.where(first, pltpu.roll(y, 96, 1), pltpu.roll(y, 32, 1))
                out = (y * cs + rot * sn).astype(BF16)
                for k in range(nb):
                    qk_ref[k, :, t0:t0 + LANES] = _rows(out, k)
            else:
                for k in range(nb):
                    v_ref[k, :, t0 - nqk:t0 - nqk + LANES] = _rows(xt, k).astype(BF16)


def _proj_qk(xs, mod, g, w, gain, cosf, sinf, nqk, nct_m):
    bn, t, d = xs.shape
    nout = w.shape[1]
    nb = _batch_group(bn)
    return pl.pallas_call(
        functools.partial(_proj_qk_kernel, nqk=nqk),
        grid=(bn // nb, t // TM),
        in_specs=[pl.BlockSpec((nb, TM, d), lambda b, i: (b, i, 0)),
                  pl.BlockSpec((nb, 1, 6, d), lambda b, i: (b, _seg(i, nct_m), 0, 0)),
                  pl.BlockSpec((1, d), lambda b, i: (0, 0)),
                  pl.BlockSpec((d, nout), lambda b, i: (0, 0)),
                  pl.BlockSpec((1, nqk), lambda b, i: (0, 0)),
                  pl.BlockSpec((TM, LANES), lambda b, i: (i, 0)),
                  pl.BlockSpec((TM, LANES), lambda b, i: (i, 0))],
        out_specs=[pl.BlockSpec((nb, TM, nqk), lambda b, i: (b, i, 0)),
                   pl.BlockSpec((nb, TM, nout - nqk), lambda b, i: (b, i, 0))],
        out_shape=[jax.ShapeDtypeStruct((bn, t, nqk), BF16),
                   jax.ShapeDtypeStruct((bn, t, nout - nqk), BF16)],
        compiler_params=_cp(("arbitrary", "arbitrary"), VMEM_LIMIT),
        name="proj_qk",
    )(xs, mod, g, w, gain, cosf, sinf)


def _rope_tables(n, l, head_dim):
    t = jnp.arange(n)
    n_freq = head_dim // 4
    inv = ROPE_BASE ** (-jnp.arange(n_freq, dtype=F32) / n_freq)
    ang = jnp.concatenate([(t // GRID_W).astype(F32)[:, None] * inv, (t % GRID_W).astype(F32)[:, None] * inv], -1)
    cs, sn = jnp.cos(ang), jnp.sin(ang)
    reps = LANES // head_dim
    cosf = jnp.tile(jnp.concatenate([cs, cs], -1), (1, reps))
    sinf = jnp.tile(jnp.concatenate([-sn, sn], -1), (1, reps))
    cosf = jnp.concatenate([jnp.ones((l, LANES), F32), cosf], 0)
    sinf = jnp.concatenate([jnp.zeros((l, LANES), F32), sinf], 0)
    return cosf, sinf


def _swa_kernel(q_ref, kp_ref, kc_ref, kn_ref, kx_ref, vp_ref, vc_ref, vn_ref, vx_ref, sink_ref, bias_ref, o_ref,
                *, nct, ntx, l):
    xi = pl.program_id(1) - nct
    lo = _iota((TQ, LANES), 1) < 64
    zero = jnp.zeros((TQ, LANES), BF16)
    c = _iota((1, 3 * TQ + l), 1)
    dead = (((c < TQ) & (xi < 1)) | ((c >= 2 * TQ) & (c < 3 * TQ) & (xi + 1 >= ntx)) | ((c < 3 * TQ) & (xi < 0)))
    bias = bias_ref[...] + jnp.where(dead, NEG, 0.0)
    bias4 = jnp.concatenate([bias] * B_GROUP, axis=0)
    gw = B_GROUP * B_HD
    scores = []
    for kv in range(B_KVHEADS):
        ks = slice(kv * LANES, (kv + 1) * LANES)
        qa, qb = q_ref[0, :, kv * gw:kv * gw + LANES], q_ref[0, :, kv * gw + LANES:(kv + 1) * gw]
        q4 = jnp.concatenate([jnp.where(lo, qa, zero), jnp.where(lo, zero, qa),
                              jnp.where(lo, qb, zero), jnp.where(lo, zero, qb)], axis=0)
        kcat = jnp.concatenate([kp_ref[0, :, ks], kc_ref[0, :, ks], kn_ref[0, :, ks], kx_ref[0, :, ks]], axis=0)
        scores.append(_dot_nt(q4, kcat) + bias4)
    probs = []
    for kv, s in enumerate(scores):
        sk = sink_ref[kv]
        m = jnp.maximum(jnp.max(s, axis=-1, keepdims=True), sk)
        p = jnp.exp(s - m)
        probs.append((p.astype(BF16), jnp.sum(p, axis=-1, keepdims=True) + jnp.exp(sk - m)))
    for kv, (p, den) in enumerate(probs):
        ks = slice(kv * LANES, (kv + 1) * LANES)
        vcat = jnp.concatenate([vp_ref[0, :, ks], vc_ref[0, :, ks], vn_ref[0, :, ks], vx_ref[0, :, ks]], axis=0)
        o4 = _dot(p, vcat) / den
        oa = jnp.where(lo, o4[0:TQ], o4[TQ:2 * TQ])
        ob = jnp.where(lo, o4[2 * TQ:3 * TQ], o4[3 * TQ:4 * TQ])
        o_ref[0, :, kv * gw:(kv + 1) * gw] = jnp.concatenate([oa, ob], axis=1).astype(BF16)


def _swa_attention(qk, v2, sink, l):
    bn, t, _ = qk.shape
    nt = t // TQ
    nct = l // TQ
    nq = B_QHEADS * B_HD
    kw = B_KVHEADS * LANES
    prev = lambda i: jnp.maximum(i - 1, 0)
    nxt = lambda i: jnp.minimum(i + 1, nt - 1)
    cur = lambda i: i
    kspec = lambda f: pl.BlockSpec((1, TQ, kw), lambda b, i: (b, f(i), nq // kw))
    vspec = lambda f: pl.BlockSpec((1, TQ, kw), lambda b, i: (b, f(i), 0))
    sinkcol = jnp.repeat(sink.reshape(B_KVHEADS, B_GROUP), TQ, axis=1).reshape(B_KVHEADS, B_GROUP * TQ, 1).astype(F32)
    r = jnp.arange(TQ)[:, None]
    c = jnp.arange(3 * TQ + l)[None, :]
    band = ((c < TQ) & (c >= r)) | ((c >= TQ) & (c < 2 * TQ)) | ((c >= 2 * TQ) & (c - 2 * TQ <= r)) | (c >= 3 * TQ)
    bias = jnp.where(band, 0.0, NEG).astype(F32)
    return pl.pallas_call(
        functools.partial(_swa_kernel, nct=nct, ntx=nt - nct, l=l),
        grid=(bn, nt),
        in_specs=[pl.BlockSpec((1, TQ, nq), lambda b, i: (b, i, 0)),
                  kspec(prev), kspec(cur), kspec(nxt),
                  pl.BlockSpec((1, l, kw), lambda b, i: (b, 0, nq // kw)),
                  vspec(prev), vspec(cur), vspec(nxt),
                  pl.BlockSpec((1, l, kw), lambda b, i: (b, 0, 0)),
                  pl.BlockSpec((B_KVHEADS, B_GROUP * TQ, 1), lambda b, i: (0, 0, 0)),
                  pl.BlockSpec((TQ, 3 * TQ + l), lambda b, i: (0, 0))],
        out_specs=pl.BlockSpec((1, TQ, nq), lambda b, i: (b, i, 0)),
        out_shape=jax.ShapeDtypeStruct((bn, t, nq), BF16),
        compiler_params=_cp(("arbitrary", "arbitrary"), VMEM_LIMIT),
        name="swa_attention",
    )(qk, qk, qk, qk, qk, v2, v2, v2, v2, sinkcol, bias)


LOG2E = 1.4426950408889634
SAFE_BOUND = 60.0


def _diff_kernel(lam_ref, gsub_ref, q_ref, k_ref, v_ref, o_ref, kn_ref, *, nct, l, t, ck, lam_init):
    i = pl.program_id(2)
    q = q_ref[0]
    tq = q.shape[0]
    lo = _iota((tq, LANES), 1) < 64
    zero = jnp.zeros((tq, LANES), BF16)
    q2 = jnp.concatenate([jnp.where(lo, q, zero), jnp.where(lo, zero, q)], axis=0)

    @pl.when(i == 0)
    def _():
        grp = (_iota((LANES, LANES), 0) // 64 == _iota((LANES, LANES), 1) // 64).astype(BF16)
        mx = jnp.zeros((1, LANES), F32)
        for c0 in range(0, t, ck):
            kf = k_ref[0, c0:c0 + ck, :].astype(F32)
            sq_hi, sq_lo = _split2(kf * kf)
            mx = jnp.maximum(mx, jnp.max(_dot(sq_hi, grp) + _dot(sq_lo, grp), axis=0, keepdims=True))
        kn_ref[...] = mx

    q2f = q2.astype(F32)
    qn = jnp.sqrt(jnp.sum(q2f * q2f, axis=1, keepdims=True))
    kn2 = kn_ref[...]
    kn = jnp.sqrt(jnp.where(_iota((2 * tq, 1), 0) < tq, kn2[:, 0:1], kn2[:, 64:65]))
    bound = qn * kn * 1.01 + 1e-6
    safe = jnp.max(bound) <= SAFE_BOUND

    def finish(den, acc):
        lv = lam_ref[...]
        lam = (jnp.exp(jnp.sum(lv[0:1] * lv[1:2], axis=-1, keepdims=True))
               - jnp.exp(jnp.sum(lv[2:3] * lv[3:4], axis=-1, keepdims=True)) + lam_init)
        o = acc[0:tq] / den[0:tq] - lam * (acc[tq:] / den[tq:])
        ms = jnp.mean(o * o, axis=-1, keepdims=True)
        o_ref[0] = (o * lax.rsqrt(ms + EPS) * gsub_ref[...] * (1.0 - lam_init)).astype(BF16)

    def bounded(nkeys):
        step = min(ck, nkeys)
        psum = jnp.zeros((2 * tq, LANES), F32)
        acc = jnp.zeros((2 * tq, LANES), F32)
        for c0 in range(0, nkeys, step):
            p = jnp.exp2(_dot_nt(q2, k_ref[0, c0:c0 + step, :]) - bound)
            for j in range(step // LANES):
                psum = psum + p[:, j * LANES:(j + 1) * LANES]
            acc = acc + _dot(p.astype(BF16), v_ref[0, c0:c0 + step, :])
        finish(jnp.sum(psum, axis=1, keepdims=True), acc)

    def online(nkeys):
        step = min(ck, nkeys)

        def body(j, carry):
            m, den, acc = carry
            st = pl.multiple_of(j * step, step)
            s = _dot_nt(q2, k_ref[0, pl.ds(st, step), :])
            m2 = jnp.maximum(m, jnp.max(s, axis=-1, keepdims=True))
            a = jnp.exp2(m - m2)
            p = jnp.exp2(s - m2)
            den = a * den + jnp.sum(p, axis=-1, keepdims=True)
            acc = a * acc + _dot(p.astype(BF16), v_ref[0, pl.ds(st, step), :])
            return m2, den, acc

        init = (jnp.full((2 * tq, 1), NEG, F32), jnp.zeros((2 * tq, 1), F32), jnp.zeros((2 * tq, LANES), F32))
        _, den, acc = lax.fori_loop(0, nkeys // step, body, init)
        finish(den, acc)

    for is_ctx, nkeys in ((True, l), (False, t)):
        seg = (i < nct) if is_ctx else (i >= nct)
        pl.when(seg & safe)(functools.partial(bounded, nkeys))
        pl.when(seg & jnp.logical_not(safe))(functools.partial(online, nkeys))


def _diff_attention(qk, v, lam_vecs, g_sub, l, lam_init):
    bn, t, _ = qk.shape
    tq = TM
    ck = 768 if t % 768 == 0 else TQ
    kcol = C_HEADS * 2 * C_HD // LANES
    return pl.pallas_call(
        functools.partial(_diff_kernel, nct=l // tq, l=l, t=t, ck=ck, lam_init=lam_init),
        scratch_shapes=[pltpu.VMEM((1, LANES), F32)],
        grid=(bn, C_HEADS, t // tq),
        in_specs=[pl.BlockSpec((4, C_HD), lambda b, h, i: (0, 0)),
                  pl.BlockSpec((1, LANES), lambda b, h, i: (0, 0)),
                  pl.BlockSpec((1, tq, LANES), lambda b, h, i: (b, i, h)),
                  pl.BlockSpec((1, t, LANES), lambda b, h, i: (b, 0, kcol + h)),
                  pl.BlockSpec((1, t, LANES), lambda b, h, i: (b, 0, h))],
        out_specs=pl.BlockSpec((1, tq, LANES), lambda b, h, i: (b, i, h)),
        out_shape=jax.ShapeDtypeStruct((bn, t, C_HEADS * 2 * C_HD), BF16),
        compiler_params=_cp(("arbitrary", "arbitrary", "arbitrary"), VMEM_LIMIT),
        name="diff_attention",
    )(lam_vecs.astype(F32), g_sub.reshape(1, LANES).astype(F32), qk, qk, v)


def _transpose_exact(x):
    n = x.shape[1]
    ident = (_iota((n, n), 0) == _iota((n, n), 1)).astype(BF16)
    x1 = x.astype(BF16)
    r1 = x - x1.astype(F32)
    x2 = r1.astype(BF16)
    x3 = (r1 - x2.astype(F32)).astype(BF16)
    return _dot_nt(ident, x1) + _dot_nt(ident, x2) + _dot_nt(ident, x3)


def _proj_delta_kernel(x_ref, mod_ref, g_ref, w_ref, wab_ref, alog_ref, dtb_ref,
                       p_ref, z_ref, gc_ref, gct_ref, beta_ref, *, nqkv):
    nb = x_ref.shape[0]
    hs = [_norm_mod(x_ref[k], g_ref[...], mod_ref[k, 0], 0, 1) for k in range(nb)]
    h_all = jnp.concatenate([h.astype(BF16) for h in hs], axis=0)
    nout = w_ref.shape[1]
    for c0 in range(0, nout, COL_CHUNK):
        p = _dot(h_all, w_ref[:, c0:c0 + COL_CHUNK]).astype(BF16)
        for k in range(nb):
            if c0 < nqkv:
                p_ref[k, :, c0:c0 + COL_CHUNK] = _rows(p, k)
            else:
                z_ref[k, :, c0 - nqkv:c0 - nqkv + COL_CHUNK] = _rows(p, k)
    nd = 2 * A_HEADS
    ri = _iota((TM, TM), 0)
    ci = _iota((TM, TM), 1)
    same = (ri // A_CHUNK) == (ci // A_CHUNK)
    lbd = (same & (ci <= ri)).astype(BF16)
    ubd = (same & (ci >= ri)).astype(BF16)
    fwd_cols = _iota((TM, nd), 1) < A_HEADS
    for k in range(nb):
        h_hi, h_lo = _split2(hs[k])
        ab = _dot(h_hi, wab_ref[0]) + _dot(h_hi, wab_ref[1]) + _dot(h_lo, wab_ref[0])
        g = -jnp.exp(alog_ref[...]) * _softplus(ab[:, :nd] + dtb_ref[...])
        beta_ref[k] = _sigmoid(ab[:, nd:])
        g_hi, g_lo = _split2(g)
        gc = jnp.where(fwd_cols, _dot(lbd, g_hi) + _dot(lbd, g_lo), _dot(ubd, g_hi) + _dot(ubd, g_lo))
        gc_ref[k] = gc
        gct_ref[k] = _transpose_exact(gc)


def _proj_delta(xs, mod, g, w_main, wab, a_log, dt_bias, nct_m):
    bn, t, d = xs.shape
    nout = w_main.shape[1]
    nqkv = 2 * A_HEADS * A_DK + A_HEADS * A_DV
    assert nqkv % COL_CHUNK == 0 and nout % COL_CHUNK == 0
    nd = 2 * A_HEADS
    nb = _batch_group(bn)
    row = lambda a: a.reshape(1, nd).astype(F32)
    full = lambda shape: pl.BlockSpec(shape, lambda b, i: (0,) * len(shape))
    return pl.pallas_call(
        functools.partial(_proj_delta_kernel, nqkv=nqkv),
        grid=(bn // nb, t // TM),
        in_specs=[pl.BlockSpec((nb, TM, d), lambda b, i: (b, i, 0)),
                  pl.BlockSpec((nb, 1, 6, d), lambda b, i: (b, _seg(i, nct_m), 0, 0)),
                  full((1, d)), full((d, nout)), full((2, d, 2 * nd)), full((1, nd)), full((1, nd))],
        out_specs=[pl.BlockSpec((nb, TM, nqkv), lambda b, i: (b, i, 0)),
                   pl.BlockSpec((nb, TM, nout - nqkv), lambda b, i: (b, i, 0)),
                   pl.BlockSpec((nb, TM, nd), lambda b, i: (b, i, 0)),
                   pl.BlockSpec((nb, nd, TM), lambda b, i: (b, 0, i)),
                   pl.BlockSpec((nb, TM, nd), lambda b, i: (b, i, 0))],
        out_shape=[jax.ShapeDtypeStruct((bn, t, nqkv), BF16),
                   jax.ShapeDtypeStruct((bn, t, nout - nqkv), BF16),
                   jax.ShapeDtypeStruct((bn, t, nd), F32),
                   jax.ShapeDtypeStruct((bn, nd, t), F32),
                   jax.ShapeDtypeStruct((bn, t, nd), F32)],
        compiler_params=_cp(("arbitrary", "arbitrary"), VMEM_LIMIT),
        name="proj_delta",
    )(xs, mod, g, w_main, wab, row(a_log), row(dt_bias))


def _conv_kernel(p_ref, w_ref, o_ref, scr, *, l, t, ch, pad):
    c = pl.program_id(1)
    scr[0:pad, :] = jnp.zeros((pad, LANES), F32)
    scr[t + pad:t + 2 * pad, :] = jnp.zeros((pad, LANES), F32)
    scr[pad:t + pad, :] = p_ref[0].astype(F32)
    w = w_ref[...]
    half = w.shape[0] // 2
    for r0 in range(0, t, ch):
        near = (r0 <= l + half) and (r0 + ch >= l - half)
        tt = r0 + _iota((ch, 1), 0)
        acc = None
        for d in range(-half, half + 1):
            xd = scr[pad + r0 + d:pad + r0 + d + ch, :]
            if near and d != 0:
                xd = jnp.where(((tt + d) < l) == (tt < l), xd, 0.0)
            term = xd * w[d + half:d + half + 1]
            acc = term if acc is None else acc + term
        y = _silu(acc)
        nrm = y * lax.rsqrt(jnp.sum(y * y, axis=-1, keepdims=True) + EPS)
        out = jnp.where(c < A_HEADS, nrm * (A_DK ** -0.5), jnp.where(c < 2 * A_HEADS, nrm, y))
        o_ref[0, r0:r0 + ch, :] = out.astype(BF16)


def _delta_conv(p, conv_w, l):
    bn, t, nq = p.shape
    ch = 384 if t % 384 == 0 else TQ
    pad = 8
    kw = conv_w.shape[0]
    return pl.pallas_call(
        functools.partial(_conv_kernel, l=l, t=t, ch=ch, pad=pad),
        grid=(bn, nq // LANES),
        in_specs=[pl.BlockSpec((1, t, LANES), lambda b, c: (b, 0, c)),
                  pl.BlockSpec((kw, LANES), lambda b, c: (0, c))],
        out_specs=pl.BlockSpec((1, t, LANES), lambda b, c: (b, 0, c)),
        out_shape=jax.ShapeDtypeStruct((bn, t, nq), BF16),
        scratch_shapes=[pltpu.VMEM((t + 2 * pad, LANES), F32)],
        compiler_params=_cp(("arbitrary", "arbitrary"), VMEM_LIMIT),
        name="delta_conv",
    )(p, conv_w.astype(F32))


def _merge_masks(ii, jj, lower):
    masks = []
    s = 1
    while s < A_CHUNK:
        grp = (ii // (2 * s)) == (jj // (2 * s))
        odd_i, odd_j = (ii // s) % 2 == 1, (jj // s) % 2 == 1
        masks.append(grp & odd_i & ~odd_j if lower else grp & ~odd_i & odd_j)
        s *= 2
    return masks


def _delta_prep_kernel(q_ref, k_ref, v_ref, gc_ref, gct_ref, beta_ref,
                       uf, ub, wf, wb, qdf, qdb, qkf, qkb, kdtf, kdtb, eg_ref, *, hb):
    hblk = pl.program_id(1)
    ii = _iota((TQ, TQ), 0)
    jj = _iota((TQ, TQ), 1)
    same = (ii // A_CHUNK) == (jj // A_CHUNK)
    eye = (ii == jj).astype(F32)
    merge = (_merge_masks(ii, jj, True), _merge_masks(ii, jj, False))
    tri = ((same & (ii >= jj), same & (ii > jj), (ii // A_CHUNK) * A_CHUNK + (A_CHUNK - 1)),
           (same & (ii <= jj), same & (ii < jj), (ii // A_CHUNK) * A_CHUNK))
    lane_d = _iota((TQ, 2 * A_HEADS), 1)
    gc_all = gc_ref[0]
    beta_all = beta_ref[0]
    outs = ((uf, wf, qdf, qkf, kdtf), (ub, wb, qdb, qkb, kdtb))
    heads = []
    for hh in range(hb):
        sl = slice(hh * LANES, (hh + 1) * LANES)
        qb, kb, vb = q_ref[0, :, sl], k_ref[0, :, sl], v_ref[0, :, sl]
        heads.append((sl, qb.astype(F32), kb.astype(F32), vb.astype(F32), _dot_nt(kb, kb), _dot_nt(qb, kb)))
    probs = []
    for hh, (sl, q, k, v, kk, qk) in enumerate(heads):
        for d in range(2):
            incl, strict, last = tri[d]
            idx = d * A_HEADS + hblk * hb + hh
            gcc = jnp.sum(jnp.where(lane_d == idx, gc_all, 0.0), axis=1, keepdims=True)
            bet = jnp.sum(jnp.where(lane_d == idx, beta_all, 0.0), axis=1, keepdims=True)
            gcr = gct_ref[0, pl.ds(idx, 1), :]
            dm = jnp.exp(jnp.where(incl, gcc - gcr, NEG))
            glast = jnp.sum(jnp.where(jj == last, gcr, 0.0), axis=1, keepdims=True)
            a = jnp.where(strict, bet * kk * dm, 0.0)
            probs.append((hh, d, gcc, bet, dm, glast, a))
    ts = [eye - jnp.where(merge[p[1]][0], p[6], 0.0) for p in probs]
    for lvl in range(1, len(merge[0])):
        tbs = [t.astype(BF16) for t in ts]
        ys = [_dot(tb, jnp.where(merge[p[1]][lvl], p[6], 0.0).astype(BF16)) for tb, p in zip(tbs, probs)]
        xs = [_dot(y.astype(BF16), tb) for y, tb in zip(ys, tbs)]
        ts = [t - x for t, x in zip(ts, xs)]
    sols = []
    for t, (hh, d, gcc, bet, dm, glast, a) in zip(ts, probs):
        _, q, k, v, kk, qk = heads[hh]
        rhs = jnp.concatenate([v * bet, k * (bet * jnp.exp(gcc))], axis=1).astype(BF16)
        sols.append(_dot(t.astype(BF16), rhs))
    for sol, (hh, d, gcc, bet, dm, glast, a) in zip(sols, probs):
        sl, q, k, v, kk, qk = heads[hh]
        u_o, w_o, qd_o, qk_o, kdt_o = outs[d]
        u_o[0, :, sl] = sol[:, :LANES]
        w_o[0, :, sl] = sol[:, LANES:].astype(BF16)
        qd_o[0, :, sl] = (q * jnp.exp(gcc)).astype(BF16)
        qk_o[0, :, sl] = (qk * dm).astype(BF16)
        kdt_o[0, sl, :] = (k * jnp.exp(glast - gcc)).T.astype(BF16)
        eglast = jnp.exp(glast)
        for cch in range(TQ // A_CHUNK):
            row = (d * hb + hh) * (TQ // A_CHUNK) + cch
            eg_ref[0, 0, 0, row:row + 1, :] = jnp.broadcast_to(eglast[cch * A_CHUNK:cch * A_CHUNK + 1], (1, LANES))


def _delta_prep(qkv, gc, gct, beta, hb):
    bn, t, _ = qkv.shape
    nt = t // TQ
    nhb = A_HEADS // hb
    wdt = A_HEADS * A_DV
    tok = lambda off: pl.BlockSpec((1, TQ, hb * LANES), lambda b, h, i: (b, i, off + h))
    nd = 2 * A_HEADS
    tok_shape = lambda dt: jax.ShapeDtypeStruct((bn, t, wdt), dt)
    return pl.pallas_call(
        functools.partial(_delta_prep_kernel, hb=hb),
        grid=(bn, nhb, nt),
        in_specs=[tok(0), tok(nhb), tok(2 * nhb),
                  pl.BlockSpec((1, TQ, nd), lambda b, h, i: (b, i, 0)),
                  pl.BlockSpec((1, nd, TQ), lambda b, h, i: (b, 0, i)),
                  pl.BlockSpec((1, TQ, nd), lambda b, h, i: (b, i, 0))],
        out_specs=[tok(0)] * 8 + [pl.BlockSpec((1, hb * LANES, TQ), lambda b, h, i: (b, h, i))] * 2
                  + [pl.BlockSpec((1, 1, 1, 4 * hb, LANES), lambda b, h, i: (b, h, i, 0, 0))],
        out_shape=[tok_shape(F32), tok_shape(F32)] + [tok_shape(BF16)] * 6
                  + [jax.ShapeDtypeStruct((bn, wdt, t), BF16)] * 2
                  + [jax.ShapeDtypeStruct((bn, nhb, nt, 4 * hb, LANES), F32)],
        compiler_params=_cp(("arbitrary", "arbitrary", "arbitrary"), VMEM_LIMIT),
        name="delta_prep",
    )(qkv, qkv, qkv, gc, gct, beta)


def _delta_scan_kernel(uf, wf, qdf, qkf, kdtf, egf, ub, wb, qdb, qkb, kdtb, egb, of_ref, ob_ref, s_ref, *, hb):
    @pl.when(pl.program_id(1) == 0)
    def _():
        s_ref[...] = jnp.zeros(s_ref.shape, F32)

    zeros = jnp.zeros((A_CHUNK, LANES), BF16)
    dirs = ((uf, wf, qdf, qkf, kdtf, egf, of_ref, (0, 1)), (ub, wb, qdb, qkb, kdtb, egb, ob_ref, (1, 0)))
    chains = [(d, head) for d in range(2) for head in range(A_HEADS)]
    states = [s_ref[d, head] for d, head in chains]
    for step in range(TQ // A_CHUNK):
        ws_all = []
        for (d, head), s in zip(chains, states):
            u, w, qd, qk, kdt, eg, o_ref, order = dirs[d]
            rs = slice(order[step] * A_CHUNK, (order[step] + 1) * A_CHUNK)
            sl = slice(head * LANES, (head + 1) * LANES)
            ws_all.append(_dot(jnp.concatenate([w[0, rs, sl], qd[0, rs, sl]], axis=0), s.astype(BF16)))
        vfulls = []
        for (d, head), ws in zip(chains, ws_all):
            u, w, qd, qk, kdt, eg, o_ref, order = dirs[d]
            cch = order[step]
            rs = slice(cch * A_CHUNK, (cch + 1) * A_CHUNK)
            sl = slice(head * LANES, (head + 1) * LANES)
            vn = (u[0, rs, sl] - ws[:A_CHUNK]).astype(BF16)
            vfull = jnp.concatenate([vn, zeros] if cch == 0 else [zeros, vn], axis=0)
            vfulls.append(vfull)
            o_ref[0, rs, sl] = ws[A_CHUNK:] + _dot(qk[0, rs, sl], vfull)
        new_states = []
        for (d, head), s, vfull in zip(chains, states, vfulls):
            u, w, qd, qk, kdt, eg, o_ref, order = dirs[d]
            hblk, hh = divmod(head, hb)
            row = (d * hb + hh) * 2 + order[step]
            sl = slice(head * LANES, (head + 1) * LANES)
            new_states.append(s * eg[0, hblk, 0, row:row + 1, :] + _dot(kdt[0, sl, :], vfull))
        states = new_states
    for (d, head), s in zip(chains, states):
        s_ref[d, head] = s


def _delta_scan(prep, l, hb):
    uf, ub, wf, wb, qdf, qdb, qkf, qkb, kdtf, kdtb, eg = prep
    bn, t, wdt = uf.shape
    nt = t // TQ
    nct = l // TQ
    nhb = A_HEADS // hb
    fwd = lambda s: s
    bwd = lambda s: jnp.where(s < nct, nct - 1 - s, nt - 1 - (s - nct))
    tok = lambda f: pl.BlockSpec((1, TQ, wdt), lambda b, s: (b, f(s), 0))
    tr = lambda f: pl.BlockSpec((1, wdt, TQ), lambda b, s: (b, 0, f(s)))
    egs = lambda f: pl.BlockSpec((1, nhb, 1, 4 * hb, LANES), lambda b, s: (b, 0, f(s), 0, 0))
    return pl.pallas_call(
        functools.partial(_delta_scan_kernel, hb=hb),
        grid=(bn, nt),
        in_specs=[tok(fwd)] * 4 + [tr(fwd), egs(fwd)] + [tok(bwd)] * 4 + [tr(bwd), egs(bwd)],
        out_specs=[tok(fwd), tok(bwd)],
        out_shape=[jax.ShapeDtypeStruct((bn, t, wdt), F32)] * 2,
        scratch_shapes=[pltpu.VMEM((2, A_HEADS, A_DK, A_DV), F32)],
        compiler_params=_cp(("arbitrary", "arbitrary"), VMEM_LIMIT),
        name="delta_scan",
    )(uf, wf, qdf, qkf, kdtf, eg, ub, wb, qdb, qkb, kdtb, eg)


def _residual_router(y, w_ref, x_ref, mod_ref, gffn_ref, wr_ref, xo_ref, h_ref, aff_ref):
    o = _dot(y, w_ref[...])
    for k in range(x_ref.shape[0]):
        mod = mod_ref[k, 0]
        xn = x_ref[k] + mod[2:3] * _rows(o, k)
        xo_ref[k] = xn
        h = _norm_mod(xn, gffn_ref[...], mod, 3, 4)
        h_hi, h_lo = _split2(h)
        h_ref[k] = h_hi
        lg = _dot(h_hi, wr_ref[0]) + _dot(h_hi, wr_ref[1]) + _dot(h_lo, wr_ref[0])
        e = jnp.exp(lg - jnp.max(lg, axis=-1, keepdims=True))
        aff_ref[k] = _transpose_exact(e / jnp.sum(e, axis=-1, keepdims=True))


def _out_kernel(y_ref, *rest):
    _residual_router(jnp.concatenate([y_ref[k] for k in range(y_ref.shape[0])], axis=0), *rest)


def _out_delta_kernel(of_ref, ob_ref, z_ref, gout_ref, *rest):
    rows = []
    for k in range(of_ref.shape[0]):
        o = of_ref[k] + ob_ref[k]
        parts = []
        for hd in range(A_HEADS):
            sl = slice(hd * A_DV, (hd + 1) * A_DV)
            oh = o[:, sl]
            ms = jnp.mean(oh * oh, axis=-1, keepdims=True)
            parts.append((oh * lax.rsqrt(ms + EPS) * gout_ref[...] * _silu(z_ref[k, :, sl].astype(F32))).astype(BF16))
        rows.append(jnp.concatenate(parts, axis=1))
    _residual_router(jnp.concatenate(rows, axis=0), *rest)


def _out_proj(pre, w_out, xs, mod, g_ffn, wr, nct_m, delta):
    bn, t, d = xs.shape
    k = w_out.shape[0]
    ne = wr.shape[2]
    nb = _batch_group(bn)
    tok = lambda width: pl.BlockSpec((nb, TM, width), lambda b, i: (b, i, 0))
    full = lambda shape: pl.BlockSpec(shape, lambda b, i: (0,) * len(shape))
    if delta:
        of, ob, z, gout = pre
        head_specs = [tok(k), tok(k), tok(k), full((1, A_DV))]
        head_args = (of, ob, z, gout.reshape(1, A_DV).astype(F32))
        body = _out_delta_kernel
    else:
        head_specs = [tok(k)]
        head_args = (pre,)
        body = _out_kernel
    return pl.pallas_call(
        body,
        grid=(bn // nb, t // TM),
        in_specs=head_specs + [full((k, d)), tok(d),
                               pl.BlockSpec((nb, 1, 6, d), lambda b, i: (b, _seg(i, nct_m), 0, 0)),
                               full((1, d)), full((2, d, ne))],
        out_specs=[tok(d), tok(d), pl.BlockSpec((nb, ne, TM), lambda b, i: (b, 0, i))],
        out_shape=[jax.ShapeDtypeStruct((bn, t, d), F32), jax.ShapeDtypeStruct((bn, t, d), BF16),
                   jax.ShapeDtypeStruct((bn, ne, t), F32)],
        compiler_params=_cp(("arbitrary", "arbitrary"), VMEM_LIMIT),
        name="out_delta" if delta else "out_proj",
    )(*head_args, w_out, xs, mod, g_ffn, wr)


def _kth_largest_bits(bits, k):
    def body(it, thr):
        cand = thr | jnp.left_shift(jnp.int32(1), 30 - it)
        cnt = jnp.sum((bits >= cand).astype(I32), axis=1, keepdims=True)
        return jnp.where(cnt >= k, cand, thr)
    return lax.fori_loop(0, 31, body, jnp.zeros((bits.shape[0], 1), I32))


def _select_kernel(aff_ref, posd_ref, offs_ref, gated_ref, *, l, t, cap_c, cap_x):
    a = aff_ref[0]
    ne = a.shape[0]
    bits = pltpu.bitcast(a, I32)
    upper = (_iota((LANES, LANES), 0) <= _iota((LANES, LANES), 1)).astype(BF16)
    nt = t // LANES
    sel = [None] * nt
    for s0, s1, cap in ((0, l, cap_c), (l, t, cap_x)):
        bseg = bits[:, s0:s1]
        thr = _kth_largest_bits(bseg, cap)
        gtf = jnp.where(bseg > thr, 1.0, 0.0)
        eqf = jnp.where(bseg == thr, 1.0, 0.0)
        need = cap - jnp.sum(gtf, axis=1, keepdims=True)
        run = jnp.zeros((ne, 1), F32)
        for j in range((s1 - s0) // LANES):
            ej = eqf[:, j * LANES:(j + 1) * LANES]
            inc = _dot(ej.astype(BF16), upper)
            keep = jnp.where(inc - ej + run < need, ej, 0.0)
            sel[s0 // LANES + j] = jnp.maximum(gtf[:, j * LANES:(j + 1) * LANES], keep)
            run = run + inc[:, LANES - 1:LANES]
    run = jnp.zeros((ne, 1), F32)
    offs = jnp.zeros((ne, LANES), I32)
    lane = _iota((ne, LANES), 1)
    for j in range(nt):
        sj = sel[j]
        inc = _dot(sj.astype(BF16), upper)
        pos = jnp.where(sj > 0.0, inc - sj + run, -1.0)
        posd_ref[0, j] = pos.astype(I32)
        offs = jnp.where(lane == j, run.astype(I32), offs)
        gated_ref[0, j] = jnp.where(sj > 0.0, a[:, j * LANES:(j + 1) * LANES], 0.0)
        run = run + inc[:, LANES - 1:LANES]
    offs_ref[0] = jnp.where(lane == nt, run.astype(I32), offs)


def _select(aff, l, cap_c, cap_x):
    bn, ne, t = aff.shape
    nt = t // LANES
    return pl.pallas_call(
        functools.partial(_select_kernel, l=l, t=t, cap_c=cap_c, cap_x=cap_x),
        grid=(bn,),
        in_specs=[pl.BlockSpec((1, ne, t), lambda b: (b, 0, 0))],
        out_specs=[pl.BlockSpec((1, nt, ne, LANES), lambda b: (b, 0, 0, 0)),
                   pl.BlockSpec((1, ne, LANES), lambda b: (b, 0, 0)),
                   pl.BlockSpec((1, nt, ne, LANES), lambda b: (b, 0, 0, 0))],
        out_shape=[jax.ShapeDtypeStruct((bn, nt, ne, LANES), I32), jax.ShapeDtypeStruct((bn, ne, LANES), I32),
                   jax.ShapeDtypeStruct((bn, nt, ne, LANES), F32)],
        compiler_params=_cp(("arbitrary",), VMEM_LIMIT),
        name="route_select",
    )(aff)


GATHER_TOK = 2 * TQ
GATHER_WIN = 64
EXPERT_GROUP = 4
COMBINE_TOK = 2 * TQ
COMBINE_WIN = COMBINE_TOK + 16
PACK_WIN = 64
EXPERT_VMEM_LIMIT = 60 * 1024 * 1024


def _gather_group(offs_ref, h_ref, pos_ref, gate_ref, xg_ref, gr_ref, b, e0, ne, nt):
    xg_ref[...] = jnp.zeros(xg_ref.shape, BF16)
    gr_ref[...] = jnp.zeros(gr_ref.shape, F32)
    tpt = GATHER_TOK // LANES
    riota = _iota((GATHER_WIN, GATHER_TOK), 0)
    stacked = (EXPERT_GROUP * GATHER_WIN, ne)
    pick = _iota(stacked, 1) == e0 + _iota(stacked, 0) // GATHER_WIN

    def bounds(j, k):
        base = (b * ne + e0 + k) * LANES
        aoff = (offs_ref[base + j * tpt] // 16) * 16
        return aoff, (offs_ref[base + (j + 1) * tpt] - aoff + GATHER_WIN - 1) // GATHER_WIN

    def add_windows(j, w):
        tok = pl.ds(pl.multiple_of(j * GATHER_TOK, GATHER_TOK), GATHER_TOK)
        starts, hots = [], []
        for k in range(EXPERT_GROUP):
            aoff, nwin = bounds(j, k)
            live = w < jnp.maximum(nwin, 1)
            pos = jnp.concatenate([pos_ref[0, j * tpt + c, pl.ds(e0 + k, 1), :] for c in range(tpt)], axis=1)
            first = jnp.where(live, aoff + w * GATHER_WIN, -(1 << 20))
            hots.append(jnp.where(riota == pos - first, 1.0, 0.0).astype(BF16))
            starts.append(pl.multiple_of(aoff + jnp.where(live, w, 0) * GATHER_WIN, 16))
        lhs = jnp.concatenate(hots, axis=0)
        rows = _dot(lhs, h_ref[0, tok, :])
        gd = jnp.concatenate([gate_ref[0, j * tpt + c] for c in range(tpt)], axis=1)
        g1 = gd.astype(BF16)
        r1 = gd - g1.astype(F32)
        g2 = r1.astype(BF16)
        g3 = (r1 - g2.astype(F32)).astype(BF16)
        gall = _dot_nt(lhs, g1) + _dot_nt(lhs, g2) + _dot_nt(lhs, g3)
        gcol = jnp.sum(jnp.where(pick, gall, 0.0), axis=1, keepdims=True)
        for k in range(EXPERT_GROUP):
            rs = slice(k * GATHER_WIN, (k + 1) * GATHER_WIN)
            dst = pl.ds(starts[k], GATHER_WIN)
            xg_ref[k, dst, :] = (xg_ref[k, dst, :].astype(F32) + rows[rs]).astype(BF16)
            gr_ref[k, dst, :] += jnp.broadcast_to(gcol[rs], (GATHER_WIN, LANES))

    def tile(j, carry):
        add_windows(j, 0)
        most = bounds(j, 0)[1]
        for k in range(1, EXPERT_GROUP):
            most = jnp.maximum(most, bounds(j, k)[1])

        def extra(w, carry2):
            add_windows(j, w)
            return carry2

        lax.fori_loop(1, most, extra, 0)
        return carry

    lax.fori_loop(0, nt // tpt, tile, 0)


def _expert_kernel(offs_ref, h_ref, pos_ref, gate_ref, wgu_ref, wd_ref, y_ref, xg_ref, gr_ref, acc_ref, *, nt, r, fc):
    b = pl.program_id(0)
    e = pl.program_id(1)
    slot = e % EXPERT_GROUP

    @pl.when(slot == 0)
    def _():
        _gather_group(offs_ref, h_ref, pos_ref, gate_ref, xg_ref, gr_ref, b, e, pl.num_programs(1), nt)

    xg = xg_ref[slot, pl.ds(0, r), :]
    f = wd_ref.shape[1]
    for c in range(f // fc):
        g = _dot(xg, wgu_ref[0, :, c * fc:(c + 1) * fc])
        u = _dot(xg, wgu_ref[0, :, f + c * fc:f + (c + 1) * fc])
        part = _dot((_silu(g) * u).astype(BF16), wd_ref[0, c * fc:(c + 1) * fc, :])
        if c == 0:
            acc_ref[...] = part
        else:
            acc_ref[...] += part
    y_ref[0, 0] = (acc_ref[...] * gr_ref[slot, pl.ds(0, r), :][:, 0:1]).astype(BF16)


def _experts(offs, hf, posd, gated, wgu, wd, r):
    bn, t, d = hf.shape
    ne, _, f2 = wgu.shape
    nt = t // TQ
    rows = r + GATHER_WIN
    assert ne % EXPERT_GROUP == 0 and rows % 16 == 0
    return pl.pallas_call(
        functools.partial(_expert_kernel, nt=nt, r=r, fc=min(256, f2 // 2)),
        grid_spec=pltpu.PrefetchScalarGridSpec(
            num_scalar_prefetch=1,
            grid=(bn, ne),
            in_specs=[pl.BlockSpec((1, t, d), lambda b, e, o: (b, 0, 0), pipeline_mode=pl.Buffered(1)),
                      pl.BlockSpec((1, nt, ne, LANES), lambda b, e, o: (b, 0, 0, 0)),
                      pl.BlockSpec((1, nt, ne, LANES), lambda b, e, o: (b, 0, 0, 0)),
                      pl.BlockSpec((1, d, f2), lambda b, e, o: (e, 0, 0)),
                      pl.BlockSpec((1, f2 // 2, d), lambda b, e, o: (e, 0, 0))],
            out_specs=pl.BlockSpec((1, 1, r, d), lambda b, e, o: (b, e, 0, 0)),
            scratch_shapes=[pltpu.VMEM((EXPERT_GROUP, rows, d), BF16), pltpu.VMEM((EXPERT_GROUP, rows, LANES), F32),
                            pltpu.VMEM((r, d), F32)]),
        out_shape=jax.ShapeDtypeStruct((bn, ne, r, d), BF16),
        compiler_params=_cp(("arbitrary", "arbitrary"), EXPERT_VMEM_LIMIT),
        name="experts",
    )(offs, hf, posd, gated, wgu, wd)


def _combine_kernel(offs_ref, y_ref, pos_ref, x_ref, mod_ref, o_ref, *, r, win, pack):
    b = pl.program_id(0)
    j = pl.program_id(1)
    ne = y_ref.shape[1]
    tpt = COMBINE_TOK // LANES
    ident = (_iota((LANES, LANES), 0) == _iota((LANES, LANES), 1)).astype(BF16)
    cols = []
    for c in range(tpt):
        p_hi, p_lo = _split2(pos_ref[0, c].astype(F32))
        cols.append((_dot_nt(ident, p_hi) + _dot_nt(ident, p_lo)).astype(I32))
    pc = jnp.concatenate(cols, axis=0)
    offs = [offs_ref[(b * ne + e) * LANES + j * tpt] for e in range(ne)]
    ends = [offs_ref[(b * ne + e) * LANES + (j + 1) * tpt] for e in range(ne)]
    los = [jnp.minimum((offs[e] // 16) * 16, r - pack) for e in range(ne)]
    fits = ends[0] - los[0] <= pack
    for e in range(1, ne):
        fits = fits & (ends[e] - los[e] <= pack)
    gate_mod = mod_ref[0, 0][5:6]

    @pl.when(fits)
    def _():
        lane_e = _iota((1, ne), 1)
        lo_row = jnp.zeros((1, ne), I32)
        for e in range(ne):
            lo_row = jnp.where(lane_e == e, los[e], lo_row)
        rel = pc - lo_row
        rel = jnp.where((pc >= 0) & (rel >= 0) & (rel < pack), rel, -1).astype(F32).astype(BF16)
        spread = (_iota((ne, ne * pack), 0) == _iota((ne, ne * pack), 1) // pack).astype(BF16)
        want = (_iota((COMBINE_TOK, ne * pack), 1) % pack).astype(F32)
        onehot = jnp.where(_dot(rel, spread) == want, 1.0, 0.0).astype(BF16)
        ycat = jnp.concatenate([y_ref[0, e, pl.ds(pl.multiple_of(los[e], 16), pack), :] for e in range(ne)], axis=0)
        o_ref[0] = x_ref[0] + gate_mod * _dot(onehot, ycat)

    @pl.when(jnp.logical_not(fits))
    def _():
        liota = _iota((COMBINE_TOK, win), 1)
        acc = jnp.zeros(x_ref.shape[1:], F32)
        for e in range(ne):
            aoff = pl.multiple_of(jnp.minimum((offs[e] // 16) * 16, r - win), 16)
            onehot = jnp.where(liota == pc[:, e:e + 1] - aoff, 1.0, 0.0).astype(BF16)
            acc = acc + _dot(onehot, y_ref[0, e, pl.ds(aoff, win), :])
        o_ref[0] = x_ref[0] + gate_mod * acc


def _combine(offs, y, posd, xs, mod, l, latent_only):
    bn, t, d = xs.shape
    ne, r = y.shape[1], y.shape[2]
    nct = l // COMBINE_TOK
    tpt = COMBINE_TOK // LANES
    win = min(COMBINE_WIN, r)
    pack = min(PACK_WIN, r)
    assert (r - win) % 16 == 0 and (r - pack) % 16 == 0 and l % COMBINE_TOK == 0 and t % COMBINE_TOK == 0
    tok = pl.BlockSpec((1, COMBINE_TOK, d), lambda b, j, o: (b, j, 0))
    if latent_only:
        out_spec = pl.BlockSpec((1, COMBINE_TOK, d), lambda b, j, o: (b, jnp.maximum(j - nct, 0), 0))
        out_rows = t - l
    else:
        out_spec, out_rows = tok, t
    return pl.pallas_call(
        functools.partial(_combine_kernel, r=r, win=win, pack=pack),
        grid_spec=pltpu.PrefetchScalarGridSpec(
            num_scalar_prefetch=1,
            grid=(bn, t // COMBINE_TOK),
            in_specs=[pl.BlockSpec((1, ne, r, d), lambda b, j, o: (b, 0, 0, 0), pipeline_mode=pl.Buffered(1)),
                      pl.BlockSpec((1, tpt, ne, LANES), lambda b, j, o: (b, j, 0, 0)),
                      tok,
                      pl.BlockSpec((1, 1, 6, d), lambda b, j, o: (b, _seg(j, nct), 0, 0))],
            out_specs=out_spec),
        out_shape=jax.ShapeDtypeStruct((bn, out_rows, d), F32),
        compiler_params=_cp(("arbitrary", "arbitrary"), VMEM_LIMIT),
        name="moe_combine",
    )(offs, y, posd, xs, mod)


def _moe(xs, hf, aff, mod, wgu, wd, l, latent_only):
    bn, t, _ = xs.shape
    cap_c = EC_CAPACITY * l // N_EXPERTS
    cap_x = EC_CAPACITY * (t - l) // N_EXPERTS
    posd, offs, gated = _select(aff, l, cap_c, cap_x)
    offs = offs.reshape(-1)
    y = _experts(offs, hf, posd, gated, wgu, wd, cap_c + cap_x)
    return _combine(offs, y, posd, xs, mod, l, latent_only)


def _hi_lo(w):
    hi = w.astype(BF16)
    return jnp.stack([hi, (w - hi.astype(F32)).astype(BF16)])


def _mixer_delta(xs, mod, g_mix, w_in, conv_w, a_log, dt_bias, l, hb=8):
    nqkvz = 2 * A_HEADS * A_DK + 2 * A_HEADS * A_DV
    wab = w_in[:, nqkvz:]
    p, z, gc, gct, beta = _proj_delta(xs, mod, g_mix, w_in[:, :nqkvz].astype(BF16), _hi_lo(wab), a_log, dt_bias, l // TM)
    qkv = _delta_conv(p, conv_w, l)
    of, ob = _delta_scan(_delta_prep(qkv, gc, gct, beta, hb), l, hb)
    return of, ob, z


def _mixer_swa(xs, mod, g_mix, w_in, qn, kn, sink, cosf, sinf, l):
    nq = B_QHEADS * B_HD
    nk = B_KVHEADS * B_HD
    dup = lambda w: jnp.concatenate([w.reshape(-1, B_KVHEADS, 1, B_HD)] * 2, axis=2).reshape(-1, 2 * nk)
    w = jnp.concatenate([w_in[:, :nq], dup(w_in[:, nq:nq + nk]), dup(w_in[:, nq + nk:])], axis=1).astype(BF16)
    gain = jnp.concatenate([jnp.tile(qn, B_QHEADS) * (B_HD ** -0.5), jnp.tile(kn, 2 * B_KVHEADS)])[None].astype(F32)
    qk, v2 = _proj_qk(xs, mod, g_mix, w, gain, cosf, sinf, nq + 2 * nk, l // TM)
    return _swa_attention(qk, v2, sink, l)


def _mixer_diff(xs, mod, g_mix, w_in, qn, kn, lam_vecs, g_sub, cosf, sinf, l, lam_init):
    nqk = C_HEADS * 2 * C_HD
    gain = jnp.concatenate([jnp.tile(qn, 2 * C_HEADS) * (C_HD ** -0.5 * LOG2E),
                            jnp.tile(kn, 2 * C_HEADS)])[None].astype(F32)
    qk, v = _proj_qk(xs, mod, g_mix, w_in.astype(BF16), gain, cosf, sinf, 2 * nqk, l // TM)
    return _diff_attention(qk, v, lam_vecs, g_sub, l, lam_init)


def kernel(x, c, ctx, c_ctx, w_ada, b_ada, g_mix, g_ffn, a_w_in, a_conv, a_log, a_dt_bias, a_g_out, a_w_out,
           b_w_in, b_q_norm, b_k_norm, b_sink, b_w_out, c_w_in, c_q_norm, c_k_norm, c_lambda, c_g_sub, c_w_out,
           w_router, w_gate_up, w_down):
    depth = w_ada.shape[0]
    n = x.shape[1]
    l = ctx.shape[1]
    assert l % TM == 0 and n % TM == 0
    xs = jnp.concatenate([ctx, x], axis=1)
    mods = _adaln(c, c_ctx, w_ada, b_ada)
    cos_b, sin_b = _rope_tables(n, l, B_HD)
    cos_c, sin_c = _rope_tables(n, l, C_HD)
    nct_m = l // TM
    for layer in range(depth):
        kind, j = layer % N_MIXERS, layer // N_MIXERS
        mod = mods[layer]
        gm = g_mix[layer][None].astype(F32)
        gf = g_ffn[layer][None].astype(F32)
        wr = _hi_lo(w_router[layer])
        if kind == 0:
            pre = _mixer_delta(xs, mod, gm, a_w_in[j], a_conv[j], a_log[j], a_dt_bias[j], l)
            xs, hf, aff = _out_proj(pre + (a_g_out[j],), a_w_out[j].astype(BF16), xs, mod, gf, wr, nct_m, True)
        elif kind == 1:
            pre = _mixer_swa(xs, mod, gm, b_w_in[j], b_q_norm[j], b_k_norm[j], b_sink[j], cos_b, sin_b, l)
            xs, hf, aff = _out_proj(pre, b_w_out[j].astype(BF16), xs, mod, gf, wr, nct_m, False)
        else:
            lam_init = 0.8 - 0.6 * math.exp(-0.3 * layer)
            pre = _mixer_diff(xs, mod, gm, c_w_in[j], c_q_norm[j], c_k_norm[j], c_lambda[j], c_g_sub[j],
                              cos_c, sin_c, l, lam_init)
            xs, hf, aff = _out_proj(pre, c_w_out[j].astype(BF16), xs, mod, gf, wr, nct_m, False)
        xs = _moe(xs, hf, aff, mod, w_gate_up[layer].astype(BF16), w_down[layer].astype(BF16), l, layer == depth - 1)
    return xs
```

```python
import functools
import math

import jax
import jax.numpy as jnp
from jax import lax
from jax.experimental import pallas as pl
from jax.experimental.pallas import tpu as pltpu

F32 = jnp.float32
BF16 = jnp.bfloat16
I32 = jnp.int32

EPS = 1e-6
ROPE_BASE = 10000.0
GRID_W = 64
N_MIXERS = 3

A_HEADS, A_DK, A_DV, A_CHUNK = 8, 128, 128, 64
B_QHEADS, B_KVHEADS, B_HD = 16, 4, 64
B_GROUP = B_QHEADS // B_KVHEADS
C_HEADS, C_HD = 8, 64
N_EXPERTS, EC_CAPACITY = 16, 2

LANES = 128
TM = 256
TQ = 128
NEG = -1e30
VMEM_LIMIT = 56 * 1024 * 1024


def _cp(sem, vmem=None):
    return pltpu.CompilerParams(dimension_semantics=sem, vmem_limit_bytes=vmem)


def _dot(a, b):
    return jnp.dot(a, b, preferred_element_type=F32)


def _dot_nt(a, b):
    return lax.dot_general(a, b, (((1,), (1,)), ((), ())), preferred_element_type=F32)


def _split2(a):
    hi = a.astype(BF16)
    return hi, (a - hi.astype(F32)).astype(BF16)


def _sigmoid(x):
    return 1.0 / (1.0 + jnp.exp(-x))


def _silu(x):
    return x * _sigmoid(x)


def _softplus(x):
    return jnp.maximum(x, 0.0) + jnp.log(1.0 + jnp.exp(-jnp.abs(x)))


def _norm_mod(x, g, mod, i_shift, i_scale):
    ms = jnp.mean(x * x, axis=-1, keepdims=True)
    y = x * lax.rsqrt(ms + EPS) * g
    return y * (1.0 + mod[i_scale:i_scale + 1]) + mod[i_shift:i_shift + 1]


def _iota(shape, dim):
    return lax.broadcasted_iota(I32, shape, dim)


def _ada_kernel(s_ref, w_ref, b_ref, o_ref):
    s = _silu(s_ref[...])
    s_hi, s_lo = _split2(s)
    w_hi, w_lo = _split2(w_ref[0])
    o_ref[0] = _dot(s_hi, w_hi) + _dot(s_hi, w_lo) + _dot(s_lo, w_hi) + b_ref[0]


def _adaln(c, c_ctx, w_ada, b_ada):
    depth, d, d6 = w_ada.shape
    bn = c.shape[0]
    rows_n = -(-(bn + 1) // 8) * 8
    rows = jnp.zeros((rows_n, d), F32).at[:bn].set(c).at[bn].set(c_ctx)
    nb = d6 // 4
    out = pl.pallas_call(
        _ada_kernel,
        grid=(depth, d6 // nb),
        in_specs=[pl.BlockSpec((rows_n, d), lambda l, j: (0, 0)),
                  pl.BlockSpec((1, d, nb), lambda l, j: (l, 0, j)),
                  pl.BlockSpec((1, 1, nb), lambda l, j: (l, 0, j))],
        out_specs=pl.BlockSpec((1, rows_n, nb), lambda l, j: (l, 0, j)),
        out_shape=jax.ShapeDtypeStruct((depth, rows_n, d6), F32),
        compiler_params=_cp(("arbitrary", "arbitrary"), VMEM_LIMIT),
        name="adaln",
    )(rows, w_ada, b_ada.reshape(depth, 1, d6))
    mx = out[:, :bn].reshape(depth, bn, 6, d)
    mc = jnp.broadcast_to(out[:, bn].reshape(depth, 1, 6, d), (depth, bn, 6, d))
    return jnp.stack([mc, mx], axis=2)


def _seg(i, nct):
    return jnp.where(i < nct, 0, 1)


COL_CHUNK = 1024


def _batch_group(bn):
    return 4 if bn % 4 == 0 else (2 if bn % 2 == 0 else 1)


def _rows(a, k):
    return a[k * TM:(k + 1) * TM]


def _proj_qk_kernel(x_ref, mod_ref, g_ref, w_ref, gain_ref, cos_ref, sin_ref, qk_ref, v_ref, *, nqk):
    nb = x_ref.shape[0]
    h = jnp.concatenate([_norm_mod(x_ref[k], g_ref[...], mod_ref[k, 0], 0, 1).astype(BF16) for k in range(nb)], axis=0)
    wide = 2 * LANES
    grp = (_iota((wide, wide), 0) // 64 == _iota((wide, wide), 1) // 64).astype(BF16)
    first = (_iota((1, wide), 1) % 64) < 32
    cs = jnp.concatenate([jnp.concatenate([cos_ref[...]] * 2, axis=1)] * nb, axis=0)
    sn = jnp.concatenate([jnp.concatenate([sin_ref[...]] * 2, axis=1)] * nb, axis=0)
    nout = w_ref.shape[1]
    for c0 in range(0, nout, COL_CHUNK):
        p = _dot(h, w_ref[:, c0:min(c0 + COL_CHUNK, nout)])
        for t0 in range(c0, min(c0 + COL_CHUNK, nout), wide):
            xt = p[:, t0 - c0:t0 - c0 + wide]
            if t0 < nqk:
                ms = _dot((xt * xt).astype(BF16), grp) * (1.0 / 64)
                y = xt * lax.rsqrt(ms + EPS) * gain_ref[:, t0:t0 + wide]
                rot = jnp.where(first, pltpu.roll(y, wide - 32, 1), pltpu.roll(y, 32, 1))
                out = (y * cs + rot * sn).astype(BF16)
                for k in range(nb):
                    qk_ref[k, :, t0:t0 + wide] = _rows(out, k)
            else:
                for k in range(nb):
                    v_ref[k, :, t0 - nqk:t0 - nqk + wide] = _rows(xt, k).astype(BF16)


def _proj_qk(xs, mod, g, w, gain, cosf, sinf, nqk, nct_m):
    bn, t, d = xs.shape
    nout = w.shape[1]
    nb = _batch_group(bn)
    return pl.pallas_call(
        functools.partial(_proj_qk_kernel, nqk=nqk),
        grid=(bn // nb, t // TM),
        in_specs=[pl.BlockSpec((nb, TM, d), lambda b, i: (b, i, 0)),
                  pl.BlockSpec((nb, 1, 6, d), lambda b, i: (b, _seg(i, nct_m), 0, 0)),
                  pl.BlockSpec((1, d), lambda b, i: (0, 0)),
                  pl.BlockSpec((d, nout), lambda b, i: (0, 0)),
                  pl.BlockSpec((1, nqk), lambda b, i: (0, 0)),
                  pl.BlockSpec((TM, LANES), lambda b, i: (i, 0)),
                  pl.BlockSpec((TM, LANES), lambda b, i: (i, 0))],
        out_specs=[pl.BlockSpec((nb, TM, nqk), lambda b, i: (b, i, 0)),
                   pl.BlockSpec((nb, TM, nout - nqk), lambda b, i: (b, i, 0))],
        out_shape=[jax.ShapeDtypeStruct((bn, t, nqk), BF16),
                   jax.ShapeDtypeStruct((bn, t, nout - nqk), BF16)],
        compiler_params=_cp(("arbitrary", "arbitrary"), VMEM_LIMIT),
        name="proj_qk",
    )(xs, mod, g, w, gain, cosf, sinf)


def _rope_tables(n, l, head_dim):
    t = jnp.arange(n)
    n_freq = head_dim // 4
    inv = ROPE_BASE ** (-jnp.arange(n_freq, dtype=F32) / n_freq)
    ang = jnp.concatenate([(t // GRID_W).astype(F32)[:, None] * inv, (t % GRID_W).astype(F32)[:, None] * inv], -1)
    cs, sn = jnp.cos(ang), jnp.sin(ang)
    reps = LANES // head_dim
    cosf = jnp.tile(jnp.concatenate([cs, cs], -1), (1, reps))
    sinf = jnp.tile(jnp.concatenate([-sn, sn], -1), (1, reps))
    cosf = jnp.concatenate([jnp.ones((l, LANES), F32), cosf], 0)
    sinf = jnp.concatenate([jnp.zeros((l, LANES), F32), sinf], 0)
    return cosf, sinf


def _swa_kernel(q_ref, kp_ref, kc_ref, kn_ref, kx_ref, vp_ref, vc_ref, vn_ref, vx_ref, sink_ref, bias_ref, o_ref,
                *, nct, ntx, l):
    xi = pl.program_id(1) - nct
    lo = _iota((TQ, LANES), 1) < 64
    zero = jnp.zeros((TQ, LANES), BF16)
    c = _iota((1, 3 * TQ + l), 1)
    dead = (((c < TQ) & (xi < 1)) | ((c >= 2 * TQ) & (c < 3 * TQ) & (xi + 1 >= ntx)) | ((c < 3 * TQ) & (xi < 0)))
    bias = bias_ref[...] + jnp.where(dead, NEG, 0.0)
    bias4 = jnp.concatenate([bias] * B_GROUP, axis=0)
    gw = B_GROUP * B_HD
    scores = []
    for kv in range(B_KVHEADS):
        ks = slice(kv * LANES, (kv + 1) * LANES)
        qa, qb = q_ref[0, :, kv * gw:kv * gw + LANES], q_ref[0, :, kv * gw + LANES:(kv + 1) * gw]
        q4 = jnp.concatenate([jnp.where(lo, qa, zero), jnp.where(lo, zero, qa),
                              jnp.where(lo, qb, zero), jnp.where(lo, zero, qb)], axis=0)
        kcat = jnp.concatenate([kp_ref[0, :, ks], kc_ref[0, :, ks], kn_ref[0, :, ks], kx_ref[0, :, ks]], axis=0)
        scores.append(_dot_nt(q4, kcat) + bias4)
    probs = []
    for kv, s in enumerate(scores):
        sk = sink_ref[kv]
        m = jnp.maximum(jnp.max(s, axis=-1, keepdims=True), sk)
        p = jnp.exp(s - m)
        probs.append((p.astype(BF16), jnp.sum(p, axis=-1, keepdims=True) + jnp.exp(sk - m)))
    for kv, (p, den) in enumerate(probs):
        ks = slice(kv * LANES, (kv + 1) * LANES)
        vcat = jnp.concatenate([vp_ref[0, :, ks], vc_ref[0, :, ks], vn_ref[0, :, ks], vx_ref[0, :, ks]], axis=0)
        o4 = _dot(p, vcat) / den
        oa = jnp.where(lo, o4[0:TQ], o4[TQ:2 * TQ])
        ob = jnp.where(lo, o4[2 * TQ:3 * TQ], o4[3 * TQ:4 * TQ])
        o_ref[0, :, kv * gw:(kv + 1) * gw] = jnp.concatenate([oa, ob], axis=1).astype(BF16)


def _swa_attention(qk, v2, sink, l):
    bn, t, _ = qk.shape
    nt = t // TQ
    nct = l // TQ
    nq = B_QHEADS * B_HD
    kw = B_KVHEADS * LANES
    prev = lambda i: jnp.maximum(i - 1, 0)
    nxt = lambda i: jnp.minimum(i + 1, nt - 1)
    cur = lambda i: i
    kspec = lambda f: pl.BlockSpec((1, TQ, kw), lambda b, i: (b, f(i), nq // kw))
    vspec = lambda f: pl.BlockSpec((1, TQ, kw), lambda b, i: (b, f(i), 0))
    sinkcol = jnp.repeat(sink.reshape(B_KVHEADS, B_GROUP), TQ, axis=1).reshape(B_KVHEADS, B_GROUP * TQ, 1).astype(F32)
    r = jnp.arange(TQ)[:, None]
    c = jnp.arange(3 * TQ + l)[None, :]
    band = ((c < TQ) & (c >= r)) | ((c >= TQ) & (c < 2 * TQ)) | ((c >= 2 * TQ) & (c - 2 * TQ <= r)) | (c >= 3 * TQ)
    bias = jnp.where(band, 0.0, NEG).astype(F32)
    return pl.pallas_call(
        functools.partial(_swa_kernel, nct=nct, ntx=nt - nct, l=l),
        grid=(bn, nt),
        in_specs=[pl.BlockSpec((1, TQ, nq), lambda b, i: (b, i, 0)),
                  kspec(prev), kspec(cur), kspec(nxt),
                  pl.BlockSpec((1, l, kw), lambda b, i: (b, 0, nq // kw)),
                  vspec(prev), vspec(cur), vspec(nxt),
                  pl.BlockSpec((1, l, kw), lambda b, i: (b, 0, 0)),
                  pl.BlockSpec((B_KVHEADS, B_GROUP * TQ, 1), lambda b, i: (0, 0, 0)),
                  pl.BlockSpec((TQ, 3 * TQ + l), lambda b, i: (0, 0))],
        out_specs=pl.BlockSpec((1, TQ, nq), lambda b, i: (b, i, 0)),
        out_shape=jax.ShapeDtypeStruct((bn, t, nq), BF16),
        compiler_params=_cp(("arbitrary", "arbitrary"), VMEM_LIMIT),
        name="swa_attention",
    )(qk, qk, qk, qk, qk, v2, v2, v2, v2, sinkcol, bias)


LOG2E = 1.4426950408889634
SAFE_BOUND = 60.0


def _diff_kernel(lam_ref, gsub_ref, q_ref, k_ref, v_ref, o_ref, kn_ref, *, nct, l, t, ck, lam_init):
    i = pl.program_id(2)
    q = q_ref[0]
    tq = q.shape[0]
    lo = _iota((tq, LANES), 1) < 64
    zero = jnp.zeros((tq, LANES), BF16)
    q2 = jnp.concatenate([jnp.where(lo, q, zero), jnp.where(lo, zero, q)], axis=0)

    @pl.when(i == 0)
    def _():
        grp = (_iota((LANES, LANES), 0) // 64 == _iota((LANES, LANES), 1) // 64).astype(BF16)
        mx = jnp.zeros((1, LANES), F32)
        for c0 in range(0, t, ck):
            kf = k_ref[0, c0:c0 + ck, :].astype(F32)
            sq_hi, sq_lo = _split2(kf * kf)
            mx = jnp.maximum(mx, jnp.max(_dot(sq_hi, grp) + _dot(sq_lo, grp), axis=0, keepdims=True))
        kn_ref[...] = mx

    q2f = q2.astype(F32)
    qn = jnp.sqrt(jnp.sum(q2f * q2f, axis=1, keepdims=True))
    kn2 = kn_ref[...]
    kn = jnp.sqrt(jnp.where(_iota((2 * tq, 1), 0) < tq, kn2[:, 0:1], kn2[:, 64:65]))
    bound = qn * kn * 1.01 + 1e-6
    safe = jnp.max(bound) <= SAFE_BOUND

    def finish(den, acc):
        lv = lam_ref[...]
        lam = (jnp.exp(jnp.sum(lv[0:1] * lv[1:2], axis=-1, keepdims=True))
               - jnp.exp(jnp.sum(lv[2:3] * lv[3:4], axis=-1, keepdims=True)) + lam_init)
        o = acc[0:tq] / den[0:tq] - lam * (acc[tq:] / den[tq:])
        ms = jnp.mean(o * o, axis=-1, keepdims=True)
        o_ref[0] = (o * lax.rsqrt(ms + EPS) * gsub_ref[...] * (1.0 - lam_init)).astype(BF16)

    def bounded(nkeys):
        step = min(ck, nkeys)
        psum = jnp.zeros((2 * tq, LANES), F32)
        acc = jnp.zeros((2 * tq, LANES), F32)
        for c0 in range(0, nkeys, step):
            p = jnp.exp2(_dot_nt(q2, k_ref[0, c0:c0 + step, :]) - bound)
            for j in range(step // LANES):
                psum = psum + p[:, j * LANES:(j + 1) * LANES]
            acc = acc + _dot(p.astype(BF16), v_ref[0, c0:c0 + step, :])
        finish(jnp.sum(psum, axis=1, keepdims=True), acc)

    def online(nkeys):
        step = min(ck, nkeys)

        def body(j, carry):
            m, den, acc = carry
            st = pl.multiple_of(j * step, step)
            s = _dot_nt(q2, k_ref[0, pl.ds(st, step), :])
            m2 = jnp.maximum(m, jnp.max(s, axis=-1, keepdims=True))
            a = jnp.exp2(m - m2)
            p = jnp.exp2(s - m2)
            den = a * den + jnp.sum(p, axis=-1, keepdims=True)
            acc = a * acc + _dot(p.astype(BF16), v_ref[0, pl.ds(st, step), :])
            return m2, den, acc

        init = (jnp.full((2 * tq, 1), NEG, F32), jnp.zeros((2 * tq, 1), F32), jnp.zeros((2 * tq, LANES), F32))
        _, den, acc = lax.fori_loop(0, nkeys // step, body, init)
        finish(den, acc)

    for is_ctx, nkeys in ((True, l), (False, t)):
        seg = (i < nct) if is_ctx else (i >= nct)
        pl.when(seg & safe)(functools.partial(bounded, nkeys))
        pl.when(seg & jnp.logical_not(safe))(functools.partial(online, nkeys))


def _diff_attention(qk, v, lam_vecs, g_sub, l, lam_init):
    bn, t, _ = qk.shape
    tq = TM
    ck = 768 if t % 768 == 0 else TQ
    kcol = C_HEADS * 2 * C_HD // LANES
    return pl.pallas_call(
        functools.partial(_diff_kernel, nct=l // tq, l=l, t=t, ck=ck, lam_init=lam_init),
        scratch_shapes=[pltpu.VMEM((1, LANES), F32)],
        grid=(bn, C_HEADS, t // tq),
        in_specs=[pl.BlockSpec((4, C_HD), lambda b, h, i: (0, 0)),
                  pl.BlockSpec((1, LANES), lambda b, h, i: (0, 0)),
                  pl.BlockSpec((1, tq, LANES), lambda b, h, i: (b, i, h)),
                  pl.BlockSpec((1, t, LANES), lambda b, h, i: (b, 0, kcol + h)),
                  pl.BlockSpec((1, t, LANES), lambda b, h, i: (b, 0, h))],
        out_specs=pl.BlockSpec((1, tq, LANES), lambda b, h, i: (b, i, h)),
        out_shape=jax.ShapeDtypeStruct((bn, t, C_HEADS * 2 * C_HD), BF16),
        compiler_params=_cp(("arbitrary", "arbitrary", "arbitrary"), VMEM_LIMIT),
        name="diff_attention",
    )(lam_vecs.astype(F32), g_sub.reshape(1, LANES).astype(F32), qk, qk, v)


def _transpose_exact(x):
    n = x.shape[1]
    ident = (_iota((n, n), 0) == _iota((n, n), 1)).astype(BF16)
    x1 = x.astype(BF16)
    r1 = x - x1.astype(F32)
    x2 = r1.astype(BF16)
    x3 = (r1 - x2.astype(F32)).astype(BF16)
    return _dot_nt(ident, x1) + _dot_nt(ident, x2) + _dot_nt(ident, x3)


def _proj_delta_kernel(x_ref, mod_ref, g_ref, w_ref, wab_ref, alog_ref, dtb_ref,
                       p_ref, z_ref, gc_ref, gct_ref, beta_ref, *, nqkv):
    nb = x_ref.shape[0]
    hs = [_norm_mod(x_ref[k], g_ref[...], mod_ref[k, 0], 0, 1) for k in range(nb)]
    h_all = jnp.concatenate([h.astype(BF16) for h in hs], axis=0)
    nout = w_ref.shape[1]
    for c0 in range(0, nout, COL_CHUNK):
        p = _dot(h_all, w_ref[:, c0:c0 + COL_CHUNK]).astype(BF16)
        for k in range(nb):
            if c0 < nqkv:
                p_ref[k, :, c0:c0 + COL_CHUNK] = _rows(p, k)
            else:
                z_ref[k, :, c0 - nqkv:c0 - nqkv + COL_CHUNK] = _rows(p, k)
    nd = 2 * A_HEADS
    ri = _iota((TM, TM), 0)
    ci = _iota((TM, TM), 1)
    same = (ri // A_CHUNK) == (ci // A_CHUNK)
    lbd = (same & (ci <= ri)).astype(BF16)
    ubd = (same & (ci >= ri)).astype(BF16)
    fwd_cols = _iota((TM, nd), 1) < A_HEADS
    for k in range(nb):
        h_hi, h_lo = _split2(hs[k])
        ab = _dot(h_hi, wab_ref[0]) + _dot(h_hi, wab_ref[1]) + _dot(h_lo, wab_ref[0])
        g = -jnp.exp(alog_ref[...]) * _softplus(ab[:, :nd] + dtb_ref[...])
        beta_ref[k] = _sigmoid(ab[:, nd:])
        g_hi, g_lo = _split2(g)
        gc = jnp.where(fwd_cols, _dot(lbd, g_hi) + _dot(lbd, g_lo), _dot(ubd, g_hi) + _dot(ubd, g_lo))
        gc_ref[k] = gc
        gct_ref[k] = _transpose_exact(gc)


def _proj_delta(xs, mod, g, w_main, wab, a_log, dt_bias, nct_m):
    bn, t, d = xs.shape
    nout = w_main.shape[1]
    nqkv = 2 * A_HEADS * A_DK + A_HEADS * A_DV
    assert nqkv % COL_CHUNK == 0 and nout % COL_CHUNK == 0
    nd = 2 * A_HEADS
    nb = _batch_group(bn)
    row = lambda a: a.reshape(1, nd).astype(F32)
    full = lambda shape: pl.BlockSpec(shape, lambda b, i: (0,) * len(shape))
    return pl.pallas_call(
        functools.partial(_proj_delta_kernel, nqkv=nqkv),
        grid=(bn // nb, t // TM),
        in_specs=[pl.BlockSpec((nb, TM, d), lambda b, i: (b, i, 0)),
                  pl.BlockSpec((nb, 1, 6, d), lambda b, i: (b, _seg(i, nct_m), 0, 0)),
                  full((1, d)), full((d, nout)), full((2, d, 2 * nd)), full((1, nd)), full((1, nd))],
        out_specs=[pl.BlockSpec((nb, TM, nqkv), lambda b, i: (b, i, 0)),
                   pl.BlockSpec((nb, TM, nout - nqkv), lambda b, i: (b, i, 0)),
                   pl.BlockSpec((nb, TM, nd), lambda b, i: (b, i, 0)),
                   pl.BlockSpec((nb, nd, TM), lambda b, i: (b, 0, i)),
                   pl.BlockSpec((nb, TM, nd), lambda b, i: (b, i, 0))],
        out_shape=[jax.ShapeDtypeStruct((bn, t, nqkv), BF16),
                   jax.ShapeDtypeStruct((bn, t, nout - nqkv), BF16),
                   jax.ShapeDtypeStruct((bn, t, nd), F32),
                   jax.ShapeDtypeStruct((bn, nd, t), F32),
                   jax.ShapeDtypeStruct((bn, t, nd), F32)],
        compiler_params=_cp(("arbitrary", "arbitrary"), VMEM_LIMIT),
        name="proj_delta",
    )(xs, mod, g, w_main, wab, row(a_log), row(dt_bias))


def _conv_kernel(p_ref, w_ref, o_ref, scr, *, l, t, ch, pad):
    c = pl.program_id(1)
    scr[0:pad, :] = jnp.zeros((pad, LANES), F32)
    scr[t + pad:t + 2 * pad, :] = jnp.zeros((pad, LANES), F32)
    scr[pad:t + pad, :] = p_ref[0].astype(F32)
    w = w_ref[...]
    half = w.shape[0] // 2
    for r0 in range(0, t, ch):
        near = (r0 <= l + half) and (r0 + ch >= l - half)
        tt = r0 + _iota((ch, 1), 0)
        acc = None
        for d in range(-half, half + 1):
            xd = scr[pad + r0 + d:pad + r0 + d + ch, :]
            if near and d != 0:
                xd = jnp.where(((tt + d) < l) == (tt < l), xd, 0.0)
            term = xd * w[d + half:d + half + 1]
            acc = term if acc is None else acc + term
        y = _silu(acc)
        inv = lax.rsqrt(jnp.sum(y * y, axis=-1, keepdims=True) + EPS)
        out = y * jnp.where(c < A_HEADS, inv * (A_DK ** -0.5), jnp.where(c < 2 * A_HEADS, inv, 1.0))
        o_ref[0, r0:r0 + ch, :] = out.astype(BF16)


def _delta_conv(p, conv_w, l):
    bn, t, nq = p.shape
    ch = 384 if t % 384 == 0 else TQ
    pad = 8
    kw = conv_w.shape[0]
    return pl.pallas_call(
        functools.partial(_conv_kernel, l=l, t=t, ch=ch, pad=pad),
        grid=(bn, nq // LANES),
        in_specs=[pl.BlockSpec((1, t, LANES), lambda b, c: (b, 0, c)),
                  pl.BlockSpec((kw, LANES), lambda b, c: (0, c))],
        out_specs=pl.BlockSpec((1, t, LANES), lambda b, c: (b, 0, c)),
        out_shape=jax.ShapeDtypeStruct((bn, t, nq), BF16),
        scratch_shapes=[pltpu.VMEM((t + 2 * pad, LANES), F32)],
        compiler_params=_cp(("arbitrary", "arbitrary"), VMEM_LIMIT),
        name="delta_conv",
    )(p, conv_w.astype(F32))


def _merge_masks(ii, jj, lower):
    masks = []
    s = 1
    while s < A_CHUNK:
        grp = (ii // (2 * s)) == (jj // (2 * s))
        odd_i, odd_j = (ii // s) % 2 == 1, (jj // s) % 2 == 1
        masks.append(grp & odd_i & ~odd_j if lower else grp & ~odd_i & odd_j)
        s *= 2
    return masks


def _delta_prep_kernel(q_ref, k_ref, v_ref, gc_ref, gct_ref, beta_ref,
                       uf, ub, wf, wb, qdf, qdb, qkf, qkb, kdtf, kdtb, eg_ref, *, hb):
    hblk = pl.program_id(1)
    ii = _iota((TQ, TQ), 0)
    jj = _iota((TQ, TQ), 1)
    same = (ii // A_CHUNK) == (jj // A_CHUNK)
    eye = (ii == jj).astype(F32)
    merge = (_merge_masks(ii, jj, True), _merge_masks(ii, jj, False))
    tri = ((same & (ii >= jj), same & (ii > jj), (ii // A_CHUNK) * A_CHUNK + (A_CHUNK - 1)),
           (same & (ii <= jj), same & (ii < jj), (ii // A_CHUNK) * A_CHUNK))
    lane_d = _iota((TQ, 2 * A_HEADS), 1)
    gc_all = gc_ref[0]
    beta_all = beta_ref[0]
    outs = ((uf, wf, qdf, qkf, kdtf), (ub, wb, qdb, qkb, kdtb))
    heads = []
    for hh in range(hb):
        sl = slice(hh * LANES, (hh + 1) * LANES)
        qb, kb, vb = q_ref[0, :, sl], k_ref[0, :, sl], v_ref[0, :, sl]
        heads.append((sl, qb.astype(F32), kb.astype(F32), vb.astype(F32), _dot_nt(kb, kb), _dot_nt(qb, kb)))
    probs = []
    for hh, (sl, q, k, v, kk, qk) in enumerate(heads):
        for d in range(2):
            incl, strict, last = tri[d]
            idx = d * A_HEADS + hblk * hb + hh
            gcc = jnp.sum(jnp.where(lane_d == idx, gc_all, 0.0), axis=1, keepdims=True)
            bet = jnp.sum(jnp.where(lane_d == idx, beta_all, 0.0), axis=1, keepdims=True)
            gcr = gct_ref[0, pl.ds(idx, 1), :]
            dm = jnp.exp(jnp.where(incl, gcc - gcr, NEG))
            glast = jnp.sum(jnp.where(jj == last, gcr, 0.0), axis=1, keepdims=True)
            a = jnp.where(strict, bet * kk * dm, 0.0)
            probs.append((hh, d, gcc, bet, dm, glast, a))
    ts = [eye - jnp.where(merge[p[1]][0], p[6], 0.0) for p in probs]
    for lvl in range(1, len(merge[0])):
        tbs = [t.astype(BF16) for t in ts]
        ys = [_dot(tb, jnp.where(merge[p[1]][lvl], p[6], 0.0).astype(BF16)) for tb, p in zip(tbs, probs)]
        xs = [_dot(y.astype(BF16), tb) for y, tb in zip(ys, tbs)]
        ts = [t - x for t, x in zip(ts, xs)]
    sols = []
    for t, (hh, d, gcc, bet, dm, glast, a) in zip(ts, probs):
        _, q, k, v, kk, qk = heads[hh]
        rhs = jnp.concatenate([v * bet, k * (bet * jnp.exp(gcc))], axis=1).astype(BF16)
        sols.append(_dot(t.astype(BF16), rhs))
    for sol, (hh, d, gcc, bet, dm, glast, a) in zip(sols, probs):
        sl, q, k, v, kk, qk = heads[hh]
        u_o, w_o, qd_o, qk_o, kdt_o = outs[d]
        u_o[0, :, sl] = sol[:, :LANES].astype(BF16)
        w_o[0, :, sl] = sol[:, LANES:].astype(BF16)
        qd_o[0, :, sl] = (q * jnp.exp(gcc)).astype(BF16)
        qk_o[0, :, sl] = (qk * dm).astype(BF16)
        kdt_o[0, sl, :] = (k * jnp.exp(glast - gcc)).T.astype(BF16)
        eglast = jnp.exp(glast)
        for cch in range(TQ // A_CHUNK):
            row = (d * hb + hh) * (TQ // A_CHUNK) + cch
            eg_ref[0, 0, 0, row:row + 1, :] = jnp.broadcast_to(eglast[cch * A_CHUNK:cch * A_CHUNK + 1], (1, LANES))


def _delta_prep(qkv, gc, gct, beta, hb):
    bn, t, _ = qkv.shape
    nt = t // TQ
    nhb = A_HEADS // hb
    wdt = A_HEADS * A_DV
    tok = lambda off: pl.BlockSpec((1, TQ, hb * LANES), lambda b, h, i: (b, i, off + h))
    nd = 2 * A_HEADS
    tok_shape = lambda dt: jax.ShapeDtypeStruct((bn, t, wdt), dt)
    return pl.pallas_call(
        functools.partial(_delta_prep_kernel, hb=hb),
        grid=(bn, nhb, nt),
        in_specs=[tok(0), tok(nhb), tok(2 * nhb),
                  pl.BlockSpec((1, TQ, nd), lambda b, h, i: (b, i, 0)),
                  pl.BlockSpec((1, nd, TQ), lambda b, h, i: (b, 0, i)),
                  pl.BlockSpec((1, TQ, nd), lambda b, h, i: (b, i, 0))],
        out_specs=[tok(0)] * 8 + [pl.BlockSpec((1, hb * LANES, TQ), lambda b, h, i: (b, h, i))] * 2
                  + [pl.BlockSpec((1, 1, 1, 4 * hb, LANES), lambda b, h, i: (b, h, i, 0, 0))],
        out_shape=[tok_shape(BF16)] * 8
                  + [jax.ShapeDtypeStruct((bn, wdt, t), BF16)] * 2
                  + [jax.ShapeDtypeStruct((bn, nhb, nt, 4 * hb, LANES), F32)],
        compiler_params=_cp(("arbitrary", "arbitrary", "arbitrary"), VMEM_LIMIT),
        name="delta_prep",
    )(qkv, qkv, qkv, gc, gct, beta)


def _delta_scan_kernel(uf, wf, qdf, qkf, kdtf, egf, ub, wb, qdb, qkb, kdtb, egb, of_ref, ob_ref, s_ref, *, hb):
    @pl.when(pl.program_id(1) == 0)
    def _():
        s_ref[...] = jnp.zeros(s_ref.shape, F32)

    zeros = jnp.zeros((A_CHUNK, LANES), BF16)
    dirs = ((uf, wf, qdf, qkf, kdtf, egf, of_ref, (0, 1)), (ub, wb, qdb, qkb, kdtb, egb, ob_ref, (1, 0)))
    chains = [(d, head) for d in range(2) for head in range(A_HEADS)]
    states = [s_ref[d, head] for d, head in chains]
    for step in range(TQ // A_CHUNK):
        ws_all = []
        for (d, head), s in zip(chains, states):
            u, w, qd, qk, kdt, eg, o_ref, order = dirs[d]
            rs = slice(order[step] * A_CHUNK, (order[step] + 1) * A_CHUNK)
            sl = slice(head * LANES, (head + 1) * LANES)
            ws_all.append(_dot(jnp.concatenate([w[0, rs, sl], qd[0, rs, sl]], axis=0), s.astype(BF16)))
        vfulls = []
        for (d, head), ws in zip(chains, ws_all):
            u, w, qd, qk, kdt, eg, o_ref, order = dirs[d]
            cch = order[step]
            rs = slice(cch * A_CHUNK, (cch + 1) * A_CHUNK)
            sl = slice(head * LANES, (head + 1) * LANES)
            vn = (u[0, rs, sl].astype(F32) - ws[:A_CHUNK]).astype(BF16)
            vfull = jnp.concatenate([vn, zeros] if cch == 0 else [zeros, vn], axis=0)
            vfulls.append(vfull)
            o_ref[0, rs, sl] = (ws[A_CHUNK:] + _dot(qk[0, rs, sl], vfull)).astype(BF16)
        new_states = []
        for (d, head), s, vfull in zip(chains, states, vfulls):
            u, w, qd, qk, kdt, eg, o_ref, order = dirs[d]
            hblk, hh = divmod(head, hb)
            row = (d * hb + hh) * 2 + order[step]
            sl = slice(head * LANES, (head + 1) * LANES)
            new_states.append(s * eg[0, hblk, 0, row:row + 1, :] + _dot(kdt[0, sl, :], vfull))
        states = new_states
    for (d, head), s in zip(chains, states):
        s_ref[d, head] = s


def _delta_scan(prep, l, hb):
    uf, ub, wf, wb, qdf, qdb, qkf, qkb, kdtf, kdtb, eg = prep
    bn, t, wdt = uf.shape
    nt = t // TQ
    nct = l // TQ
    nhb = A_HEADS // hb
    fwd = lambda s: s
    bwd = lambda s: jnp.where(s < nct, nct - 1 - s, nt - 1 - (s - nct))
    tok = lambda f: pl.BlockSpec((1, TQ, wdt), lambda b, s: (b, f(s), 0))
    tr = lambda f: pl.BlockSpec((1, wdt, TQ), lambda b, s: (b, 0, f(s)))
    egs = lambda f: pl.BlockSpec((1, nhb, 1, 4 * hb, LANES), lambda b, s: (b, 0, f(s), 0, 0))
    return pl.pallas_call(
        functools.partial(_delta_scan_kernel, hb=hb),
        grid=(bn, nt),
        in_specs=[tok(fwd)] * 4 + [tr(fwd), egs(fwd)] + [tok(bwd)] * 4 + [tr(bwd), egs(bwd)],
        out_specs=[tok(fwd), tok(bwd)],
        out_shape=[jax.ShapeDtypeStruct((bn, t, wdt), BF16)] * 2,
        scratch_shapes=[pltpu.VMEM((2, A_HEADS, A_DK, A_DV), F32)],
        compiler_params=_cp(("arbitrary", "arbitrary"), VMEM_LIMIT),
        name="delta_scan",
    )(uf, wf, qdf, qkf, kdtf, eg, ub, wb, qdb, qkb, kdtb, eg)


def _residual_router(y, w_ref, x_ref, mod_ref, gffn_ref, wr_ref, xo_ref, h_ref, aff_ref):
    o = _dot(y, w_ref[...])
    for k in range(x_ref.shape[0]):
        mod = mod_ref[k, 0]
        xn = x_ref[k] + mod[2:3] * _rows(o, k)
        xo_ref[k] = xn
        h = _norm_mod(xn, gffn_ref[...], mod, 3, 4)
        h_hi, h_lo = _split2(h)
        h_ref[k] = h_hi
        ne = wr_ref.shape[1] // 2
        a_hi = _dot(h_hi, wr_ref[...])
        lg = a_hi[:, :ne] + a_hi[:, ne:] + _dot(h_lo, wr_ref[...])[:, :ne]
        e = jnp.exp(lg - jnp.max(lg, axis=-1, keepdims=True))
        aff_ref[k] = _transpose_exact(e / jnp.sum(e, axis=-1, keepdims=True))


def _out_kernel(y_ref, *rest):
    _residual_router(jnp.concatenate([y_ref[k] for k in range(y_ref.shape[0])], axis=0), *rest)


def _out_delta_kernel(of_ref, ob_ref, z_ref, gout_ref, *rest):
    rows = []
    for k in range(of_ref.shape[0]):
        o = of_ref[k].astype(F32) + ob_ref[k].astype(F32)
        parts = []
        for hd in range(A_HEADS):
            sl = slice(hd * A_DV, (hd + 1) * A_DV)
            oh = o[:, sl]
            ms = jnp.mean(oh * oh, axis=-1, keepdims=True)
            parts.append((oh * lax.rsqrt(ms + EPS) * gout_ref[...] * _silu(z_ref[k, :, sl].astype(F32))).astype(BF16))
        rows.append(jnp.concatenate(parts, axis=1))
    _residual_router(jnp.concatenate(rows, axis=0), *rest)


def _out_proj(pre, w_out, xs, mod, g_ffn, wr, nct_m, delta):
    bn, t, d = xs.shape
    k = w_out.shape[0]
    ne = wr.shape[1] // 2
    nb = _batch_group(bn)
    tok = lambda width: pl.BlockSpec((nb, TM, width), lambda b, i: (b, i, 0))
    full = lambda shape: pl.BlockSpec(shape, lambda b, i: (0,) * len(shape))
    if delta:
        of, ob, z, gout = pre
        head_specs = [tok(k), tok(k), tok(k), full((1, A_DV))]
        head_args = (of, ob, z, gout.reshape(1, A_DV).astype(F32))
        body = _out_delta_kernel
    else:
        head_specs = [tok(k)]
        head_args = (pre,)
        body = _out_kernel
    return pl.pallas_call(
        body,
        grid=(bn // nb, t // TM),
        in_specs=head_specs + [full((k, d)), tok(d),
                               pl.BlockSpec((nb, 1, 6, d), lambda b, i: (b, _seg(i, nct_m), 0, 0)),
                               full((1, d)), full((d, 2 * ne))],
        out_specs=[tok(d), tok(d), pl.BlockSpec((nb, ne, TM), lambda b, i: (b, 0, i))],
        out_shape=[jax.ShapeDtypeStruct((bn, t, d), F32), jax.ShapeDtypeStruct((bn, t, d), BF16),
                   jax.ShapeDtypeStruct((bn, ne, t), F32)],
        compiler_params=_cp(("arbitrary", "arbitrary"), VMEM_LIMIT),
        name="out_delta" if delta else "out_proj",
    )(*head_args, w_out, xs, mod, g_ffn, wr)


def _kth_largest_bits(bits, k):
    def body(it, thr):
        cand = thr | jnp.left_shift(jnp.int32(1), 30 - it)
        cnt = jnp.sum((bits >= cand).astype(I32), axis=1, keepdims=True)
        return jnp.where(cnt >= k, cand, thr)
    return lax.fori_loop(0, 31, body, jnp.zeros((bits.shape[0], 1), I32))


def _select_kernel(aff_ref, posd_ref, offs_ref, gated_ref, *, l, t, cap_c, cap_x):
    a = aff_ref[0]
    ne = a.shape[0]
    bits = pltpu.bitcast(a, I32)
    upper = (_iota((LANES, LANES), 0) <= _iota((LANES, LANES), 1)).astype(BF16)
    nt = t // LANES
    sel = [None] * nt
    for s0, s1, cap in ((0, l, cap_c), (l, t, cap_x)):
        bseg = bits[:, s0:s1]
        thr = _kth_largest_bits(bseg, cap)
        gtf = jnp.where(bseg > thr, 1.0, 0.0)
        eqf = jnp.where(bseg == thr, 1.0, 0.0)
        need = cap - jnp.sum(gtf, axis=1, keepdims=True)
        run = jnp.zeros((ne, 1), F32)
        for j in range((s1 - s0) // LANES):
            ej = eqf[:, j * LANES:(j + 1) * LANES]
            inc = _dot(ej.astype(BF16), upper)
            keep = jnp.where(inc - ej + run < need, ej, 0.0)
            sel[s0 // LANES + j] = jnp.maximum(gtf[:, j * LANES:(j + 1) * LANES], keep)
            run = run + inc[:, LANES - 1:LANES]
    run = jnp.zeros((ne, 1), F32)
    offs = jnp.zeros((ne, LANES), I32)
    lane = _iota((ne, LANES), 1)
    for j in range(nt):
        sj = sel[j]
        inc = _dot(sj.astype(BF16), upper)
        pos = jnp.where(sj > 0.0, inc - sj + run, -1.0)
        posd_ref[0, j] = pos.astype(I32)
        offs = jnp.where(lane == j, run.astype(I32), offs)
        gated_ref[0, j] = jnp.where(sj > 0.0, a[:, j * LANES:(j + 1) * LANES], 0.0)
        run = run + inc[:, LANES - 1:LANES]
    offs_ref[0] = jnp.where(lane == nt, run.astype(I32), offs)


def _select(aff, l, cap_c, cap_x):
    bn, ne, t = aff.shape
    nt = t // LANES
    return pl.pallas_call(
        functools.partial(_select_kernel, l=l, t=t, cap_c=cap_c, cap_x=cap_x),
        grid=(bn,),
        in_specs=[pl.BlockSpec((1, ne, t), lambda b: (b, 0, 0))],
        out_specs=[pl.BlockSpec((1, nt, ne, LANES), lambda b: (b, 0, 0, 0)),
                   pl.BlockSpec((1, ne, LANES), lambda b: (b, 0, 0)),
                   pl.BlockSpec((1, nt, ne, LANES), lambda b: (b, 0, 0, 0))],
        out_shape=[jax.ShapeDtypeStruct((bn, nt, ne, LANES), I32), jax.ShapeDtypeStruct((bn, ne, LANES), I32),
                   jax.ShapeDtypeStruct((bn, nt, ne, LANES), F32)],
        compiler_params=_cp(("arbitrary",), VMEM_LIMIT),
        name="route_select",
    )(aff)


GATHER_TOK = 2 * TQ
GATHER_WIN = 64
EXPERT_GROUP = 4
COMBINE_TOK = 2 * TQ
COMBINE_WIN = COMBINE_TOK + 16
PACK_WIN = 64
EXPERT_VMEM_LIMIT = 60 * 1024 * 1024


def _gather_group(offs_ref, h_ref, pos_ref, gate_ref, xg_ref, gr_ref, b, e0, ne, nt):
    xg_ref[...] = jnp.zeros(xg_ref.shape, BF16)
    gr_ref[...] = jnp.zeros(gr_ref.shape, F32)
    tpt = GATHER_TOK // LANES
    riota = _iota((GATHER_WIN, GATHER_TOK), 0)
    stacked = (EXPERT_GROUP * GATHER_WIN, ne)
    pick = _iota(stacked, 1) == e0 + _iota(stacked, 0) // GATHER_WIN

    def bounds(j, k):
        base = (b * ne + e0 + k) * LANES
        aoff = (offs_ref[base + j * tpt] // 16) * 16
        return aoff, (offs_ref[base + (j + 1) * tpt] - aoff + GATHER_WIN - 1) // GATHER_WIN

    def add_windows(j, w):
        tok = pl.ds(pl.multiple_of(j * GATHER_TOK, GATHER_TOK), GATHER_TOK)
        starts, hots = [], []
        for k in range(EXPERT_GROUP):
            aoff, nwin = bounds(j, k)
            live = w < jnp.maximum(nwin, 1)
            pos = jnp.concatenate([pos_ref[0, j * tpt + c, pl.ds(e0 + k, 1), :] for c in range(tpt)], axis=1)
            first = jnp.where(live, aoff + w * GATHER_WIN, -(1 << 20))
            hots.append(jnp.where(riota == pos - first, 1.0, 0.0).astype(BF16))
            starts.append(pl.multiple_of(aoff + jnp.where(live, w, 0) * GATHER_WIN, 16))
        lhs = jnp.concatenate(hots, axis=0)
        rows = _dot(lhs, h_ref[0, tok, :])
        gd = jnp.concatenate([gate_ref[0, j * tpt + c] for c in range(tpt)], axis=1)
        g1 = gd.astype(BF16)
        r1 = gd - g1.astype(F32)
        g2 = r1.astype(BF16)
        g3 = (r1 - g2.astype(F32)).astype(BF16)
        gall = _dot_nt(lhs, g1) + _dot_nt(lhs, g2) + _dot_nt(lhs, g3)
        gcol = jnp.sum(jnp.where(pick, gall, 0.0), axis=1, keepdims=True)
        for k in range(EXPERT_GROUP):
            rs = slice(k * GATHER_WIN, (k + 1) * GATHER_WIN)
            dst = pl.ds(starts[k], GATHER_WIN)
            xg_ref[k, dst, :] = (xg_ref[k, dst, :].astype(F32) + rows[rs]).astype(BF16)
            gr_ref[k, dst, :] += jnp.broadcast_to(gcol[rs], (GATHER_WIN, LANES))

    def tile(j, carry):
        add_windows(j, 0)
        most = bounds(j, 0)[1]
        for k in range(1, EXPERT_GROUP):
            most = jnp.maximum(most, bounds(j, k)[1])

        def extra(w, carry2):
            add_windows(j, w)
            return carry2

        lax.fori_loop(1, most, extra, 0)
        return carry

    lax.fori_loop(0, nt // tpt, tile, 0)


def _expert_kernel(offs_ref, h_ref, pos_ref, gate_ref, wgu_ref, wd_ref, y_ref, xg_ref, gr_ref, acc_ref, *, nt, r, fc):
    b = pl.program_id(0)
    e = pl.program_id(1)
    slot = e % EXPERT_GROUP

    @pl.when(slot == 0)
    def _():
        _gather_group(offs_ref, h_ref, pos_ref, gate_ref, xg_ref, gr_ref, b, e, pl.num_programs(1), nt)

    xg = xg_ref[slot, pl.ds(0, r), :]
    f = wd_ref.shape[1]
    for c in range(f // fc):
        g = _dot(xg, wgu_ref[0, :, c * fc:(c + 1) * fc])
        u = _dot(xg, wgu_ref[0, :, f + c * fc:f + (c + 1) * fc])
        part = _dot((_silu(g) * u).astype(BF16), wd_ref[0, c * fc:(c + 1) * fc, :])
        if c == 0:
            acc_ref[...] = part
        else:
            acc_ref[...] += part
    y_ref[0, 0] = (acc_ref[...] * gr_ref[slot, pl.ds(0, r), :][:, 0:1]).astype(BF16)


def _experts(offs, hf, posd, gated, wgu, wd, r):
    bn, t, d = hf.shape
    ne, _, f2 = wgu.shape
    nt = t // TQ
    rows = r + GATHER_WIN
    assert ne % EXPERT_GROUP == 0 and rows % 16 == 0
    return pl.pallas_call(
        functools.partial(_expert_kernel, nt=nt, r=r, fc=min(256, f2 // 2)),
        grid_spec=pltpu.PrefetchScalarGridSpec(
            num_scalar_prefetch=1,
            grid=(bn, ne),
            in_specs=[pl.BlockSpec((1, t, d), lambda b, e, o: (b, 0, 0), pipeline_mode=pl.Buffered(1)),
                      pl.BlockSpec((1, nt, ne, LANES), lambda b, e, o: (b, 0, 0, 0)),
                      pl.BlockSpec((1, nt, ne, LANES), lambda b, e, o: (b, 0, 0, 0)),
                      pl.BlockSpec((1, d, f2), lambda b, e, o: (e, 0, 0)),
                      pl.BlockSpec((1, f2 // 2, d), lambda b, e, o: (e, 0, 0))],
            out_specs=pl.BlockSpec((1, 1, r, d), lambda b, e, o: (b, e, 0, 0)),
            scratch_shapes=[pltpu.VMEM((EXPERT_GROUP, rows, d), BF16), pltpu.VMEM((EXPERT_GROUP, rows, LANES), F32),
                            pltpu.VMEM((r, d), F32)]),
        out_shape=jax.ShapeDtypeStruct((bn, ne, r, d), BF16),
        compiler_params=_cp(("arbitrary", "arbitrary"), EXPERT_VMEM_LIMIT),
        name="experts",
    )(offs, hf, posd, gated, wgu, wd)


def _combine_kernel(offs_ref, y_ref, pos_ref, x_ref, mod_ref, o_ref, *, r, win, pack):
    b = pl.program_id(0)
    j = pl.program_id(1)
    ne = y_ref.shape[1]
    tpt = COMBINE_TOK // LANES
    ident = (_iota((LANES, LANES), 0) == _iota((LANES, LANES), 1)).astype(BF16)
    cols = []
    for c in range(tpt):
        p_hi, p_lo = _split2(pos_ref[0, c].astype(F32))
        cols.append((_dot_nt(ident, p_hi) + _dot_nt(ident, p_lo)).astype(I32))
    pc = jnp.concatenate(cols, axis=0)
    offs = [offs_ref[(b * ne + e) * LANES + j * tpt] for e in range(ne)]
    ends = [offs_ref[(b * ne + e) * LANES + (j + 1) * tpt] for e in range(ne)]
    los = [jnp.minimum((offs[e] // 16) * 16, r - pack) for e in range(ne)]
    fits = ends[0] - los[0] <= pack
    for e in range(1, ne):
        fits = fits & (ends[e] - los[e] <= pack)
    gate_mod = mod_ref[0, 0][5:6]

    @pl.when(fits)
    def _():
        lane_e = _iota((1, ne), 1)
        lo_row = jnp.zeros((1, ne), I32)
        for e in range(ne):
            lo_row = jnp.where(lane_e == e, los[e], lo_row)
        rel = pc - lo_row
        rel = jnp.where((pc >= 0) & (rel >= 0) & (rel < pack), rel, -1).astype(F32).astype(BF16)
        spread = (_iota((ne, ne * pack), 0) == _iota((ne, ne * pack), 1) // pack).astype(BF16)
        want = (_iota((COMBINE_TOK, ne * pack), 1) % pack).astype(F32)
        onehot = jnp.where(_dot(rel, spread) == want, 1.0, 0.0).astype(BF16)
        ycat = jnp.concatenate([y_ref[0, e, pl.ds(pl.multiple_of(los[e], 16), pack), :] for e in range(ne)], axis=0)
        o_ref[0] = x_ref[0] + gate_mod * _dot(onehot, ycat)

    @pl.when(jnp.logical_not(fits))
    def _():
        liota = _iota((COMBINE_TOK, win), 1)
        acc = jnp.zeros(x_ref.shape[1:], F32)
        for e in range(ne):
            aoff = pl.multiple_of(jnp.minimum((offs[e] // 16) * 16, r - win), 16)
            onehot = jnp.where(liota == pc[:, e:e + 1] - aoff, 1.0, 0.0).astype(BF16)
            acc = acc + _dot(onehot, y_ref[0, e, pl.ds(aoff, win), :])
        o_ref[0] = x_ref[0] + gate_mod * acc


def _combine(offs, y, posd, xs, mod, l, latent_only):
    bn, t, d = xs.shape
    ne, r = y.shape[1], y.shape[2]
    nct = l // COMBINE_TOK
    tpt = COMBINE_TOK // LANES
    win = min(COMBINE_WIN, r)
    pack = min(PACK_WIN, r)
    assert (r - win) % 16 == 0 and (r - pack) % 16 == 0 and l % COMBINE_TOK == 0 and t % COMBINE_TOK == 0
    tok = pl.BlockSpec((1, COMBINE_TOK, d), lambda b, j, o: (b, j, 0))
    if latent_only:
        out_spec = pl.BlockSpec((1, COMBINE_TOK, d), lambda b, j, o: (b, jnp.maximum(j - nct, 0), 0))
        out_rows = t - l
    else:
        out_spec, out_rows = tok, t
    return pl.pallas_call(
        functools.partial(_combine_kernel, r=r, win=win, pack=pack),
        grid_spec=pltpu.PrefetchScalarGridSpec(
            num_scalar_prefetch=1,
            grid=(bn, t // COMBINE_TOK),
            in_specs=[pl.BlockSpec((1, ne, r, d), lambda b, j, o: (b, 0, 0, 0), pipeline_mode=pl.Buffered(1)),
                      pl.BlockSpec((1, tpt, ne, LANES), lambda b, j, o: (b, j, 0, 0)),
                      tok,
                      pl.BlockSpec((1, 1, 6, d), lambda b, j, o: (b, _seg(j, nct), 0, 0))],
            out_specs=out_spec),
        out_shape=jax.ShapeDtypeStruct((bn, out_rows, d), F32),
        compiler_params=_cp(("arbitrary", "arbitrary"), VMEM_LIMIT),
        name="moe_combine",
    )(offs, y, posd, xs, mod)


def _moe(xs, hf, aff, mod, wgu, wd, l, latent_only):
    bn, t, _ = xs.shape
    cap_c = EC_CAPACITY * l // N_EXPERTS
    cap_x = EC_CAPACITY * (t - l) // N_EXPERTS
    posd, offs, gated = _select(aff, l, cap_c, cap_x)
    offs = offs.reshape(-1)
    y = _experts(offs, hf, posd, gated, wgu, wd, cap_c + cap_x)
    return _combine(offs, y, posd, xs, mod, l, latent_only)


def _hi_lo(w):
    hi = w.astype(BF16)
    return jnp.stack([hi, (w - hi.astype(F32)).astype(BF16)])


def _mixer_delta(xs, mod, g_mix, w_in, conv_w, a_log, dt_bias, l, hb=8):
    nqkvz = 2 * A_HEADS * A_DK + 2 * A_HEADS * A_DV
    wab = w_in[:, nqkvz:]
    p, z, gc, gct, beta = _proj_delta(xs, mod, g_mix, w_in[:, :nqkvz].astype(BF16), _hi_lo(wab), a_log, dt_bias, l // TM)
    qkv = _delta_conv(p, conv_w, l)
    of, ob = _delta_scan(_delta_prep(qkv, gc, gct, beta, hb), l, hb)
    return of, ob, z


def _mixer_swa(xs, mod, g_mix, w_in, qn, kn, sink, cosf, sinf, l):
    nq = B_QHEADS * B_HD
    nk = B_KVHEADS * B_HD
    dup = lambda w: jnp.concatenate([w.reshape(-1, B_KVHEADS, 1, B_HD)] * 2, axis=2).reshape(-1, 2 * nk)
    w = jnp.concatenate([w_in[:, :nq], dup(w_in[:, nq:nq + nk]), dup(w_in[:, nq + nk:])], axis=1).astype(BF16)
    gain = jnp.concatenate([jnp.tile(qn, B_QHEADS) * (B_HD ** -0.5), jnp.tile(kn, 2 * B_KVHEADS)])[None].astype(F32)
    qk, v2 = _proj_qk(xs, mod, g_mix, w, gain, cosf, sinf, nq + 2 * nk, l // TM)
    return _swa_attention(qk, v2, sink, l)


def _mixer_diff(xs, mod, g_mix, w_in, qn, kn, lam_vecs, g_sub, cosf, sinf, l, lam_init):
    nqk = C_HEADS * 2 * C_HD
    gain = jnp.concatenate([jnp.tile(qn, 2 * C_HEADS) * (C_HD ** -0.5 * LOG2E),
                            jnp.tile(kn, 2 * C_HEADS)])[None].astype(F32)
    qk, v = _proj_qk(xs, mod, g_mix, w_in.astype(BF16), gain, cosf, sinf, 2 * nqk, l // TM)
    return _diff_attention(qk, v, lam_vecs, g_sub, l, lam_init)


def kernel(x, c, ctx, c_ctx, w_ada, b_ada, g_mix, g_ffn, a_w_in, a_conv, a_log, a_dt_bias, a_g_out, a_w_out,
           b_w_in, b_q_norm, b_k_norm, b_sink, b_w_out, c_w_in, c_q_norm, c_k_norm, c_lambda, c_g_sub, c_w_out,
           w_router, w_gate_up, w_down):
    depth = w_ada.shape[0]
    n = x.shape[1]
    l = ctx.shape[1]
    assert l % TM == 0 and n % TM == 0
    xs = jnp.concatenate([ctx, x], axis=1)
    mods = _adaln(c, c_ctx, w_ada, b_ada)
    cos_b, sin_b = _rope_tables(n, l, B_HD)
    cos_c, sin_c = _rope_tables(n, l, C_HD)
    nct_m = l // TM
    for layer in range(depth):
        kind, j = layer % N_MIXERS, layer // N_MIXERS
        mod = mods[layer]
        gm = g_mix[layer][None].astype(F32)
        gf = g_ffn[layer][None].astype(F32)
        wr = jnp.concatenate(list(_hi_lo(w_router[layer])), axis=1)
        if kind == 0:
            pre = _mixer_delta(xs, mod, gm, a_w_in[j], a_conv[j], a_log[j], a_dt_bias[j], l)
            xs, hf, aff = _out_proj(pre + (a_g_out[j],), a_w_out[j].astype(BF16), xs, mod, gf, wr, nct_m, True)
        elif kind == 1:
            pre = _mixer_swa(xs, mod, gm, b_w_in[j], b_q_norm[j], b_k_norm[j], b_sink[j], cos_b, sin_b, l)
            xs, hf, aff = _out_proj(pre, b_w_out[j].astype(BF16), xs, mod, gf, wr, nct_m, False)
        else:
            lam_init = 0.8 - 0.6 * math.exp(-0.3 * layer)
            pre = _mixer_diff(xs, mod, gm, c_w_in[j], c_q_norm[j], c_k_norm[j], c_lambda[j], c_g_sub[j],
                              cos_c, sin_c, l, lam_init)
            xs, hf, aff = _out_proj(pre, c_w_out[j].astype(BF16), xs, mod, gf, wr, nct_m, False)
        xs = _moe(xs, hf, aff, mod, w_gate_up[layer].astype(BF16), w_down[layer].astype(BF16), l, layer == depth - 1)
    return xs
```

```python
import functools
import math

import jax
import jax.numpy as jnp
from jax import lax
from jax.experimental import pallas as pl
from jax.experimental.pallas import tpu as pltpu

F32 = jnp.float32
BF16 = jnp.bfloat16
I32 = jnp.int32

EPS = 1e-6
ROPE_BASE = 10000.0
GRID_W = 64
N_MIXERS = 3

A_HEADS, A_DK, A_DV, A_CHUNK = 8, 128, 128, 64
B_QHEADS, B_KVHEADS, B_HD = 16, 4, 64
B_GROUP = B_QHEADS // B_KVHEADS
C_HEADS, C_HD = 8, 64
N_EXPERTS, EC_CAPACITY = 16, 2

LANES = 128
TM = 256
TQ = 128
NEG = -1e30
VMEM_LIMIT = 56 * 1024 * 1024


def _cp(sem, vmem=None):
    return pltpu.CompilerParams(dimension_semantics=sem, vmem_limit_bytes=vmem)


def _dot(a, b):
    return jnp.dot(a, b, preferred_element_type=F32)


def _dot_nt(a, b):
    return lax.dot_general(a, b, (((1,), (1,)), ((), ())), preferred_element_type=F32)


def _split2(a):
    hi = a.astype(BF16)
    return hi, (a - hi.astype(F32)).astype(BF16)


def _sigmoid(x):
    return 1.0 / (1.0 + jnp.exp(-x))


def _silu(x):
    return x * _sigmoid(x)


def _softplus(x):
    return jnp.maximum(x, 0.0) + jnp.log(1.0 + jnp.exp(-jnp.abs(x)))


def _norm_mod(x, g, mod, i_shift, i_scale):
    ms = jnp.mean(x * x, axis=-1, keepdims=True)
    y = x * lax.rsqrt(ms + EPS) * g
    return y * (1.0 + mod[i_scale:i_scale + 1]) + mod[i_shift:i_shift + 1]


def _iota(shape, dim):
    return lax.broadcasted_iota(I32, shape, dim)


def _ada_kernel(s_ref, w_ref, b_ref, o_ref):
    s = _silu(s_ref[...])
    s_hi, s_lo = _split2(s)
    w_hi, w_lo = _split2(w_ref[0])
    o_ref[0] = _dot(s_hi, w_hi) + _dot(s_hi, w_lo) + _dot(s_lo, w_hi) + b_ref[0]


def _adaln(c, c_ctx, w_ada, b_ada):
    depth, d, d6 = w_ada.shape
    bn = c.shape[0]
    rows_n = -(-(bn + 1) // 8) * 8
    rows = jnp.zeros((rows_n, d), F32).at[:bn].set(c).at[bn].set(c_ctx)
    nb = d6 // 4
    out = pl.pallas_call(
        _ada_kernel,
        grid=(depth, d6 // nb),
        in_specs=[pl.BlockSpec((rows_n, d), lambda l, j: (0, 0)),
                  pl.BlockSpec((1, d, nb), lambda l, j: (l, 0, j)),
                  pl.BlockSpec((1, 1, nb), lambda l, j: (l, 0, j))],
        out_specs=pl.BlockSpec((1, rows_n, nb), lambda l, j: (l, 0, j)),
        out_shape=jax.ShapeDtypeStruct((depth, rows_n, d6), F32),
        compiler_params=_cp(("arbitrary", "arbitrary"), VMEM_LIMIT),
        name="adaln",
    )(rows, w_ada, b_ada.reshape(depth, 1, d6))
    mx = out[:, :bn].reshape(depth, bn, 6, d)
    mc = jnp.broadcast_to(out[:, bn].reshape(depth, 1, 6, d), (depth, bn, 6, d))
    return jnp.stack([mc, mx], axis=2)


def _seg(i, nct):
    return jnp.where(i < nct, 0, 1)


COL_CHUNK = 1024


def _batch_group(bn):
    return 4 if bn % 4 == 0 else (2 if bn % 2 == 0 else 1)


def _rows(a, k):
    return a[k * TM:(k + 1) * TM]


def _proj_qk_kernel(x_ref, mod_ref, g_ref, w_ref, gain_ref, cos_ref, sin_ref, qk_ref, v_ref, *, nqk):
    nb = x_ref.shape[0]
    h = jnp.concatenate([_norm_mod(x_ref[k], g_ref[...], mod_ref[k, 0], 0, 1).astype(BF16) for k in range(nb)], axis=0)
    wide = 2 * LANES
    grp = (_iota((wide, wide), 0) // 64 == _iota((wide, wide), 1) // 64).astype(BF16)
    first = (_iota((1, wide), 1) % 64) < 32
    cs = jnp.concatenate([jnp.concatenate([cos_ref[...]] * 2, axis=1)] * nb, axis=0)
    sn = jnp.concatenate([jnp.concatenate([sin_ref[...]] * 2, axis=1)] * nb, axis=0)
    nout = w_ref.shape[1]
    for c0 in range(0, nout, COL_CHUNK):
        p = _dot(h, w_ref[:, c0:min(c0 + COL_CHUNK, nout)])
        for t0 in range(c0, min(c0 + COL_CHUNK, nout), wide):
            xt = p[:, t0 - c0:t0 - c0 + wide]
            if t0 < nqk:
                ms = _dot((xt * xt).astype(BF16), grp) * (1.0 / 64)
                y = xt * lax.rsqrt(ms + EPS) * gain_ref[:, t0:t0 + wide]
                rot = jnp.where(first, pltpu.roll(y, wide - 32, 1), pltpu.roll(y, 32, 1))
                out = (y * cs + rot * sn).astype(BF16)
                for k in range(nb):
                    qk_ref[k, :, t0:t0 + wide] = _rows(out, k)
            else:
                for k in range(nb):
                    v_ref[k, :, t0 - nqk:t0 - nqk + wide] = _rows(xt, k).astype(BF16)


def _proj_qk(xs, mod, g, w, gain, cosf, sinf, nqk, nct_m):
    bn, t, d = xs.shape
    nout = w.shape[1]
    nb = _batch_group(bn)
    return pl.pallas_call(
        functools.partial(_proj_qk_kernel, nqk=nqk),
        grid=(bn // nb, t // TM),
        in_specs=[pl.BlockSpec((nb, TM, d), lambda b, i: (b, i, 0)),
                  pl.BlockSpec((nb, 1, 6, d), lambda b, i: (b, _seg(i, nct_m), 0, 0)),
                  pl.BlockSpec((1, d), lambda b, i: (0, 0)),
                  pl.BlockSpec((d, nout), lambda b, i: (0, 0)),
                  pl.BlockSpec((1, nqk), lambda b, i: (0, 0)),
                  pl.BlockSpec((TM, LANES), lambda b, i: (i, 0)),
                  pl.BlockSpec((TM, LANES), lambda b, i: (i, 0))],
        out_specs=[pl.BlockSpec((nb, TM, nqk), lambda b, i: (b, i, 0)),
                   pl.BlockSpec((nb, TM, nout - nqk), lambda b, i: (b, i, 0))],
        out_shape=[jax.ShapeDtypeStruct((bn, t, nqk), BF16),
                   jax.ShapeDtypeStruct((bn, t, nout - nqk), BF16)],
        compiler_params=_cp(("arbitrary", "arbitrary"), VMEM_LIMIT),
        name="proj_qk",
    )(xs, mod, g, w, gain, cosf, sinf)


def _rope_tables(n, l, head_dim):
    t = jnp.arange(n)
    n_freq = head_dim // 4
    inv = ROPE_BASE ** (-jnp.arange(n_freq, dtype=F32) / n_freq)
    ang = jnp.concatenate([(t // GRID_W).astype(F32)[:, None] * inv, (t % GRID_W).astype(F32)[:, None] * inv], -1)
    cs, sn = jnp.cos(ang), jnp.sin(ang)
    reps = LANES // head_dim
    cosf = jnp.tile(jnp.concatenate([cs, cs], -1), (1, reps))
    sinf = jnp.tile(jnp.concatenate([-sn, sn], -1), (1, reps))
    cosf = jnp.concatenate([jnp.ones((l, LANES), F32), cosf], 0)
    sinf = jnp.concatenate([jnp.zeros((l, LANES), F32), sinf], 0)
    return cosf, sinf


def _swa_kernel(q_ref, kp_ref, kc_ref, kn_ref, kx_ref, vp_ref, vc_ref, vn_ref, vx_ref, sink_ref, bias_ref, o_ref,
                *, nct, ntx, l):
    xi = pl.program_id(1) - nct
    lo = _iota((TQ, LANES), 1) < 64
    zero = jnp.zeros((TQ, LANES), BF16)
    c = _iota((1, 3 * TQ + l), 1)
    dead = (((c < TQ) & (xi < 1)) | ((c >= 2 * TQ) & (c < 3 * TQ) & (xi + 1 >= ntx)) | ((c < 3 * TQ) & (xi < 0)))
    bias = bias_ref[...] + jnp.where(dead, NEG, 0.0)
    bias4 = jnp.concatenate([bias] * B_GROUP, axis=0)
    gw = B_GROUP * B_HD
    scores = []
    for kv in range(B_KVHEADS):
        ks = slice(kv * LANES, (kv + 1) * LANES)
        qa, qb = q_ref[0, :, kv * gw:kv * gw + LANES], q_ref[0, :, kv * gw + LANES:(kv + 1) * gw]
        q4 = jnp.concatenate([jnp.where(lo, qa, zero), jnp.where(lo, zero, qa),
                              jnp.where(lo, qb, zero), jnp.where(lo, zero, qb)], axis=0)
        kcat = jnp.concatenate([kp_ref[0, :, ks], kc_ref[0, :, ks], kn_ref[0, :, ks], kx_ref[0, :, ks]], axis=0)
        scores.append(_dot_nt(q4, kcat) + bias4)
    probs = []
    for kv, s in enumerate(scores):
        sk = sink_ref[kv]
        m = jnp.maximum(jnp.max(s, axis=-1, keepdims=True), sk)
        p = jnp.exp(s - m)
        probs.append((p.astype(BF16), jnp.sum(p, axis=-1, keepdims=True) + jnp.exp(sk - m)))
    for kv, (p, den) in enumerate(probs):
        ks = slice(kv * LANES, (kv + 1) * LANES)
        vcat = jnp.concatenate([vp_ref[0, :, ks], vc_ref[0, :, ks], vn_ref[0, :, ks], vx_ref[0, :, ks]], axis=0)
        o4 = _dot(p, vcat) / den
        oa = jnp.where(lo, o4[0:TQ], o4[TQ:2 * TQ])
        ob = jnp.where(lo, o4[2 * TQ:3 * TQ], o4[3 * TQ:4 * TQ])
        o_ref[0, :, kv * gw:(kv + 1) * gw] = jnp.concatenate([oa, ob], axis=1).astype(BF16)


def _swa_attention(qk, v2, sink, l):
    bn, t, _ = qk.shape
    nt = t // TQ
    nct = l // TQ
    nq = B_QHEADS * B_HD
    kw = B_KVHEADS * LANES
    prev = lambda i: jnp.maximum(i - 1, 0)
    nxt = lambda i: jnp.minimum(i + 1, nt - 1)
    cur = lambda i: i
    kspec = lambda f: pl.BlockSpec((1, TQ, kw), lambda b, i: (b, f(i), nq // kw))
    vspec = lambda f: pl.BlockSpec((1, TQ, kw), lambda b, i: (b, f(i), 0))
    sinkcol = jnp.repeat(sink.reshape(B_KVHEADS, B_GROUP), TQ, axis=1).reshape(B_KVHEADS, B_GROUP * TQ, 1).astype(F32)
    r = jnp.arange(TQ)[:, None]
    c = jnp.arange(3 * TQ + l)[None, :]
    band = ((c < TQ) & (c >= r)) | ((c >= TQ) & (c < 2 * TQ)) | ((c >= 2 * TQ) & (c - 2 * TQ <= r)) | (c >= 3 * TQ)
    bias = jnp.where(band, 0.0, NEG).astype(F32)
    return pl.pallas_call(
        functools.partial(_swa_kernel, nct=nct, ntx=nt - nct, l=l),
        grid=(bn, nt),
        in_specs=[pl.BlockSpec((1, TQ, nq), lambda b, i: (b, i, 0)),
                  kspec(prev), kspec(cur), kspec(nxt),
                  pl.BlockSpec((1, l, kw), lambda b, i: (b, 0, nq // kw)),
                  vspec(prev), vspec(cur), vspec(nxt),
                  pl.BlockSpec((1, l, kw), lambda b, i: (b, 0, 0)),
                  pl.BlockSpec((B_KVHEADS, B_GROUP * TQ, 1), lambda b, i: (0, 0, 0)),
                  pl.BlockSpec((TQ, 3 * TQ + l), lambda b, i: (0, 0))],
        out_specs=pl.BlockSpec((1, TQ, nq), lambda b, i: (b, i, 0)),
        out_shape=jax.ShapeDtypeStruct((bn, t, nq), BF16),
        compiler_params=_cp(("arbitrary", "arbitrary"), VMEM_LIMIT),
        name="swa_attention",
    )(qk, qk, qk, qk, qk, v2, v2, v2, v2, sinkcol, bias)


LOG2E = 1.4426950408889634
SAFE_BOUND = 60.0


def _diff_kernel(lam_ref, gsub_ref, q_ref, k_ref, v_ref, o_ref, kn_ref, *, l, t, ck, lam_init):
    i = pl.program_id(2)
    q = q_ref[0]
    tq = q.shape[0]
    lo = _iota((tq, LANES), 1) < 64
    zero = jnp.zeros((tq, LANES), BF16)
    q2 = jnp.concatenate([jnp.where(lo, q, zero), jnp.where(lo, zero, q)], axis=0)

    @pl.when(i == 0)
    def _():
        grp = (_iota((LANES, LANES), 0) // 64 == _iota((LANES, LANES), 1) // 64).astype(BF16)
        mx = jnp.zeros((1, LANES), F32)
        for c0 in range(0, t, ck):
            kf = k_ref[0, c0:c0 + ck, :].astype(F32)
            sq_hi, sq_lo = _split2(kf * kf)
            mx = jnp.maximum(mx, jnp.max(_dot(sq_hi, grp) + _dot(sq_lo, grp), axis=0, keepdims=True))
        kn_ref[...] = mx

    q2f = q2.astype(F32)
    qn = jnp.sqrt(jnp.sum(q2f * q2f, axis=1, keepdims=True))
    kn2 = kn_ref[...]
    kn = jnp.sqrt(jnp.where(_iota((2 * tq, 1), 0) < tq, kn2[:, 0:1], kn2[:, 64:65]))
    bound = qn * kn * 1.01 + 1e-6
    safe = jnp.max(bound) <= SAFE_BOUND

    def finish(den, acc):
        lv = lam_ref[...]
        lam = (jnp.exp(jnp.sum(lv[0:1] * lv[1:2], axis=-1, keepdims=True))
               - jnp.exp(jnp.sum(lv[2:3] * lv[3:4], axis=-1, keepdims=True)) + lam_init)
        o = acc[0:tq] / den[0:tq] - lam * (acc[tq:] / den[tq:])
        ms = jnp.mean(o * o, axis=-1, keepdims=True)
        o_ref[0] = (o * lax.rsqrt(ms + EPS) * gsub_ref[...] * (1.0 - lam_init)).astype(BF16)

    ctx_row = (i * tq + _iota((2 * tq, 1), 0) % tq) < l

    def ctx_bias(first_key, nkeys):
        return jnp.where(ctx_row & (first_key + _iota((1, nkeys), 1) >= l), NEG, 0.0)

    def bounded(masked):
        psum = jnp.zeros((2 * tq, LANES), F32)
        acc = jnp.zeros((2 * tq, LANES), F32)
        for c0 in range(0, t, ck):
            s = _dot_nt(q2, k_ref[0, c0:c0 + ck, :])
            if masked:
                s = s + ctx_bias(c0, ck)
            p = jnp.exp2(s - bound)
            for j in range(ck // LANES):
                psum = psum + p[:, j * LANES:(j + 1) * LANES]
            acc = acc + _dot(p.astype(BF16), v_ref[0, c0:c0 + ck, :])
        finish(jnp.sum(psum, axis=1, keepdims=True), acc)

    def online(masked):
        def body(j, carry):
            m, den, acc = carry
            st = pl.multiple_of(j * ck, ck)
            s = _dot_nt(q2, k_ref[0, pl.ds(st, ck), :])
            if masked:
                s = s + ctx_bias(st, ck)
            m2 = jnp.maximum(m, jnp.max(s, axis=-1, keepdims=True))
            a = jnp.exp2(m - m2)
            p = jnp.exp2(s - m2)
            den = a * den + jnp.sum(p, axis=-1, keepdims=True)
            acc = a * acc + _dot(p.astype(BF16), v_ref[0, pl.ds(st, ck), :])
            return m2, den, acc

        init = (jnp.full((2 * tq, 1), NEG, F32), jnp.zeros((2 * tq, 1), F32), jnp.zeros((2 * tq, LANES), F32))
        _, den, acc = lax.fori_loop(0, t // ck, body, init)
        finish(den, acc)

    has_ctx = i * tq < l
    for masked, seg in ((True, has_ctx), (False, jnp.logical_not(has_ctx))):
        pl.when(seg & safe)(functools.partial(bounded, masked))
        pl.when(seg & jnp.logical_not(safe))(functools.partial(online, masked))


def _diff_attention(qk, v, lam_vecs, g_sub, l, lam_init):
    bn, t, _ = qk.shape
    ck = 768 if t % 768 == 0 else TQ
    tq = 768 if t % 768 == 0 else TM
    kcol = C_HEADS * 2 * C_HD // LANES
    return pl.pallas_call(
        functools.partial(_diff_kernel, l=l, t=t, ck=ck, lam_init=lam_init),
        scratch_shapes=[pltpu.VMEM((1, LANES), F32)],
        grid=(bn, C_HEADS, t // tq),
        in_specs=[pl.BlockSpec((4, C_HD), lambda b, h, i: (0, 0)),
                  pl.BlockSpec((1, LANES), lambda b, h, i: (0, 0)),
                  pl.BlockSpec((1, tq, LANES), lambda b, h, i: (b, i, h)),
                  pl.BlockSpec((1, t, LANES), lambda b, h, i: (b, 0, kcol + h)),
                  pl.BlockSpec((1, t, LANES), lambda b, h, i: (b, 0, h))],
        out_specs=pl.BlockSpec((1, tq, LANES), lambda b, h, i: (b, i, h)),
        out_shape=jax.ShapeDtypeStruct((bn, t, C_HEADS * 2 * C_HD), BF16),
        compiler_params=_cp(("arbitrary", "arbitrary", "arbitrary"), VMEM_LIMIT),
        name="diff_attention",
    )(lam_vecs.astype(F32), g_sub.reshape(1, LANES).astype(F32), qk, qk, v)


def _transpose_exact(x):
    n = x.shape[1]
    ident = (_iota((n, n), 0) == _iota((n, n), 1)).astype(BF16)
    x1 = x.astype(BF16)
    r1 = x - x1.astype(F32)
    x2 = r1.astype(BF16)
    x3 = (r1 - x2.astype(F32)).astype(BF16)
    return _dot_nt(ident, x1) + _dot_nt(ident, x2) + _dot_nt(ident, x3)


def _proj_delta_kernel(x_ref, mod_ref, g_ref, w_ref, wab_ref, alog_ref, dtb_ref,
                       p_ref, z_ref, gc_ref, gct_ref, beta_ref, *, nqkv):
    nb = x_ref.shape[0]
    hs = [_norm_mod(x_ref[k], g_ref[...], mod_ref[k, 0], 0, 1) for k in range(nb)]
    h_all = jnp.concatenate([h.astype(BF16) for h in hs], axis=0)
    nout = w_ref.shape[1]
    for c0 in range(0, nout, COL_CHUNK):
        p = _dot(h_all, w_ref[:, c0:c0 + COL_CHUNK]).astype(BF16)
        for k in range(nb):
            if c0 < nqkv:
                p_ref[k, :, c0:c0 + COL_CHUNK] = _rows(p, k)
            else:
                z_ref[k, :, c0 - nqkv:c0 - nqkv + COL_CHUNK] = _rows(p, k)
    nd = 2 * A_HEADS
    ri = _iota((TM, TM), 0)
    ci = _iota((TM, TM), 1)
    same = (ri // A_CHUNK) == (ci // A_CHUNK)
    lbd = (same & (ci <= ri)).astype(BF16)
    ubd = (same & (ci >= ri)).astype(BF16)
    fwd_cols = _iota((TM, nd), 1) < A_HEADS
    for k in range(nb):
        h_hi, h_lo = _split2(hs[k])
        ab_hi = _dot(h_hi, wab_ref[...])
        ab = ab_hi[:, :2 * nd] + ab_hi[:, 2 * nd:] + _dot(h_lo, wab_ref[...])[:, :2 * nd]
        g = -jnp.exp(alog_ref[...]) * _softplus(ab[:, :nd] + dtb_ref[...])
        beta_ref[k] = _sigmoid(ab[:, nd:])
        g_hi, g_lo = _split2(g)
        gc = jnp.where(fwd_cols, _dot(lbd, g_hi) + _dot(lbd, g_lo), _dot(ubd, g_hi) + _dot(ubd, g_lo))
        gc_ref[k] = gc
        gct_ref[k] = _transpose_exact(gc)


def _proj_delta(xs, mod, g, w_main, wab, a_log, dt_bias, nct_m):
    bn, t, d = xs.shape
    nout = w_main.shape[1]
    nqkv = 2 * A_HEADS * A_DK + A_HEADS * A_DV
    assert nqkv % COL_CHUNK == 0 and nout % COL_CHUNK == 0
    nd = 2 * A_HEADS
    nb = _batch_group(bn)
    row = lambda a: a.reshape(1, nd).astype(F32)
    full = lambda shape: pl.BlockSpec(shape, lambda b, i: (0,) * len(shape))
    return pl.pallas_call(
        functools.partial(_proj_delta_kernel, nqkv=nqkv),
        grid=(bn // nb, t // TM),
        in_specs=[pl.BlockSpec((nb, TM, d), lambda b, i: (b, i, 0)),
                  pl.BlockSpec((nb, 1, 6, d), lambda b, i: (b, _seg(i, nct_m), 0, 0)),
                  full((1, d)), full((d, nout)), full((d, 4 * nd)), full((1, nd)), full((1, nd))],
        out_specs=[pl.BlockSpec((nb, TM, nqkv), lambda b, i: (b, i, 0)),
                   pl.BlockSpec((nb, TM, nout - nqkv), lambda b, i: (b, i, 0)),
                   pl.BlockSpec((nb, TM, nd), lambda b, i: (b, i, 0)),
                   pl.BlockSpec((nb, nd, TM), lambda b, i: (b, 0, i)),
                   pl.BlockSpec((nb, TM, nd), lambda b, i: (b, i, 0))],
        out_shape=[jax.ShapeDtypeStruct((bn, t, nqkv), BF16),
                   jax.ShapeDtypeStruct((bn, t, nout - nqkv), BF16),
                   jax.ShapeDtypeStruct((bn, t, nd), F32),
                   jax.ShapeDtypeStruct((bn, nd, t), F32),
                   jax.ShapeDtypeStruct((bn, t, nd), F32)],
        compiler_params=_cp(("arbitrary", "arbitrary"), VMEM_LIMIT),
        name="proj_delta",
    )(xs, mod, g, w_main, wab, row(a_log), row(dt_bias))


def _conv_kernel(p_ref, w_ref, o_ref, scr, *, l, t, ch, pad):
    c = pl.program_id(1)
    scr[0:pad, :] = jnp.zeros((pad, LANES), F32)
    scr[t + pad:t + 2 * pad, :] = jnp.zeros((pad, LANES), F32)
    scr[pad:t + pad, :] = p_ref[0].astype(F32)
    w = w_ref[...]
    half = w.shape[0] // 2
    for r0 in range(0, t, ch):
        near = (r0 <= l + half) and (r0 + ch >= l - half)
        tt = r0 + _iota((ch, 1), 0)
        acc = None
        for d in range(-half, half + 1):
            xd = scr[pad + r0 + d:pad + r0 + d + ch, :]
            if near and d != 0:
                xd = jnp.where(((tt + d) < l) == (tt < l), xd, 0.0)
            term = xd * w[d + half:d + half + 1]
            acc = term if acc is None else acc + term
        y = _silu(acc)
        inv = lax.rsqrt(jnp.sum(y * y, axis=-1, keepdims=True) + EPS)
        out = y * jnp.where(c < A_HEADS, inv * (A_DK ** -0.5), jnp.where(c < 2 * A_HEADS, inv, 1.0))
        o_ref[0, r0:r0 + ch, :] = out.astype(BF16)


def _delta_conv(p, conv_w, l):
    bn, t, nq = p.shape
    ch = 384 if t % 384 == 0 else TQ
    pad = 8
    kw = conv_w.shape[0]
    return pl.pallas_call(
        functools.partial(_conv_kernel, l=l, t=t, ch=ch, pad=pad),
        grid=(bn, nq // LANES),
        in_specs=[pl.BlockSpec((1, t, LANES), lambda b, c: (b, 0, c)),
                  pl.BlockSpec((kw, LANES), lambda b, c: (0, c))],
        out_specs=pl.BlockSpec((1, t, LANES), lambda b, c: (b, 0, c)),
        out_shape=jax.ShapeDtypeStruct((bn, t, nq), BF16),
        scratch_shapes=[pltpu.VMEM((t + 2 * pad, LANES), F32)],
        compiler_params=_cp(("arbitrary", "arbitrary"), VMEM_LIMIT),
        name="delta_conv",
    )(p, conv_w.astype(F32))


def _merge_masks(ii, jj, lower):
    masks = []
    s = 1
    while s < A_CHUNK:
        grp = (ii // (2 * s)) == (jj // (2 * s))
        odd_i, odd_j = (ii // s) % 2 == 1, (jj // s) % 2 == 1
        masks.append(grp & odd_i & ~odd_j if lower else grp & ~odd_i & odd_j)
        s *= 2
    return masks


def _delta_prep_kernel(q_ref, k_ref, v_ref, gc_ref, gct_ref, beta_ref,
                       uf, ub, wf, wb, qdf, qdb, qkf, qkb, kdtf, kdtb, eg_ref, *, hb):
    hblk = pl.program_id(1)
    ii = _iota((TQ, TQ), 0)
    jj = _iota((TQ, TQ), 1)
    same = (ii // A_CHUNK) == (jj // A_CHUNK)
    eye = (ii == jj).astype(F32)
    merge = (_merge_masks(ii, jj, True), _merge_masks(ii, jj, False))
    tri = ((same & (ii >= jj), same & (ii > jj), (ii // A_CHUNK) * A_CHUNK + (A_CHUNK - 1)),
           (same & (ii <= jj), same & (ii < jj), (ii // A_CHUNK) * A_CHUNK))
    lane_d = _iota((TQ, 2 * A_HEADS), 1)
    gc_all = gc_ref[0]
    beta_all = beta_ref[0]
    outs = ((uf, wf, qdf, qkf, kdtf), (ub, wb, qdb, qkb, kdtb))
    heads = []
    for hh in range(hb):
        sl = slice(hh * LANES, (hh + 1) * LANES)
        qb, kb, vb = q_ref[0, :, sl], k_ref[0, :, sl], v_ref[0, :, sl]
        heads.append((sl, qb.astype(F32), kb.astype(F32), vb.astype(F32), _dot_nt(kb, kb), _dot_nt(qb, kb)))
    probs = []
    for hh, (sl, q, k, v, kk, qk) in enumerate(heads):
        for d in range(2):
            incl, strict, last = tri[d]
            idx = d * A_HEADS + hblk * hb + hh
            gcc = jnp.sum(jnp.where(lane_d == idx, gc_all, 0.0), axis=1, keepdims=True)
            bet = jnp.sum(jnp.where(lane_d == idx, beta_all, 0.0), axis=1, keepdims=True)
            gcr = gct_ref[0, pl.ds(idx, 1), :]
            dm = jnp.exp(jnp.where(incl, gcc - gcr, NEG))
            glast = jnp.sum(jnp.where(jj == last, gcr, 0.0), axis=1, keepdims=True)
            a = jnp.where(strict, bet * kk * dm, 0.0)
            probs.append((hh, d, gcc, bet, dm, glast, a))
    ts = [eye - jnp.where(merge[p[1]][0], p[6], 0.0) for p in probs]
    for lvl in range(1, len(merge[0])):
        tbs = [t.astype(BF16) for t in ts]
        ys = [_dot(tb, jnp.where(merge[p[1]][lvl], p[6], 0.0).astype(BF16)) for tb, p in zip(tbs, probs)]
        xs = [_dot(y.astype(BF16), tb) for y, tb in zip(ys, tbs)]
        ts = [t - x for t, x in zip(ts, xs)]
    sols = []
    for t, (hh, d, gcc, bet, dm, glast, a) in zip(ts, probs):
        _, q, k, v, kk, qk = heads[hh]
        rhs = jnp.concatenate([v * bet, k * (bet * jnp.exp(gcc))], axis=1).astype(BF16)
        sols.append(_dot(t.astype(BF16), rhs))
    for sol, (hh, d, gcc, bet, dm, glast, a) in zip(sols, probs):
        sl, q, k, v, kk, qk = heads[hh]
        u_o, w_o, qd_o, qk_o, kdt_o = outs[d]
        u_o[0, :, sl] = sol[:, :LANES].astype(BF16)
        w_o[0, :, sl] = sol[:, LANES:].astype(BF16)
        qd_o[0, :, sl] = (q * jnp.exp(gcc)).astype(BF16)
        qk_o[0, :, sl] = (qk * dm).astype(BF16)
        kdt_o[0, sl, :] = (k * jnp.exp(glast - gcc)).T.astype(BF16)
        eglast = jnp.exp(glast)
        for cch in range(TQ // A_CHUNK):
            row = (d * hb + hh) * (TQ // A_CHUNK) + cch
            eg_ref[0, 0, 0, row:row + 1, :] = jnp.broadcast_to(eglast[cch * A_CHUNK:cch * A_CHUNK + 1], (1, LANES))


def _delta_prep(qkv, gc, gct, beta, hb):
    bn, t, _ = qkv.shape
    nt = t // TQ
    nhb = A_HEADS // hb
    wdt = A_HEADS * A_DV
    tok = lambda off: pl.BlockSpec((1, TQ, hb * LANES), lambda b, h, i: (b, i, off + h))
    nd = 2 * A_HEADS
    tok_shape = lambda dt: jax.ShapeDtypeStruct((bn, t, wdt), dt)
    return pl.pallas_call(
        functools.partial(_delta_prep_kernel, hb=hb),
        grid=(bn, nhb, nt),
        in_specs=[tok(0), tok(nhb), tok(2 * nhb),
                  pl.BlockSpec((1, TQ, nd), lambda b, h, i: (b, i, 0)),
                  pl.BlockSpec((1, nd, TQ), lambda b, h, i: (b, 0, i)),
                  pl.BlockSpec((1, TQ, nd), lambda b, h, i: (b, i, 0))],
        out_specs=[tok(0)] * 8 + [pl.BlockSpec((1, hb * LANES, TQ), lambda b, h, i: (b, h, i))] * 2
                  + [pl.BlockSpec((1, 1, 1, 4 * hb, LANES), lambda b, h, i: (b, h, i, 0, 0))],
        out_shape=[tok_shape(BF16)] * 8
                  + [jax.ShapeDtypeStruct((bn, wdt, t), BF16)] * 2
                  + [jax.ShapeDtypeStruct((bn, nhb, nt, 4 * hb, LANES), F32)],
        compiler_params=_cp(("arbitrary", "arbitrary", "arbitrary"), VMEM_LIMIT),
        name="delta_prep",
    )(qkv, qkv, qkv, gc, gct, beta)


def _delta_scan_kernel(uf, wf, qdf, qkf, kdtf, egf, ub, wb, qdb, qkb, kdtb, egb, of_ref, ob_ref, s_ref, *, hb):
    @pl.when(pl.program_id(1) == 0)
    def _():
        s_ref[...] = jnp.zeros(s_ref.shape, F32)

    zeros = jnp.zeros((A_CHUNK, LANES), BF16)
    dirs = ((uf, wf, qdf, qkf, kdtf, egf, of_ref, (0, 1)), (ub, wb, qdb, qkb, kdtb, egb, ob_ref, (1, 0)))
    chains = [(d, head) for d in range(2) for head in range(A_HEADS)]
    states = [s_ref[d, head] for d, head in chains]
    for step in range(TQ // A_CHUNK):
        ws_all = []
        for (d, head), s in zip(chains, states):
            u, w, qd, qk, kdt, eg, o_ref, order = dirs[d]
            rs = slice(order[step] * A_CHUNK, (order[step] + 1) * A_CHUNK)
            sl = slice(head * LANES, (head + 1) * LANES)
            ws_all.append(_dot(jnp.concatenate([w[0, rs, sl], qd[0, rs, sl]], axis=0), s.astype(BF16)))
        vfulls = []
        for (d, head), ws in zip(chains, ws_all):
            u, w, qd, qk, kdt, eg, o_ref, order = dirs[d]
            cch = order[step]
            rs = slice(cch * A_CHUNK, (cch + 1) * A_CHUNK)
            sl = slice(head * LANES, (head + 1) * LANES)
            vn = (u[0, rs, sl].astype(F32) - ws[:A_CHUNK]).astype(BF16)
            vfull = jnp.concatenate([vn, zeros] if cch == 0 else [zeros, vn], axis=0)
            vfulls.append(vfull)
            o_ref[0, rs, sl] = (ws[A_CHUNK:] + _dot(qk[0, rs, sl], vfull)).astype(BF16)
        new_states = []
        for (d, head), s, vfull in zip(chains, states, vfulls):
            u, w, qd, qk, kdt, eg, o_ref, order = dirs[d]
            hblk, hh = divmod(head, hb)
            row = (d * hb + hh) * 2 + order[step]
            sl = slice(head * LANES, (head + 1) * LANES)
            new_states.append(s * eg[0, hblk, 0, row:row + 1, :] + _dot(kdt[0, sl, :], vfull))
        states = new_states
    for (d, head), s in zip(chains, states):
        s_ref[d, head] = s


def _delta_scan(prep, l, hb):
    uf, ub, wf, wb, qdf, qdb, qkf, qkb, kdtf, kdtb, eg = prep
    bn, t, wdt = uf.shape
    nt = t // TQ
    nct = l // TQ
    nhb = A_HEADS // hb
    fwd = lambda s: s
    bwd = lambda s: jnp.where(s < nct, nct - 1 - s, nt - 1 - (s - nct))
    tok = lambda f: pl.BlockSpec((1, TQ, wdt), lambda b, s: (b, f(s), 0))
    tr = lambda f: pl.BlockSpec((1, wdt, TQ), lambda b, s: (b, 0, f(s)))
    egs = lambda f: pl.BlockSpec((1, nhb, 1, 4 * hb, LANES), lambda b, s: (b, 0, f(s), 0, 0))
    return pl.pallas_call(
        functools.partial(_delta_scan_kernel, hb=hb),
        grid=(bn, nt),
        in_specs=[tok(fwd)] * 4 + [tr(fwd), egs(fwd)] + [tok(bwd)] * 4 + [tr(bwd), egs(bwd)],
        out_specs=[tok(fwd), tok(bwd)],
        out_shape=[jax.ShapeDtypeStruct((bn, t, wdt), BF16)] * 2,
        scratch_shapes=[pltpu.VMEM((2, A_HEADS, A_DK, A_DV), F32)],
        compiler_params=_cp(("arbitrary", "arbitrary"), VMEM_LIMIT),
        name="delta_scan",
    )(uf, wf, qdf, qkf, kdtf, eg, ub, wb, qdb, qkb, kdtb, eg)


def _residual_router(y, w_ref, x_ref, mod_ref, gffn_ref, wr_ref, xo_ref, h_ref, aff_ref):
    o = _dot(y, w_ref[...])
    for k in range(x_ref.shape[0]):
        mod = mod_ref[k, 0]
        xn = x_ref[k] + mod[2:3] * _rows(o, k)
        xo_ref[k] = xn
        h = _norm_mod(xn, gffn_ref[...], mod, 3, 4)
        h_hi, h_lo = _split2(h)
        h_ref[k] = h_hi
        ne = wr_ref.shape[1] // 2
        a_hi = _dot(h_hi, wr_ref[...])
        lg = a_hi[:, :ne] + a_hi[:, ne:] + _dot(h_lo, wr_ref[...])[:, :ne]
        e = jnp.exp(lg - jnp.max(lg, axis=-1, keepdims=True))
        aff_ref[k] = _transpose_exact(e / jnp.sum(e, axis=-1, keepdims=True))


def _out_kernel(y_ref, *rest):
    _residual_router(jnp.concatenate([y_ref[k] for k in range(y_ref.shape[0])], axis=0), *rest)


def _out_delta_kernel(of_ref, ob_ref, z_ref, gout_ref, *rest):
    rows = []
    for k in range(of_ref.shape[0]):
        o = of_ref[k].astype(F32) + ob_ref[k].astype(F32)
        parts = []
        for hd in range(A_HEADS):
            sl = slice(hd * A_DV, (hd + 1) * A_DV)
            oh = o[:, sl]
            ms = jnp.mean(oh * oh, axis=-1, keepdims=True)
            parts.append((oh * lax.rsqrt(ms + EPS) * gout_ref[...] * _silu(z_ref[k, :, sl].astype(F32))).astype(BF16))
        rows.append(jnp.concatenate(parts, axis=1))
    _residual_router(jnp.concatenate(rows, axis=0), *rest)


def _out_proj(pre, w_out, xs, mod, g_ffn, wr, nct_m, delta):
    bn, t, d = xs.shape
    k = w_out.shape[0]
    ne = wr.shape[1] // 2
    nb = _batch_group(bn)
    tok = lambda width: pl.BlockSpec((nb, TM, width), lambda b, i: (b, i, 0))
    full = lambda shape: pl.BlockSpec(shape, lambda b, i: (0,) * len(shape))
    if delta:
        of, ob, z, gout = pre
        head_specs = [tok(k), tok(k), tok(k), full((1, A_DV))]
        head_args = (of, ob, z, gout.reshape(1, A_DV).astype(F32))
        body = _out_delta_kernel
    else:
        head_specs = [tok(k)]
        head_args = (pre,)
        body = _out_kernel
    return pl.pallas_call(
        body,
        grid=(bn // nb, t // TM),
        in_specs=head_specs + [full((k, d)), tok(d),
                               pl.BlockSpec((nb, 1, 6, d), lambda b, i: (b, _seg(i, nct_m), 0, 0)),
                               full((1, d)), full((d, 2 * ne))],
        out_specs=[tok(d), tok(d), pl.BlockSpec((nb, ne, TM), lambda b, i: (b, 0, i))],
        out_shape=[jax.ShapeDtypeStruct((bn, t, d), F32), jax.ShapeDtypeStruct((bn, t, d), BF16),
                   jax.ShapeDtypeStruct((bn, ne, t), F32)],
        compiler_params=_cp(("arbitrary", "arbitrary"), VMEM_LIMIT),
        name="out_delta" if delta else "out_proj",
    )(*head_args, w_out, xs, mod, g_ffn, wr)


def _kth_largest_bits(bits, k):
    def body(it, thr):
        cand = thr | jnp.left_shift(jnp.int32(1), 30 - it)
        cnt = jnp.sum((bits >= cand).astype(I32), axis=1, keepdims=True)
        return jnp.where(cnt >= k, cand, thr)
    return lax.fori_loop(0, 31, body, jnp.zeros((bits.shape[0], 1), I32))


def _select_kernel(aff_ref, posd_ref, offs_ref, gated_ref, *, l, t, cap_c, cap_x):
    a = aff_ref[0]
    ne = a.shape[0]
    bits = pltpu.bitcast(a, I32)
    upper = (_iota((LANES, LANES), 0) <= _iota((LANES, LANES), 1)).astype(BF16)
    nt = t // LANES
    sel = [None] * nt
    for s0, s1, cap in ((0, l, cap_c), (l, t, cap_x)):
        bseg = bits[:, s0:s1]
        thr = _kth_largest_bits(bseg, cap)
        gtf = jnp.where(bseg > thr, 1.0, 0.0)
        eqf = jnp.where(bseg == thr, 1.0, 0.0)
        need = cap - jnp.sum(gtf, axis=1, keepdims=True)
        run = jnp.zeros((ne, 1), F32)
        for j in range((s1 - s0) // LANES):
            ej = eqf[:, j * LANES:(j + 1) * LANES]
            inc = _dot(ej.astype(BF16), upper)
            keep = jnp.where(inc - ej + run < need, ej, 0.0)
            sel[s0 // LANES + j] = jnp.maximum(gtf[:, j * LANES:(j + 1) * LANES], keep)
            run = run + inc[:, LANES - 1:LANES]
    run = jnp.zeros((ne, 1), F32)
    offs = jnp.zeros((ne, LANES), I32)
    lane = _iota((ne, LANES), 1)
    for j in range(nt):
        sj = sel[j]
        inc = _dot(sj.astype(BF16), upper)
        pos = jnp.where(sj > 0.0, inc - sj + run, -1.0)
        posd_ref[0, j] = pos.astype(I32)
        offs = jnp.where(lane == j, run.astype(I32), offs)
        gated_ref[0, j] = jnp.where(sj > 0.0, a[:, j * LANES:(j + 1) * LANES], 0.0)
        run = run + inc[:, LANES - 1:LANES]
    offs_ref[0] = jnp.where(lane == nt, run.astype(I32), offs)


def _select(aff, l, cap_c, cap_x):
    bn, ne, t = aff.shape
    nt = t // LANES
    return pl.pallas_call(
        functools.partial(_select_kernel, l=l, t=t, cap_c=cap_c, cap_x=cap_x),
        grid=(bn,),
        in_specs=[pl.BlockSpec((1, ne, t), lambda b: (b, 0, 0))],
        out_specs=[pl.BlockSpec((1, nt, ne, LANES), lambda b: (b, 0, 0, 0)),
                   pl.BlockSpec((1, ne, LANES), lambda b: (b, 0, 0)),
                   pl.BlockSpec((1, nt, ne, LANES), lambda b: (b, 0, 0, 0))],
        out_shape=[jax.ShapeDtypeStruct((bn, nt, ne, LANES), I32), jax.ShapeDtypeStruct((bn, ne, LANES), I32),
                   jax.ShapeDtypeStruct((bn, nt, ne, LANES), F32)],
        compiler_params=_cp(("arbitrary",), VMEM_LIMIT),
        name="route_select",
    )(aff)


GATHER_TOK = 2 * TQ
GATHER_WIN = 64
EXPERT_GROUP = 4
GATHER_UNROLL = 4
COMBINE_TOK = 2 * TQ
COMBINE_WIN = COMBINE_TOK + 16
PACK_WIN = 64
EXPERT_VMEM_LIMIT = 60 * 1024 * 1024


def _gather_group(offs_ref, h_ref, pos_ref, gate_ref, xg_ref, gr_ref, b, e0, ne, nt):
    xg_ref[...] = jnp.zeros(xg_ref.shape, BF16)
    gr_ref[...] = jnp.zeros(gr_ref.shape, F32)
    tpt = GATHER_TOK // LANES
    riota = _iota((GATHER_WIN, GATHER_TOK), 0)
    stacked = (EXPERT_GROUP * GATHER_WIN, ne)
    pick = _iota(stacked, 1) == e0 + _iota(stacked, 0) // GATHER_WIN

    def bounds(j, k):
        base = (b * ne + e0 + k) * LANES
        aoff = (offs_ref[base + j * tpt] // 16) * 16
        return aoff, (offs_ref[base + (j + 1) * tpt] - aoff + GATHER_WIN - 1) // GATHER_WIN

    def windows(j, w):
        tok0 = j * GATHER_TOK
        tok = pl.ds(tok0 if isinstance(j, int) else pl.multiple_of(tok0, GATHER_TOK), GATHER_TOK)
        starts, hots = [], []
        for k in range(EXPERT_GROUP):
            aoff, nwin = bounds(j, k)
            live = w < jnp.maximum(nwin, 1)
            pos = jnp.concatenate([pos_ref[0, j * tpt + c, pl.ds(e0 + k, 1), :] for c in range(tpt)], axis=1)
            first = jnp.where(live, aoff + w * GATHER_WIN, -(1 << 20))
            hots.append(jnp.where(riota == pos - first, 1.0, 0.0).astype(BF16))
            starts.append(pl.multiple_of(aoff + jnp.where(live, w, 0) * GATHER_WIN, 16))
        lhs = jnp.concatenate(hots, axis=0)
        rows = _dot(lhs, h_ref[0, tok, :])
        gd = jnp.concatenate([gate_ref[0, j * tpt + c] for c in range(tpt)], axis=1)
        g1 = gd.astype(BF16)
        r1 = gd - g1.astype(F32)
        g2 = r1.astype(BF16)
        g3 = (r1 - g2.astype(F32)).astype(BF16)
        gall = _dot_nt(lhs, g1) + _dot_nt(lhs, g2) + _dot_nt(lhs, g3)
        return starts, rows, jnp.sum(jnp.where(pick, gall, 0.0), axis=1, keepdims=True)

    def add(starts, rows, gcol):
        for k in range(EXPERT_GROUP):
            rs = slice(k * GATHER_WIN, (k + 1) * GATHER_WIN)
            dst = pl.ds(starts[k], GATHER_WIN)
            xg_ref[k, dst, :] = (xg_ref[k, dst, :].astype(F32) + rows[rs]).astype(BF16)
            gr_ref[k, dst, :] += jnp.broadcast_to(gcol[rs], (GATHER_WIN, LANES))

    def overflow(j):
        most = bounds(j, 0)[1]
        for k in range(1, EXPERT_GROUP):
            most = jnp.maximum(most, bounds(j, k)[1])

        def extra(w, carry):
            add(*windows(j, w))
            return carry

        lax.fori_loop(1, most, extra, 0)

    def tiles(js):
        found = [windows(j, 0) for j in js]
        for f in found:
            add(*f)
        for j in js:
            overflow(j)

    def tile_group(jg, carry):
        tiles([GATHER_UNROLL * jg + i for i in range(GATHER_UNROLL)])
        return carry

    ntile = nt // tpt
    lax.fori_loop(0, ntile // GATHER_UNROLL, tile_group, 0)
    if ntile % GATHER_UNROLL:
        tiles(list(range(ntile - ntile % GATHER_UNROLL, ntile)))


def _expert_kernel(offs_ref, h_ref, pos_ref, gate_ref, wgu_ref, wd_ref, y_ref, xg_ref, gr_ref, acc_ref, *, nt, r, fc):
    b = pl.program_id(0)
    e = pl.program_id(1)
    slot = e % EXPERT_GROUP

    @pl.when(slot == 0)
    def _():
        _gather_group(offs_ref, h_ref, pos_ref, gate_ref, xg_ref, gr_ref, b, e, pl.num_programs(1), nt)

    xg = xg_ref[slot, pl.ds(0, r), :]
    f = wd_ref.shape[1]
    for c in range(f // fc):
        g = _dot(xg, wgu_ref[0, :, c * fc:(c + 1) * fc])
        u = _dot(xg, wgu_ref[0, :, f + c * fc:f + (c + 1) * fc])
        part = _dot((_silu(g) * u).astype(BF16), wd_ref[0, c * fc:(c + 1) * fc, :])
        if c == 0:
            acc_ref[...] = part
        else:
            acc_ref[...] += part
    y_ref[0, 0] = (acc_ref[...] * gr_ref[slot, pl.ds(0, r), :][:, 0:1]).astype(BF16)


def _experts(offs, hf, posd, gated, wgu, wd, r):
    bn, t, d = hf.shape
    ne, _, f2 = wgu.shape
    nt = t // TQ
    rows = r + GATHER_WIN
    assert ne % EXPERT_GROUP == 0 and rows % 16 == 0
    return pl.pallas_call(
        functools.partial(_expert_kernel, nt=nt, r=r, fc=min(256, f2 // 2)),
        grid_spec=pltpu.PrefetchScalarGridSpec(
            num_scalar_prefetch=1,
            grid=(bn, ne),
            in_specs=[pl.BlockSpec((1, t, d), lambda b, e, o: (b, 0, 0), pipeline_mode=pl.Buffered(1)),
                      pl.BlockSpec((1, nt, ne, LANES), lambda b, e, o: (b, 0, 0, 0)),
                      pl.BlockSpec((1, nt, ne, LANES), lambda b, e, o: (b, 0, 0, 0)),
                      pl.BlockSpec((1, d, f2), lambda b, e, o: (e, 0, 0)),
                      pl.BlockSpec((1, f2 // 2, d), lambda b, e, o: (e, 0, 0))],
            out_specs=pl.BlockSpec((1, 1, r, d), lambda b, e, o: (b, e, 0, 0)),
            scratch_shapes=[pltpu.VMEM((EXPERT_GROUP, rows, d), BF16), pltpu.VMEM((EXPERT_GROUP, rows, LANES), F32),
                            pltpu.VMEM((r, d), F32)]),
        out_shape=jax.ShapeDtypeStruct((bn, ne, r, d), BF16),
        compiler_params=_cp(("arbitrary", "arbitrary"), EXPERT_VMEM_LIMIT),
        name="experts",
    )(offs, hf, posd, gated, wgu, wd)


def _combine_kernel(offs_ref, y_ref, pos_ref, x_ref, mod_ref, o_ref, *, r, win, pack):
    b = pl.program_id(0)
    j = pl.program_id(1)
    ne = y_ref.shape[1]
    tpt = COMBINE_TOK // LANES
    ident = (_iota((LANES, LANES), 0) == _iota((LANES, LANES), 1)).astype(BF16)
    cols = []
    for c in range(tpt):
        p_hi, p_lo = _split2(pos_ref[0, c].astype(F32))
        cols.append((_dot_nt(ident, p_hi) + _dot_nt(ident, p_lo)).astype(I32))
    pc = jnp.concatenate(cols, axis=0)
    offs = [offs_ref[(b * ne + e) * LANES + j * tpt] for e in range(ne)]
    ends = [offs_ref[(b * ne + e) * LANES + (j + 1) * tpt] for e in range(ne)]
    los = [jnp.minimum((offs[e] // 16) * 16, r - pack) for e in range(ne)]
    fits = ends[0] - los[0] <= pack
    for e in range(1, ne):
        fits = fits & (ends[e] - los[e] <= pack)
    gate_mod = mod_ref[0, 0][5:6]

    @pl.when(fits)
    def _():
        lane_e = _iota((1, ne), 1)
        lo_row = jnp.zeros((1, ne), I32)
        for e in range(ne):
            lo_row = jnp.where(lane_e == e, los[e], lo_row)
        rel = pc - lo_row
        rel = jnp.where((pc >= 0) & (rel >= 0) & (rel < pack), rel, -1).astype(F32).astype(BF16)
        spread = (_iota((ne, ne * pack), 0) == _iota((ne, ne * pack), 1) // pack).astype(BF16)
        want = (_iota((COMBINE_TOK, ne * pack), 1) % pack).astype(F32)
        onehot = jnp.where(_dot(rel, spread) == want, 1.0, 0.0).astype(BF16)
        ycat = jnp.concatenate([y_ref[0, e, pl.ds(pl.multiple_of(los[e], 16), pack), :] for e in range(ne)], axis=0)
        o_ref[0] = x_ref[0] + gate_mod * _dot(onehot, ycat)

    @pl.when(jnp.logical_not(fits))
    def _():
        liota = _iota((COMBINE_TOK, win), 1)
        acc = jnp.zeros(x_ref.shape[1:], F32)
        for e in range(ne):
            aoff = pl.multiple_of(jnp.minimum((offs[e] // 16) * 16, r - win), 16)
            onehot = jnp.where(liota == pc[:, e:e + 1] - aoff, 1.0, 0.0).astype(BF16)
            acc = acc + _dot(onehot, y_ref[0, e, pl.ds(aoff, win), :])
        o_ref[0] = x_ref[0] + gate_mod * acc


def _combine(offs, y, posd, xs, mod, l, latent_only):
    bn, t, d = xs.shape
    ne, r = y.shape[1], y.shape[2]
    nct = l // COMBINE_TOK
    tpt = COMBINE_TOK // LANES
    win = min(COMBINE_WIN, r)
    pack = min(PACK_WIN, r)
    assert (r - win) % 16 == 0 and (r - pack) % 16 == 0 and l % COMBINE_TOK == 0 and t % COMBINE_TOK == 0
    tok = pl.BlockSpec((1, COMBINE_TOK, d), lambda b, j, o: (b, j, 0))
    if latent_only:
        out_spec = pl.BlockSpec((1, COMBINE_TOK, d), lambda b, j, o: (b, jnp.maximum(j - nct, 0), 0))
        out_rows = t - l
    else:
        out_spec, out_rows = tok, t
    return pl.pallas_call(
        functools.partial(_combine_kernel, r=r, win=win, pack=pack),
        grid_spec=pltpu.PrefetchScalarGridSpec(
            num_scalar_prefetch=1,
            grid=(bn, t // COMBINE_TOK),
            in_specs=[pl.BlockSpec((1, ne, r, d), lambda b, j, o: (b, 0, 0, 0), pipeline_mode=pl.Buffered(1)),
                      pl.BlockSpec((1, tpt, ne, LANES), lambda b, j, o: (b, j, 0, 0)),
                      tok,
                      pl.BlockSpec((1, 1, 6, d), lambda b, j, o: (b, _seg(j, nct), 0, 0))],
            out_specs=out_spec),
        out_shape=jax.ShapeDtypeStruct((bn, out_rows, d), F32),
        compiler_params=_cp(("arbitrary", "arbitrary"), VMEM_LIMIT),
        name="moe_combine",
    )(offs, y, posd, xs, mod)


def _moe(xs, hf, aff, mod, wgu, wd, l, latent_only):
    bn, t, _ = xs.shape
    cap_c = EC_CAPACITY * l // N_EXPERTS
    cap_x = EC_CAPACITY * (t - l) // N_EXPERTS
    posd, offs, gated = _select(aff, l, cap_c, cap_x)
    offs = offs.reshape(-1)
    y = _experts(offs, hf, posd, gated, wgu, wd, cap_c + cap_x)
    return _combine(offs, y, posd, xs, mod, l, latent_only)


def _hi_lo(w):
    hi = w.astype(BF16)
    return jnp.stack([hi, (w - hi.astype(F32)).astype(BF16)])


def _mixer_delta(xs, mod, g_mix, w_in, conv_w, a_log, dt_bias, l, hb=8):
    nqkvz = 2 * A_HEADS * A_DK + 2 * A_HEADS * A_DV
    wab = w_in[:, nqkvz:]
    p, z, gc, gct, beta = _proj_delta(xs, mod, g_mix, w_in[:, :nqkvz].astype(BF16),
                                      jnp.concatenate(list(_hi_lo(wab)), axis=1), a_log, dt_bias, l // TM)
    qkv = _delta_conv(p, conv_w, l)
    of, ob = _delta_scan(_delta_prep(qkv, gc, gct, beta, hb), l, hb)
    return of, ob, z


def _mixer_swa(xs, mod, g_mix, w_in, qn, kn, sink, cosf, sinf, l):
    nq = B_QHEADS * B_HD
    nk = B_KVHEADS * B_HD
    dup = lambda w: jnp.concatenate([w.reshape(-1, B_KVHEADS, 1, B_HD)] * 2, axis=2).reshape(-1, 2 * nk)
    w = jnp.concatenate([w_in[:, :nq], dup(w_in[:, nq:nq + nk]), dup(w_in[:, nq + nk:])], axis=1).astype(BF16)
    gain = jnp.concatenate([jnp.tile(qn, B_QHEADS) * (B_HD ** -0.5), jnp.tile(kn, 2 * B_KVHEADS)])[None].astype(F32)
    qk, v2 = _proj_qk(xs, mod, g_mix, w, gain, cosf, sinf, nq + 2 * nk, l // TM)
    return _swa_attention(qk, v2, sink, l)


def _mixer_diff(xs, mod, g_mix, w_in, qn, kn, lam_vecs, g_sub, cosf, sinf, l, lam_init):
    nqk = C_HEADS * 2 * C_HD
    gain = jnp.concatenate([jnp.tile(qn, 2 * C_HEADS) * (C_HD ** -0.5 * LOG2E),
                            jnp.tile(kn, 2 * C_HEADS)])[None].astype(F32)
    qk, v = _proj_qk(xs, mod, g_mix, w_in.astype(BF16), gain, cosf, sinf, 2 * nqk, l // TM)
    return _diff_attention(qk, v, lam_vecs, g_sub, l, lam_init)


def kernel(x, c, ctx, c_ctx, w_ada, b_ada, g_mix, g_ffn, a_w_in, a_conv, a_log, a_dt_bias, a_g_out, a_w_out,
           b_w_in, b_q_norm, b_k_norm, b_sink, b_w_out, c_w_in, c_q_norm, c_k_norm, c_lambda, c_g_sub, c_w_out,
           w_router, w_gate_up, w_down):
    depth = w_ada.shape[0]
    n = x.shape[1]
    l = ctx.shape[1]
    assert l % TM == 0 and n % TM == 0
    xs = jnp.concatenate([ctx, x], axis=1)
    mods = _adaln(c, c_ctx, w_ada, b_ada)
    cos_b, sin_b = _rope_tables(n, l, B_HD)
    cos_c, sin_c = _rope_tables(n, l, C_HD)
    nct_m = l // TM
    for layer in range(depth):
        kind, j = layer % N_MIXERS, layer // N_MIXERS
        mod = mods[layer]
        gm = g_mix[layer][None].astype(F32)
        gf = g_ffn[layer][None].astype(F32)
        wr = jnp.concatenate(list(_hi_lo(w_router[layer])), axis=1)
        if kind == 0:
            pre = _mixer_delta(xs, mod, gm, a_w_in[j], a_conv[j], a_log[j], a_dt_bias[j], l)
            xs, hf, aff = _out_proj(pre + (a_g_out[j],), a_w_out[j].astype(BF16), xs, mod, gf, wr, nct_m, True)
        elif kind == 1:
            pre = _mixer_swa(xs, mod, gm, b_w_in[j], b_q_norm[j], b_k_norm[j], b_sink[j], cos_b, sin_b, l)
            xs, hf, aff = _out_proj(pre, b_w_out[j].astype(BF16), xs, mod, gf, wr, nct_m, False)
        else:
            lam_init = 0.8 - 0.6 * math.exp(-0.3 * layer)
            pre = _mixer_diff(xs, mod, gm, c_w_in[j], c_q_norm[j], c_k_norm[j], c_lambda[j], c_g_sub[j],
                              cos_c, sin_c, l, lam_init)
            xs, hf, aff = _out_proj(pre, c_w_out[j].astype(BF16), xs, mod, gf, wr, nct_m, False)
        xs = _moe(xs, hf, aff, mod, w_gate_up[layer].astype(BF16), w_down[layer].astype(BF16), l, layer == depth - 1)
    return xs
```
